```python
import math
import jax, jax.numpy as jnp
from jax import lax
import numpy as np

D_MODEL = 2048
BATCH = 2
SEQ = 8192
DEPTH = 1
DEC_BATCH = 2
DEC_SEQ = 16384
PAST_LEN = 128

GDN_QK_HEADS = 16
GDN_V_HEADS = 32
GDN_DK = 128
GDN_DV = 128
GDN_KEY_W = GDN_QK_HEADS * GDN_DK
GDN_VAL_W = GDN_V_HEADS * GDN_DV
GDN_CONV_CH = 2 * GDN_KEY_W + GDN_VAL_W
GDN_GATE_W = 4 * GDN_V_HEADS
GDN_CONV_K = 5
GDN_CHUNK = 64
DIFF_HEADS = D_MODEL // 128 // 2
DIFF_HD = 128
DIFF_QK_W = 2 * DIFF_HEADS * DIFF_HD
DIFF_VAL_W = DIFF_HEADS * 2 * DIFF_HD
ATTN_Q_BLOCK = 128
N_EXPERTS = 256
TOP_K = 8
N_GROUPS = 8
TOPK_GROUPS = 4
EXPERT_FF = 512
SHARED_FF = 512
ROUTED_SCALE = 2.5
MOE_BLOCK = 128
IN_SPLITS = (GDN_CONV_CH, GDN_VAL_W, GDN_GATE_W, DIFF_QK_W, DIFF_QK_W, DIFF_VAL_W, 2 * D_MODEL)
IN_COLS = GDN_CONV_CH + GDN_VAL_W + GDN_GATE_W + 2 * DIFF_QK_W + DIFF_VAL_W + 2 * D_MODEL
DN_ALPHA = (2 * DEPTH) ** 0.25
DN_BETA = (8 * DEPTH) ** -0.25
LN_EPS = 1e-5
RMS_EPS = 1e-6

kernel_name = 'gdn_diffattn_moe_deepnorm_encoder'


def layer_norm(x, g, b):
    xf = x.astype(jnp.float32)
    mu = jnp.mean(xf, axis=-1, keepdims=True)
    var = jnp.mean(jnp.square(xf - mu), axis=-1, keepdims=True)
    return ((xf - mu) * lax.rsqrt(var + LN_EPS) * g.astype(jnp.float32) + b.astype(jnp.float32)).astype(x.dtype)


def rms_norm(xf, eps):
    return xf * lax.rsqrt(jnp.mean(jnp.square(xf), axis=-1, keepdims=True) + eps)


def l2_normalize(xf):
    return xf * lax.rsqrt(jnp.sum(jnp.square(xf), axis=-1, keepdims=True) + RMS_EPS)


def centred_depthwise_conv(u, w):
    pad = (GDN_CONV_K - 1) // 2
    return lax.conv_general_dilated(u, w[:, None, :].astype(u.dtype), window_strides=(1,), padding=[(pad, pad)],
                                    dimension_numbers=('NWC', 'WIO', 'NWC'), feature_group_count=u.shape[-1])


def chunk_gated_delta(q, k, v, log_decay, beta):
    B, T, H, DK = q.shape
    DV = v.shape[-1]
    C = GDN_CHUNK
    NC = T // C

    def to_chunks(a):
        a = a.reshape((B, NC, C, H) + a.shape[3:])
        return jnp.moveaxis(a, (1, 3), (0, 2))

    qc = to_chunks(q * (DK ** -0.5))
    kc = to_chunks(k)
    vc = to_chunks(v)
    bc = to_chunks(beta)
    gc = jnp.cumsum(to_chunks(log_decay), axis=-1)
    causal = jnp.tril(jnp.ones((C, C), dtype=bool))
    strict = jnp.tril(jnp.ones((C, C), dtype=bool), -1)
    decay = jnp.exp(jnp.where(causal, gc[..., :, None] - gc[..., None, :], -jnp.inf))
    kb = kc * bc[..., None]
    lower = jnp.where(strict, jnp.einsum('nbhcd,nbhsd->nbhcs', kb, kc) * decay, 0.0)
    eye = jnp.eye(C, dtype=jnp.float32)
    tmat = lax.linalg.triangular_solve(eye + lower, jnp.broadcast_to(eye, lower.shape),
                                       left_side=True, lower=True, unit_diagonal=True)
    u = jnp.einsum('nbhcs,nbhse->nbhce', tmat, vc * bc[..., None])
    w = jnp.einsum('nbhcs,nbhsd->nbhcd', tmat, kb * jnp.exp(gc)[..., None])
    qk = jnp.einsum('nbhcd,nbhsd->nbhcs', qc, kc) * decay
    q_dec = qc * jnp.exp(gc)[..., None]
    k_dec = kc * jnp.exp(gc[..., -1:] - gc)[..., None]
    chunk_decay = jnp.exp(gc[..., -1])

    def step(S, xs):
        u_n, w_n, qk_n, q_n, k_n, d_n = xs
        v_new = u_n - jnp.einsum('bhcd,bhde->bhce', w_n, S)
        o_n = jnp.einsum('bhcd,bhde->bhce', q_n, S) + jnp.einsum('bhcs,bhse->bhce', qk_n, v_new)
        S = S * d_n[..., None, None] + jnp.einsum('bhcd,bhce->bhde', k_n, v_new)
        return S, o_n

    S0 = jnp.zeros((B, H, DK, DV), jnp.float32)
    _, o = lax.scan(step, S0, (u, w, qk, q_dec, k_dec, chunk_decay))
    return jnp.moveaxis(o, (0, 2), (1, 3)).reshape(B, T, H, DV)


def gated_deltanet_branch(qkv, z, ab, conv_w, a_log, dt_bias, norm_w):
    B, T, _ = qkv.shape
    act = jax.nn.silu(centred_depthwise_conv(qkv, conv_w)).astype(jnp.float32)
    q, k, v = jnp.split(act, [GDN_KEY_W, 2 * GDN_KEY_W], axis=-1)
    rep = GDN_V_HEADS // GDN_QK_HEADS
    q = jnp.repeat(l2_normalize(q.reshape(B, T, GDN_QK_HEADS, GDN_DK)), rep, axis=2)
    k = jnp.repeat(l2_normalize(k.reshape(B, T, GDN_QK_HEADS, GDN_DK)), rep, axis=2)
    v = v.reshape(B, T, GDN_V_HEADS, GDN_DV)
    ab = ab.astype(jnp.float32).reshape(B, T, 2, 2, GDN_V_HEADS)
    log_decay = -jnp.exp(a_log.astype(jnp.float32)) * jax.nn.softplus(ab[:, :, :, 0, :] + dt_bias.astype(jnp.float32))
    beta = jax.nn.sigmoid(ab[:, :, :, 1, :])
    o_fwd = chunk_gated_delta(q, k, v, log_decay[:, :, 0], beta[:, :, 0])
    flip = lambda a: jnp.flip(a, axis=1)
    o_bwd = flip(chunk_gated_delta(flip(q), flip(k), flip(v), flip(log_decay[:, :, 1]), flip(beta[:, :, 1])))
    o = rms_norm(o_fwd + o_bwd, RMS_EPS) * norm_w.astype(jnp.float32) * jax.nn.silu(
        z.astype(jnp.float32).reshape(B, T, GDN_V_HEADS, GDN_DV))
    return o.reshape(B, T, GDN_VAL_W).astype(z.dtype)


def diff_attention_branch(dq, dk, dv, lam, subln_w, layer_idx):
    B, T, _ = dq.shape
    H, hd = DIFF_HEADS, DIFF_HD
    q = dq.reshape(B, T, H, 2, hd)
    k = dk.reshape(B, T, H, 2, hd)
    v = dv.reshape(B, T, H, 2 * hd)
    lam_init = 0.8 - 0.6 * math.exp(-0.3 * layer_idx)
    lf = lam.astype(jnp.float32)
    lam_full = jnp.exp(jnp.sum(lf[0] * lf[1])) - jnp.exp(jnp.sum(lf[2] * lf[3])) + lam_init
    slopes = 2.0 ** (-8.0 * jnp.arange(1, H + 1, dtype=jnp.float32) / H)
    key_pos = jnp.arange(T, dtype=jnp.float32)
    scale = hd ** -0.5

    def block(i):
        start = i * ATTN_Q_BLOCK
        q_blk = lax.dynamic_slice_in_dim(q, start, ATTN_Q_BLOCK, axis=1)
        s = jnp.einsum('bqhmd,bkhmd->bhmqk', q_blk, k, preferred_element_type=jnp.float32) * scale
        q_pos = (start + jnp.arange(ATTN_Q_BLOCK)).astype(jnp.float32)
        dist = jnp.abs(q_pos[:, None] - key_pos[None, :])
        p = jax.nn.softmax(s - slopes[None, :, None, None, None] * dist, axis=-1)
        a = p[:, :, 0] - lam_full * p[:, :, 1]
        return jnp.einsum('bhqk,bkhe->bqhe', a.astype(v.dtype), v, preferred_element_type=jnp.float32)

    o = lax.map(block, jnp.arange(T // ATTN_Q_BLOCK))
    o = o.transpose(1, 0, 2, 3, 4).reshape(B, T, H, 2 * hd)
    o = rms_norm(o, LN_EPS) * subln_w.astype(jnp.float32) * (1.0 - lam_init)
    return o.reshape(B, T, DIFF_VAL_W).astype(dq.dtype)


def token_mixer(x, layer_idx, w_in, conv_w, a_log, dt_bias, norm_w, lam, subln_w, w_branch_gdn, w_branch_diff, w_out):
    proj = x @ w_in
    offsets = np.cumsum(IN_SPLITS)[:-1].tolist()
    qkv, z, ab, dq, dk, dv, gates = jnp.split(proj, offsets, axis=-1)
    o_gdn = gated_deltanet_branch(qkv, z, ab, conv_w, a_log, dt_bias, norm_w)
    o_diff = diff_attention_branch(dq, dk, dv, lam, subln_w, layer_idx)
    g_gdn, g_diff = jnp.split(jax.nn.sigmoid(gates), 2, axis=-1)
    merged = g_gdn * (o_gdn @ w_branch_gdn) + g_diff * (o_diff @ w_branch_diff)
    return merged @ w_out


def swiglu(x, wg, wu, wd):
    return (jax.nn.silu(x @ wg) * (x @ wu)) @ wd


def routed_experts(xf, top_e, gate_w, w_gate, w_up, w_down):
    N = xf.shape[0]
    A = N * TOP_K
    n_blocks = (A + N_EXPERTS * (MOE_BLOCK - 1) + MOE_BLOCK - 1) // MOE_BLOCK
    e_flat = top_e.reshape(-1)
    tok_flat = jnp.arange(A, dtype=jnp.int32) // TOP_K
    w_flat = gate_w.reshape(-1)
    order = jnp.argsort(e_flat)
    e_sorted = e_flat[order]
    counts = jnp.zeros((N_EXPERTS,), jnp.int32).at[e_flat].add(1)
    padded = (counts + MOE_BLOCK - 1) // MOE_BLOCK * MOE_BLOCK
    pad_end = jnp.cumsum(padded)
    pad_start = pad_end - padded
    start = jnp.cumsum(counts) - counts
    dest = pad_start[e_sorted] + jnp.arange(A, dtype=jnp.int32) - start[e_sorted]
    slot_tok = jnp.zeros((n_blocks * MOE_BLOCK,), jnp.int32).at[dest].set(tok_flat[order])
    slot_w = jnp.zeros((n_blocks * MOE_BLOCK,), jnp.float32).at[dest].set(w_flat[order])
    block_start = jnp.arange(n_blocks, dtype=jnp.int32) * MOE_BLOCK
    block_e = jnp.minimum(jnp.sum(block_start[:, None] >= pad_end[None, :], axis=1), N_EXPERTS - 1)

    def step(y, blk):
        tok, wt, e = blk
        out = swiglu(xf[tok], w_gate[e], w_up[e], w_down[e])
        return y.at[tok].add((out * wt[:, None]).astype(y.dtype)), None

    y, _ = lax.scan(step, jnp.zeros_like(xf),
                    (slot_tok.reshape(n_blocks, MOE_BLOCK), slot_w.reshape(n_blocks, MOE_BLOCK), block_e))
    return y


def moe_ffn(h, router_w, router_bias, w_gate, w_up, w_down, sh_gate, sh_up, sh_down):
    B, T, D = h.shape
    N = B * T
    xf = h.reshape(N, D)
    scores = jax.nn.sigmoid(jnp.matmul(xf, router_w, preferred_element_type=jnp.float32))
    choice = scores + router_bias.astype(jnp.float32)
    group_scores = jnp.sum(lax.top_k(choice.reshape(N, N_GROUPS, N_EXPERTS // N_GROUPS), 2)[0], axis=-1)
    _, top_groups = lax.top_k(group_scores, TOPK_GROUPS)
    group_mask = jnp.any(top_groups[:, :, None] == jnp.arange(N_GROUPS)[None, None, :], axis=1)
    expert_mask = jnp.repeat(group_mask, N_EXPERTS // N_GROUPS, axis=1)
    _, top_e = lax.top_k(jnp.where(expert_mask, choice, -jnp.inf), TOP_K)
    gw = jnp.take_along_axis(scores, top_e, axis=1)
    gw = gw / (jnp.sum(gw, axis=-1, keepdims=True) + 1e-20) * ROUTED_SCALE
    routed = routed_experts(xf, top_e, gw, w_gate, w_up, w_down)
    shared = swiglu(xf, sh_gate, sh_up, sh_down)
    return (routed + shared).reshape(B, T, D)


def encoder_layer(x, layer_idx, w_in, conv_w, a_log, dt_bias, norm_w, lam, subln_w, w_branch_gdn, w_branch_diff,
                  w_out, ln1_g, ln1_b, router_w, router_bias, w_gate, w_up, w_down, sh_gate, sh_up, sh_down,
                  ln2_g, ln2_b):
    mix = token_mixer(x, layer_idx, w_in, conv_w, a_log, dt_bias, norm_w, lam, subln_w,
                      w_branch_gdn, w_branch_diff, w_out)
    h = layer_norm(DN_ALPHA * x + mix, ln1_g, ln1_b)
    f = moe_ffn(h, router_w, router_bias, w_gate, w_up, w_down, sh_gate, sh_up, sh_down)
    return layer_norm(DN_ALPHA * h + f, ln2_g, ln2_b)


def setup_inputs(seed: int = 0) -> dict:
    key = jax.random.key(seed)
    ks = jax.random.split(key, 24)
    f32 = jnp.float32
    L = DEPTH

    def nrm(k, shape, scale):
        return jax.random.normal(k, shape, f32) * scale

    col_scale = jnp.concatenate([
        jnp.ones((2 * GDN_KEY_W,), f32), jnp.full((GDN_VAL_W,), DN_BETA, f32),
        jnp.ones((GDN_VAL_W + GDN_GATE_W + 2 * DIFF_QK_W,), f32), jnp.full((DIFF_VAL_W,), DN_BETA, f32),
        jnp.ones((2 * D_MODEL,), f32)])
    return {
        'x_prompt': nrm(ks[0], (BATCH, SEQ, D_MODEL), 1.0),
        'x_sample': nrm(ks[1], (DEC_BATCH, DEC_SEQ, D_MODEL), 1.0),
        'w_in': nrm(ks[2], (L, D_MODEL, IN_COLS), D_MODEL ** -0.5) * col_scale,
        'gdn_conv_w': nrm(ks[3], (L, GDN_CONV_K, GDN_CONV_CH), GDN_CONV_K ** -0.5),
        'gdn_a_log': jnp.log(jax.random.uniform(ks[4], (L, 2, GDN_V_HEADS), f32, 1.0, 16.0)),
        'gdn_dt_bias': 1.0 + nrm(ks[5], (L, 2, GDN_V_HEADS), 0.1),
        'gdn_norm_w': 1.0 + nrm(ks[6], (L, GDN_DV), 0.05),
        'diff_lambda': nrm(ks[7], (L, 4, DIFF_HD), 0.1),
        'diff_subln_w': 1.0 + nrm(ks[8], (L, 2 * DIFF_HD), 0.05),
        'w_branch_gdn': nrm(ks[9], (L, GDN_VAL_W, D_MODEL), GDN_VAL_W ** -0.5),
        'w_branch_diff': nrm(ks[10], (L, DIFF_VAL_W, D_MODEL), DIFF_VAL_W ** -0.5),
        'w_out': nrm(ks[11], (L, D_MODEL, D_MODEL), DN_BETA * D_MODEL ** -0.5),
        'ln1_g': 1.0 + nrm(ks[12], (L, D_MODEL), 0.05),
        'ln1_b': nrm(ks[13], (L, D_MODEL), 0.02),
        'router_w': nrm(ks[14], (L, D_MODEL, N_EXPERTS), D_MODEL ** -0.5),
        'router_bias': nrm(ks[15], (L, N_EXPERTS), 0.01),
        'exp_w_gate': nrm(ks[16], (L, N_EXPERTS, D_MODEL, EXPERT_FF), D_MODEL ** -0.5),
        'exp_w_up': nrm(ks[17], (L, N_EXPERTS, D_MODEL, EXPERT_FF), D_MODEL ** -0.5),
        'exp_w_down': nrm(ks[18], (L, N_EXPERTS, EXPERT_FF, D_MODEL), DN_BETA * EXPERT_FF ** -0.5),
        'sh_w_gate': nrm(ks[19], (L, D_MODEL, SHARED_FF), D_MODEL ** -0.5),
        'sh_w_up': nrm(ks[20], (L, D_MODEL, SHARED_FF), D_MODEL ** -0.5),
        'sh_w_down': nrm(ks[21], (L, SHARED_FF, D_MODEL), DN_BETA * SHARED_FF ** -0.5),
        'ln2_g': 1.0 + nrm(ks[22], (L, D_MODEL), 0.05),
        'ln2_b': nrm(ks[23], (L, D_MODEL), 0.02),
    }


def reference(x_prompt, x_sample, w_in, gdn_conv_w, gdn_a_log, gdn_dt_bias, gdn_norm_w, diff_lambda, diff_subln_w,
              w_branch_gdn, w_branch_diff, w_out, ln1_g, ln1_b, router_w, router_bias, exp_w_gate, exp_w_up,
              exp_w_down, sh_w_gate, sh_w_up, sh_w_down, ln2_g, ln2_b):
    def trunk(x):
        for l in range(DEPTH):
            x = encoder_layer(x, l, w_in[l], gdn_conv_w[l], gdn_a_log[l], gdn_dt_bias[l], gdn_norm_w[l],
                              diff_lambda[l], diff_subln_w[l], w_branch_gdn[l], w_branch_diff[l], w_out[l],
                              ln1_g[l], ln1_b[l], router_w[l], router_bias[l], exp_w_gate[l], exp_w_up[l],
                              exp_w_down[l], sh_w_gate[l], sh_w_up[l], sh_w_down[l], ln2_g[l], ln2_b[l])
        return x

    y_prompt = trunk(x_prompt)
    y_sample = trunk(x_sample)
    return (y_prompt, y_sample)
```

```python
import functools
import math

import jax
import jax.numpy as jnp
from jax import lax
from jax.experimental import pallas as pl
from jax.experimental.pallas import tpu as pltpu

F32 = jnp.float32
BF16 = jnp.bfloat16
HIGHEST = lax.Precision.HIGHEST

D_MODEL = 2048
GDN_QK_HEADS = 16
GDN_V_HEADS = 32
GDN_DK = 128
GDN_DV = 128
GDN_KEY_W = GDN_QK_HEADS * GDN_DK
GDN_VAL_W = GDN_V_HEADS * GDN_DV
GDN_CONV_CH = 2 * GDN_KEY_W + GDN_VAL_W
GDN_GATE_W = 4 * GDN_V_HEADS
GDN_CONV_K = 5
GDN_CHUNK = 64
DIFF_HEADS = 8
DIFF_HD = 128
DIFF_QK_W = 2 * DIFF_HEADS * DIFF_HD
DIFF_VAL_W = DIFF_HEADS * 2 * DIFF_HD
N_GROUPS = 8
TOPK_GROUPS = 4
TOP_K = 8
ROUTED_SCALE = 2.5
DEPTH = 1
DN_ALPHA = (2 * DEPTH) ** 0.25
LN_EPS = 1e-5
RMS_EPS = 1e-6
NEG_BIG = -1e30

P_QKV = 0
P_Z = P_QKV + GDN_CONV_CH
P_DQ = P_Z + GDN_VAL_W
P_DK = P_DQ + DIFF_QK_W
P_DV = P_DK + DIFF_QK_W
P_GATES = P_DV + DIFF_VAL_W
P_COLS = P_GATES + 2 * D_MODEL

VMEM_LIMIT = 56 * 1024 * 1024


def _params(sem, vmem=VMEM_LIMIT):
    return pltpu.CompilerParams(dimension_semantics=sem, vmem_limit_bytes=vmem)


def _sigmoid(x):
    return 1.0 / (1.0 + jnp.exp(-x))


def _silu(x):
    return x * _sigmoid(x)


def _layer_norm(x, g, b):
    mu = jnp.mean(x, axis=-1, keepdims=True)
    xc = x - mu
    var = jnp.mean(xc * xc, axis=-1, keepdims=True)
    return xc * lax.rsqrt(var + LN_EPS) * g + b


def _proj_in_body(x_ref, w_ref, o_ref, xb_ref, *, tn):
    j = pl.program_id(1)

    @pl.when(j == 0)
    def _():
        xb_ref[...] = x_ref[...].astype(BF16)

    acc = jnp.dot(xb_ref[...], w_ref[...], preferred_element_type=F32)
    col = j * tn
    is_gate = col >= P_GATES
    is_dq = jnp.logical_and(col >= P_DQ, col < P_DK)
    scale = jnp.where(is_dq, DIFF_HD ** -0.5, 1.0).astype(F32)

    @pl.when(is_gate)
    def _():
        o_ref[...] = _sigmoid(acc).astype(o_ref.dtype)

    @pl.when(jnp.logical_not(is_gate))
    def _():
        o_ref[...] = (acc * scale).astype(o_ref.dtype)


def _proj_in(x, w, *, tm=1024, tn=512):
    n, k = x.shape
    m = w.shape[1]
    tm = min(tm, n)
    return pl.pallas_call(
        functools.partial(_proj_in_body, tn=tn),
        out_shape=jax.ShapeDtypeStruct((n, m), BF16),
        grid=(n // tm, m // tn),
        in_specs=[pl.BlockSpec((tm, k), lambda i, j: (i, 0)),
                  pl.BlockSpec((k, tn), lambda i, j: (0, j))],
        out_specs=pl.BlockSpec((tm, tn), lambda i, j: (i, j)),
        scratch_shapes=[pltpu.VMEM((tm, k), BF16)],
        compiler_params=_params(("parallel", "arbitrary")),
        name="proj_in",
    )(x, w)


def _gates_body(x_ref, w_ref, prm_ref, o_ref, *, tt):
    ab = jnp.dot(x_ref[...].astype(BF16), w_ref[...], preferred_element_type=F32)
    a_log = prm_ref[0:1, :]
    dt_bias = prm_ref[1:2, :]
    y = ab + dt_bias
    softplus = jnp.maximum(y, 0.0) + jnp.log(1.0 + jnp.exp(-jnp.abs(y)))
    log_decay = -jnp.exp(a_log) * softplus
    beta = _sigmoid(ab)
    ri = lax.broadcasted_iota(jnp.int32, (tt, tt), 0)
    ci = lax.broadcasted_iota(jnp.int32, (tt, tt), 1)
    same = (ri // GDN_CHUNK) == (ci // GDN_CHUNK)
    m_lo = jnp.where(same, jnp.where(ci <= ri, 1.0, 0.0), 0.0).astype(F32)
    m_up = jnp.where(same, jnp.where(ci >= ri, 1.0, 0.0), 0.0).astype(F32)
    c_lo = jnp.dot(m_lo, log_decay, precision=HIGHEST, preferred_element_type=F32)
    c_up = jnp.dot(m_up, log_decay, precision=HIGHEST, preferred_element_type=F32)
    lane = lax.broadcasted_iota(jnp.int32, (tt, GDN_GATE_W), 1)
    is_a = (lane % (2 * GDN_V_HEADS)) < GDN_V_HEADS
    is_bwd = lane >= 2 * GDN_V_HEADS
    o_ref[...] = jnp.where(is_a, jnp.where(is_bwd, c_up, c_lo), beta)


def _gdn_gates(x, w_ab, a_log, dt_bias, *, tt=512):
    n, k = x.shape
    tt = min(tt, n)
    zeros = jnp.zeros((GDN_V_HEADS,), F32)
    prm = jnp.zeros((8, GDN_GATE_W), F32)
    prm = prm.at[0].set(jnp.concatenate([a_log[0], zeros, a_log[1], zeros]).astype(F32))
    prm = prm.at[1].set(jnp.concatenate([dt_bias[0], zeros, dt_bias[1], zeros]).astype(F32))
    return pl.pallas_call(
        functools.partial(_gates_body, tt=tt),
        out_shape=jax.ShapeDtypeStruct((n, GDN_GATE_W), F32),
        grid=(n // tt,),
        in_specs=[pl.BlockSpec((tt, k), lambda i: (i, 0)),
                  pl.BlockSpec((k, GDN_GATE_W), lambda i: (0, 0)),
                  pl.BlockSpec((8, GDN_GATE_W), lambda i: (0, 0))],
        out_specs=pl.BlockSpec((tt, GDN_GATE_W), lambda i: (i, 0)),
        compiler_params=_params(("parallel",)),
        name="gdn_gates",
    )(x, w_ab, prm)


CONV_HALO = 16


def _any_equal(value, constants):
    hit = value == constants[0]
    for c in constants[1:]:
        hit = jnp.logical_or(hit, value == c)
    return hit


def _conv_body(prev_ref, cur_ref, next_ref, w_ref, o_ref, ext_ref, *, tt, tc, seq_starts, seq_ends):
    i = pl.program_id(0)
    j = pl.program_id(1)
    t0 = i * tt
    at_start = _any_equal(t0, seq_starts)
    at_end = _any_equal(t0 + tt, seq_ends)
    prev = prev_ref[...].astype(F32)[CONV_HALO - 8:, :]
    nxt = next_ref[...].astype(F32)[:8, :]
    ext_ref[0:8, :] = jnp.where(at_start, 0.0, prev)
    ext_ref[8:8 + tt, :] = cur_ref[...].astype(F32)
    ext_ref[8 + tt:16 + tt, :] = jnp.where(at_end, 0.0, nxt)
    pad = (GDN_CONV_K - 1) // 2
    acc = jnp.zeros((tt, tc), F32)
    for tap in range(GDN_CONV_K):
        acc = acc + ext_ref[pl.ds(8 - pad + tap, tt), :] * w_ref[tap:tap + 1, :]
    act = _silu(acc)
    col = j * tc
    is_qk = col < 2 * GDN_KEY_W
    q_scale = jnp.where(col < GDN_KEY_W, GDN_DK ** -0.5, 1.0).astype(F32)
    for s in range(tc // GDN_DK):
        a = act[:, s * GDN_DK:(s + 1) * GDN_DK]
        ss = jnp.sum(a * a, axis=-1, keepdims=True)
        normed = a * (lax.rsqrt(ss + RMS_EPS) * q_scale)
        o_ref[:, s * GDN_DK:(s + 1) * GDN_DK] = jnp.where(is_qk, normed, a).astype(o_ref.dtype)


def _gdn_conv(p, conv_w, seq_starts, seq_ends, *, tt=512, tc=512):
    n = p.shape[0]
    tt = min(tt, n)
    hb = tt // CONV_HALO
    n_halo = n // CONV_HALO
    return pl.pallas_call(
        functools.partial(_conv_body, tt=tt, tc=tc, seq_starts=seq_starts, seq_ends=seq_ends),
        out_shape=jax.ShapeDtypeStruct((n, GDN_CONV_CH), BF16),
        grid=(n // tt, GDN_CONV_CH // tc),
        in_specs=[pl.BlockSpec((CONV_HALO, tc), lambda i, j: (jnp.maximum(i * hb - 1, 0), j)),
                  pl.BlockSpec((tt, tc), lambda i, j: (i, j)),
                  pl.BlockSpec((CONV_HALO, tc), lambda i, j: (jnp.minimum((i + 1) * hb, n_halo - 1), j)),
                  pl.BlockSpec((GDN_CONV_K, tc), lambda i, j: (0, j))],
        out_specs=pl.BlockSpec((tt, tc), lambda i, j: (i, j)),
        scratch_shapes=[pltpu.VMEM((tt + 16, tc), F32)],
        compiler_params=_params(("parallel", "parallel")),
        name="gdn_conv",
    )(p, p, p, conv_w)


def _dot_f32(a, b):
    return jnp.dot(a, b, precision=HIGHEST, preferred_element_type=F32)


def _dot_bf16(a, b):
    return jnp.dot(a.astype(BF16), b.astype(BF16), preferred_element_type=F32)


def _gdn_scan_body(*refs, tt, nt, reverse, seq_starts, seq_ends):
    if reverse:
        q_ref, k_ref, v_ref, g_ref, b_ref, ofwd_ref, z_ref, nw_ref, o_ref, s_ref = refs
    else:
        q_ref, k_ref, v_ref, g_ref, b_ref, o_ref, s_ref = refs
    C = GDN_CHUNK
    t = pl.program_id(1)
    tok0 = ((nt - 1 - t) if reverse else t) * tt
    reset = _any_equal(tok0 + tt, seq_ends) if reverse else _any_equal(tok0, seq_starts)

    @pl.when(reset)
    def _():
        s_ref[...] = jnp.zeros_like(s_ref)

    ri = lax.broadcasted_iota(jnp.int32, (C, C), 0)
    ci = lax.broadcasted_iota(jnp.int32, (C, C), 1)
    eye = ri == ci
    causal = (ci >= ri) if reverse else (ci <= ri)
    strict = (ci > ri) if reverse else (ci < ri)
    last = 0 if reverse else C - 1
    n_chunks = tt // C
    order = range(n_chunks - 1, -1, -1) if reverse else range(n_chunks)
    for c in order:
        rows = slice(c * C, (c + 1) * C)
        q = q_ref[rows, :]
        k = k_ref[rows, :]
        kf = k.astype(F32)
        kt = kf.T
        ktb = kt.astype(BF16)
        kk = jnp.dot(k, ktb, preferred_element_type=F32)
        qk = jnp.dot(q, ktb, preferred_element_type=F32)
        qf = q.astype(F32)
        for hh in range(2):
            cols = slice(hh * GDN_DV, (hh + 1) * GDN_DV)
            g_row = g_ref[hh, :, rows]
            b_row = b_ref[hh, :, rows]
            g_col = jnp.sum(jnp.where(eye, g_row, 0.0), axis=1, keepdims=True)
            b_col = jnp.sum(jnp.where(eye, b_row, 0.0), axis=1, keepdims=True)
            g_last = g_row[:, last:last + 1]
            decay = jnp.exp(jnp.where(causal, g_col - g_row, NEG_BIG))
            low = jnp.where(strict, kk * decay, 0.0) * b_col
            eg_col = jnp.exp(g_col)
            v = v_ref[rows, cols].astype(F32)
            x = jnp.concatenate([v * b_col, kf * (b_col * eg_col)], axis=1)
            x = x - _dot_f32(low, x)
            pw = low
            for _ in range(int(math.log2(C)) - 1):
                pw = _dot_f32(pw, pw)
                x = x + _dot_f32(pw, x)
            u = x[:, :GDN_DV]
            w = x[:, GDN_DV:]
            state = s_ref[hh]
            state_b = state.astype(BF16)
            v_new = u - jnp.dot(w.astype(BF16), state_b, preferred_element_type=F32)
            v_new_b = v_new.astype(BF16)
            o = (jnp.dot((qf * eg_col).astype(BF16), state_b, preferred_element_type=F32)
                 + jnp.dot((qk * decay).astype(BF16), v_new_b, preferred_element_type=F32))
            kdt = kt * jnp.exp(g_last - g_row)
            s_ref[hh] = state * jnp.exp(g_last) + jnp.dot(kdt.astype(BF16), v_new_b,
                                                          preferred_element_type=F32)
            if reverse:
                tot = o + ofwd_ref[rows, cols]
                ms = jnp.mean(tot * tot, axis=-1, keepdims=True)
                z = z_ref[rows, cols].astype(F32)
                o_ref[rows, cols] = (tot * lax.rsqrt(ms + RMS_EPS) * nw_ref[...] * _silu(z)).astype(o_ref.dtype)
            else:
                o_ref[rows, cols] = o


def _gdn_scan(act, g_t, p, o_fwd, norm_w, seq_starts, seq_ends, *, reverse, tt=256):
    n = act.shape[0]
    tt = min(tt, n)
    nt = n // tt
    pairs = GDN_V_HEADS // 2
    dv2 = 2 * GDN_DV

    def tmap(t):
        return (nt - 1 - t) if reverse else t

    dirn = 1 if reverse else 0
    in_specs = [
        pl.BlockSpec((tt, GDN_DK), lambda h, t: (tmap(t), h)),
        pl.BlockSpec((tt, GDN_DK), lambda h, t: (tmap(t), GDN_QK_HEADS + h)),
        pl.BlockSpec((tt, dv2), lambda h, t: (tmap(t), 2 * GDN_KEY_W // dv2 + h)),
        pl.BlockSpec((2, 1, tt), lambda h, t: (dirn * GDN_V_HEADS + h, 0, tmap(t))),
        pl.BlockSpec((2, 1, tt), lambda h, t: (dirn * GDN_V_HEADS + pairs + h, 0, tmap(t))),
    ]
    args = [act, act, act, g_t, g_t]
    if reverse:
        in_specs += [
            pl.BlockSpec((tt, dv2), lambda h, t: (tmap(t), h)),
            pl.BlockSpec((tt, dv2), lambda h, t: (tmap(t), P_Z // dv2 + h)),
            pl.BlockSpec((1, GDN_DV), lambda h, t: (0, 0)),
        ]
        args += [o_fwd, p, norm_w.reshape(1, GDN_DV).astype(F32)]
        out_dtype = BF16
    else:
        out_dtype = F32
    return pl.pallas_call(
        functools.partial(_gdn_scan_body, tt=tt, nt=nt, reverse=reverse,
                          seq_starts=seq_starts, seq_ends=seq_ends),
        out_shape=jax.ShapeDtypeStruct((n, GDN_VAL_W), out_dtype),
        grid=(pairs, nt),
        in_specs=in_specs,
        out_specs=pl.BlockSpec((tt, dv2), lambda h, t: (tmap(t), h)),
        scratch_shapes=[pltpu.VMEM((2, GDN_DK, GDN_DV), F32)],
        compiler_params=_params(("parallel", "arbitrary")),
        name="gdn_scan_bwd" if reverse else "gdn_scan_fwd",
    )(*args)


def _attn_body(slope_ref, lam_ref, sw_ref, q_ref, k_ref, v_ref, o_ref, m_ref, l_ref, acc_ref,
               *, tq, tk, nk, lam_init):
    h = pl.program_id(1)
    qi = pl.program_id(2)
    kj = pl.program_id(3)

    @pl.when(kj == 0)
    def _():
        m_ref[...] = jnp.full_like(m_ref, NEG_BIG)
        l_ref[...] = jnp.zeros_like(l_ref)
        acc_ref[...] = jnp.zeros_like(acc_ref)

    slope = slope_ref[h]
    d0 = qi * tq - kj * tk
    ii = lax.broadcasted_iota(jnp.int32, (tq, tk), 0)
    jj = lax.broadcasted_iota(jnp.int32, (tq, tk), 1)
    bias = jnp.abs(ii - jj + d0).astype(F32) * slope
    v = v_ref[...]
    for m in range(2):
        cols = slice(m * DIFF_HD, (m + 1) * DIFF_HD)
        s = lax.dot_general(q_ref[:, cols], k_ref[:, cols], (((1,), (1,)), ((), ())),
                            preferred_element_type=F32) - bias
        m_old = m_ref[m]
        m_new = jnp.maximum(m_old, jnp.max(s, axis=1, keepdims=True))
        alpha = jnp.exp(m_old - m_new)
        p = jnp.exp(s - m_new)
        l_ref[m] = alpha * l_ref[m] + jnp.sum(p, axis=1, keepdims=True)
        acc_ref[m] = alpha * acc_ref[m] + jnp.dot(p.astype(BF16), v, preferred_element_type=F32)
        m_ref[m] = m_new

    @pl.when(kj == nk - 1)
    def _():
        lam = lam_ref[...]
        lam_full = (jnp.exp(jnp.sum(lam[0:1, :] * lam[1:2, :], axis=-1, keepdims=True))
                    - jnp.exp(jnp.sum(lam[2:3, :] * lam[3:4, :], axis=-1, keepdims=True)) + lam_init)
        o = acc_ref[0] / l_ref[0] - lam_full * (acc_ref[1] / l_ref[1])
        ms = jnp.mean(o * o, axis=-1, keepdims=True)
        o_ref[...] = (o * lax.rsqrt(ms + LN_EPS) * sw_ref[...] * (1.0 - lam_init)).astype(o_ref.dtype)


def _diff_attention(p, lam, subln_w, tok_off, n_seq, seq_len, layer_idx, *, tq=512, tk=512):
    tq = min(tq, seq_len)
    tk = min(tk, seq_len)
    nq = seq_len // tq
    nk = seq_len // tk
    lam_init = 0.8 - 0.6 * math.exp(-0.3 * layer_idx)
    slopes = jnp.asarray([2.0 ** (-8.0 * (i + 1) / DIFF_HEADS) for i in range(DIFF_HEADS)], F32)
    w2 = 2 * DIFF_HD
    qo = tok_off // tq
    ko = tok_off // tk
    return pl.pallas_call(
        functools.partial(_attn_body, tq=tq, tk=tk, nk=nk, lam_init=lam_init),
        out_shape=jax.ShapeDtypeStruct((n_seq * seq_len, DIFF_VAL_W), BF16),
        grid=(n_seq, DIFF_HEADS, nq, nk),
        in_specs=[pl.BlockSpec(memory_space=pltpu.SMEM),
                  pl.BlockSpec((4, DIFF_HD), lambda b, h, i, j: (0, 0)),
                  pl.BlockSpec((1, w2), lambda b, h, i, j: (0, 0)),
                  pl.BlockSpec((tq, w2), lambda b, h, i, j: (qo + b * nq + i, P_DQ // w2 + h)),
                  pl.BlockSpec((tk, w2), lambda b, h, i, j: (ko + b * nk + j, P_DK // w2 + h)),
                  pl.BlockSpec((tk, w2), lambda b, h, i, j: (ko + b * nk + j, P_DV // w2 + h))],
        out_specs=pl.BlockSpec((tq, w2), lambda b, h, i, j: (b * nq + i, h)),
        scratch_shapes=[pltpu.VMEM((2, tq, 1), F32), pltpu.VMEM((2, tq, 1), F32),
                        pltpu.VMEM((2, tq, w2), F32)],
        compiler_params=_params(("parallel", "parallel", "parallel", "arbitrary")),
        name="diff_attention",
    )(slopes, lam.astype(F32), subln_w.reshape(1, w2).astype(F32), p, p, p)


def _merge_body(og_ref, od_ref, wg_ref, wd_ref, gg_ref, gd_ref, o_ref):
    a = jnp.dot(og_ref[...], wg_ref[...], preferred_element_type=F32)
    b = jnp.dot(od_ref[...], wd_ref[...], preferred_element_type=F32)
    o_ref[...] = (gg_ref[...].astype(F32) * a + gd_ref[...].astype(F32) * b).astype(o_ref.dtype)


def _merge(o_gdn, o_diff, w_g, w_d, p, *, tm=512, tn=512):
    n = o_gdn.shape[0]
    tm = min(tm, n)
    return pl.pallas_call(
        _merge_body,
        out_shape=jax.ShapeDtypeStruct((n, D_MODEL), BF16),
        grid=(n // tm, D_MODEL // tn),
        in_specs=[pl.BlockSpec((tm, GDN_VAL_W), lambda i, j: (i, 0)),
                  pl.BlockSpec((tm, DIFF_VAL_W), lambda i, j: (i, 0)),
                  pl.BlockSpec((GDN_VAL_W, tn), lambda i, j: (0, j)),
                  pl.BlockSpec((DIFF_VAL_W, tn), lambda i, j: (0, j)),
                  pl.BlockSpec((tm, tn), lambda i, j: (i, P_GATES // tn + j)),
                  pl.BlockSpec((tm, tn), lambda i, j: (i, (P_GATES + D_MODEL) // tn + j))],
        out_specs=pl.BlockSpec((tm, tn), lambda i, j: (i, j)),
        compiler_params=_params(("parallel", "arbitrary")),
        name="branch_merge",
    )(o_gdn, o_diff, w_g, w_d, p, p)


def _out_ln_body(m_ref, w_ref, x_ref, g_ref, b_ref, o_ref):
    mix = jnp.dot(m_ref[...], w_ref[...], preferred_element_type=F32)
    o_ref[...] = _layer_norm(DN_ALPHA * x_ref[...] + mix, g_ref[...], b_ref[...])


def _out_ln(merged, w_out, x, g, b, *, tm=512):
    n = x.shape[0]
    tm = min(tm, n)
    return pl.pallas_call(
        _out_ln_body,
        out_shape=jax.ShapeDtypeStruct((n, D_MODEL), F32),
        grid=(n // tm,),
        in_specs=[pl.BlockSpec((tm, D_MODEL), lambda i: (i, 0)),
                  pl.BlockSpec((D_MODEL, D_MODEL), lambda i: (0, 0)),
                  pl.BlockSpec((tm, D_MODEL), lambda i: (i, 0)),
                  pl.BlockSpec((1, D_MODEL), lambda i: (0, 0)),
                  pl.BlockSpec((1, D_MODEL), lambda i: (0, 0))],
        out_specs=pl.BlockSpec((tm, D_MODEL), lambda i: (i, 0)),
        compiler_params=_params(("parallel",)),
        name="out_proj_ln1",
    )(merged, w_out, x, g.reshape(1, D_MODEL).astype(F32), b.reshape(1, D_MODEL).astype(F32))


def _first_argmax(x, row_ids, n_rows):
    mx = jnp.max(x, axis=0, keepdims=True)
    idx = jnp.min(jnp.where(x == mx, row_ids, n_rows), axis=0, keepdims=True)
    return mx, idx


def _router_body(h_ref, w_ref, bias_ref, e_ref, g_ref, *, n_exp, tm):
    per = n_exp // N_GROUPS
    logits = lax.dot_general(w_ref[...], h_ref[...], (((1,), (1,)), ((), ())),
                             precision=HIGHEST, preferred_element_type=F32)
    scores = _sigmoid(logits)
    choice = scores + bias_ref[...]
    ids_g = lax.broadcasted_iota(jnp.int32, (per, tm), 0)
    group_rows = []
    for g in range(N_GROUPS):
        xg = choice[g * per:(g + 1) * per, :]
        m1, i1 = _first_argmax(xg, ids_g, per)
        m2 = jnp.max(jnp.where(ids_g == i1, -jnp.inf, xg), axis=0, keepdims=True)
        group_rows.append(m1 + m2)
    gs = jnp.concatenate(group_rows, axis=0)
    ids_n = lax.broadcasted_iota(jnp.int32, (N_GROUPS, tm), 0)
    keep = jnp.zeros((N_GROUPS, tm), F32)
    for _ in range(TOPK_GROUPS):
        _, gi = _first_argmax(gs, ids_n, N_GROUPS)
        hit = ids_n == gi
        keep = jnp.where(hit, 1.0, keep)
        gs = jnp.where(hit, -jnp.inf, gs)
    masked = jnp.concatenate(
        [jnp.where(keep[g:g + 1, :] > 0.5, choice[g * per:(g + 1) * per, :], -jnp.inf)
         for g in range(N_GROUPS)], axis=0)
    ids_e = lax.broadcasted_iota(jnp.int32, (n_exp, tm), 0)
    top_idx = []
    top_w = []
    for _ in range(TOP_K):
        _, ei = _first_argmax(masked, ids_e, n_exp)
        hit = ids_e == ei
        top_idx.append(ei)
        top_w.append(jnp.sum(jnp.where(hit, scores, 0.0), axis=0, keepdims=True))
        masked = jnp.where(hit, -jnp.inf, masked)
    gw = jnp.concatenate(top_w, axis=0)
    gw = gw / (jnp.sum(gw, axis=0, keepdims=True) + 1e-20) * ROUTED_SCALE
    e_ref[...] = jnp.concatenate(top_idx, axis=0)
    g_ref[...] = gw


def _router(h, router_w, router_bias, *, tm=256):
    n = h.shape[0]
    n_exp = router_w.shape[1]
    tm = min(tm, n)
    return pl.pallas_call(
        functools.partial(_router_body, n_exp=n_exp, tm=tm),
        out_shape=(jax.ShapeDtypeStruct((TOP_K, n), jnp.int32), jax.ShapeDtypeStruct((TOP_K, n), F32)),
        grid=(n // tm,),
        in_specs=[pl.BlockSpec((tm, D_MODEL), lambda i: (i, 0)),
                  pl.BlockSpec((n_exp, D_MODEL), lambda i: (0, 0)),
                  pl.BlockSpec((n_exp, 1), lambda i: (0, 0))],
        out_specs=(pl.BlockSpec((TOP_K, tm), lambda i: (0, i)), pl.BlockSpec((TOP_K, tm), lambda i: (0, i))),
        compiler_params=_params(("parallel",)),
        name="moe_router",
    )(h, router_w.T.astype(F32), router_bias.reshape(n_exp, 1).astype(F32))


def _gather_body(idx_ref, src_ref, dst_ref, sem, *, rows, steps):
    i = pl.program_id(0)
    base = i * rows

    def issue(r, carry):
        tok = idx_ref[0, 0, r]
        pltpu.make_async_copy(src_ref.at[pl.ds(tok, 1)], dst_ref.at[pl.ds(base + r, 1)], sem).start()
        return carry

    lax.fori_loop(0, rows, issue, 0)

    def drain(step):
        off = step * rows
        pltpu.make_async_copy(dst_ref.at[pl.ds(off, rows)], dst_ref.at[pl.ds(off, rows)], sem).wait()

    @pl.when(i > 0)
    def _():
        drain(i - 1)

    @pl.when(i == steps - 1)
    def _():
        drain(i)


def _gather_rows(src, idx, *, rows=256):
    n_out = idx.shape[0]
    rows = min(rows, n_out)
    steps = n_out // rows
    d = src.shape[1]
    return pl.pallas_call(
        functools.partial(_gather_body, rows=rows, steps=steps),
        out_shape=jax.ShapeDtypeStruct((n_out, d), src.dtype),
        grid=(steps,),
        in_specs=[pl.BlockSpec((1, 1, rows), lambda i: (i, 0, 0), memory_space=pltpu.SMEM),
                  pl.BlockSpec(memory_space=pl.ANY)],
        out_specs=pl.BlockSpec(memory_space=pl.ANY),
        scratch_shapes=[pltpu.SemaphoreType.DMA(())],
        compiler_params=_params(("arbitrary",)),
        name="row_gather",
    )(idx.reshape(steps, 1, rows).astype(jnp.int32), src)


def _expert_body(be_ref, nu_ref, x_ref, wg_ref, wu_ref, wd_ref, o_ref, wgb_ref, wub_ref, wdb_ref):
    i = pl.program_id(0)
    used = i < nu_ref[0]
    fresh = jnp.logical_or(i == 0, be_ref[i] != be_ref[jnp.maximum(i - 1, 0)])

    @pl.when(jnp.logical_and(used, fresh))
    def _():
        wgb_ref[...] = wg_ref[...].astype(BF16)
        wub_ref[...] = wu_ref[...].astype(BF16)
        wdb_ref[...] = wd_ref[...].astype(BF16)

    @pl.when(used)
    def _():
        xb = x_ref[...].astype(BF16)
        gate = jnp.dot(xb, wgb_ref[...], preferred_element_type=F32)
        up = jnp.dot(xb, wub_ref[...], preferred_element_type=F32)
        act = (_silu(gate) * up).astype(BF16)
        o_ref[...] = jnp.dot(act, wdb_ref[...], preferred_element_type=F32)

    @pl.when(jnp.logical_not(used))
    def _():
        o_ref[...] = jnp.zeros_like(o_ref)


def _experts(xs, block_e, n_used, w_gate, w_up, w_down, *, tb):
    n_slots = xs.shape[0]
    ff = w_gate.shape[2]
    grid_spec = pltpu.PrefetchScalarGridSpec(
        num_scalar_prefetch=2,
        grid=(n_slots // tb,),
        in_specs=[pl.BlockSpec((tb, D_MODEL), lambda i, be, nu: (i, 0)),
                  pl.BlockSpec((None, D_MODEL, ff), lambda i, be, nu: (be[i], 0, 0)),
                  pl.BlockSpec((None, D_MODEL, ff), lambda i, be, nu: (be[i], 0, 0)),
                  pl.BlockSpec((None, ff, D_MODEL), lambda i, be, nu: (be[i], 0, 0))],
        out_specs=pl.BlockSpec((tb, D_MODEL), lambda i, be, nu: (i, 0)),
        scratch_shapes=[pltpu.VMEM((D_MODEL, ff), BF16), pltpu.VMEM((D_MODEL, ff), BF16),
                        pltpu.VMEM((ff, D_MODEL), BF16)],
    )
    return pl.pallas_call(
        _expert_body,
        out_shape=jax.ShapeDtypeStruct((n_slots, D_MODEL), F32),
        grid_spec=grid_spec,
        compiler_params=_params(("arbitrary",)),
        name="moe_experts",
    )(block_e, n_used, xs, w_gate, w_up, w_down)


def _combine_body(h_ref, y_ref, gw_ref, sgu_ref, sd_ref, g_ref, b_ref, o_ref, *, ff):
    h = h_ref[...]
    hb = h.astype(BF16)
    gu = jnp.dot(hb, sgu_ref[...], preferred_element_type=F32)
    act = (_silu(gu[:, :ff]) * gu[:, ff:]).astype(BF16)
    f = jnp.dot(act, sd_ref[...], preferred_element_type=F32)
    gw = gw_ref[...]
    for k in range(TOP_K):
        f = f + y_ref[:, k * D_MODEL:(k + 1) * D_MODEL] * gw[:, k:k + 1]
    o_ref[...] = _layer_norm(DN_ALPHA * h + f, g_ref[...], b_ref[...])


def _combine(h, y_tok, gw_t, sh_gu, sh_down, g, b, tok_off, n_rows, *, tm=128):
    ff = sh_down.shape[0]
    tm = min(tm, n_rows)
    off = tok_off // tm
    return pl.pallas_call(
        functools.partial(_combine_body, ff=ff),
        out_shape=jax.ShapeDtypeStruct((n_rows, D_MODEL), F32),
        grid=(n_rows // tm,),
        in_specs=[pl.BlockSpec((tm, D_MODEL), lambda i: (off + i, 0)),
                  pl.BlockSpec((tm, TOP_K * D_MODEL), lambda i: (off + i, 0)),
                  pl.BlockSpec((tm, TOP_K), lambda i: (off + i, 0)),
                  pl.BlockSpec((D_MODEL, 2 * ff), lambda i: (0, 0)),
                  pl.BlockSpec((ff, D_MODEL), lambda i: (0, 0)),
                  pl.BlockSpec((1, D_MODEL), lambda i: (0, 0)),
                  pl.BlockSpec((1, D_MODEL), lambda i: (0, 0))],
        out_specs=pl.BlockSpec((tm, D_MODEL), lambda i: (i, 0)),
        compiler_params=_params(("parallel",)),
        name="moe_combine_ln2",
    )(h, y_tok, gw_t, sh_gu, sh_down, g.reshape(1, D_MODEL).astype(F32), b.reshape(1, D_MODEL).astype(F32))


def _routing_tables(top_e, n_exp, tb):
    k, n = top_e.shape
    a = k * n
    n_blocks = a // tb + n_exp
    e_flat = top_e.reshape(-1)
    tok_flat = jnp.tile(jnp.arange(n, dtype=jnp.int32), k)
    order = jnp.argsort(e_flat)
    e_sorted = e_flat[order]
    counts = jnp.zeros((n_exp,), jnp.int32).at[e_flat].add(1)
    padded = (counts + tb - 1) // tb * tb
    pad_end = jnp.cumsum(padded)
    pad_start = pad_end - padded
    start = jnp.cumsum(counts) - counts
    dest = pad_start[e_sorted] + jnp.arange(a, dtype=jnp.int32) - start[e_sorted]
    slot_tok = jnp.zeros((n_blocks * tb,), jnp.int32).at[dest].set(tok_flat[order])
    pos = jnp.zeros((a,), jnp.int32).at[order].set(dest)
    block_start = jnp.arange(n_blocks, dtype=jnp.int32) * tb
    block_e = jnp.minimum(jnp.sum(block_start[:, None] >= pad_end[None, :], axis=1), n_exp - 1).astype(jnp.int32)
    n_used = (pad_end[-1] // tb).astype(jnp.int32).reshape(1)
    block_e = jnp.where(block_start < pad_end[-1], block_e, block_e[jnp.maximum(n_used[0] - 1, 0)])
    pos_tok = pos.reshape(k, n).T.reshape(-1)
    return slot_tok, pos_tok, block_e, n_used


def _layer(xs, layer_idx, w_in, conv_w, a_log, dt_bias, norm_w, lam, subln_w, w_branch_gdn, w_branch_diff,
           w_out, ln1_g, ln1_b, router_w, router_bias, w_gate, w_up, w_down, sh_gate, sh_up, sh_down,
           ln2_g, ln2_b, *, moe_tb=256):
    seqs = [(x.shape[0], x.shape[1]) for x in xs]
    x = jnp.concatenate([x.reshape(-1, D_MODEL) for x in xs], axis=0)
    n = x.shape[0]
    seq_starts, seq_ends, group_off = [], [], []
    off = 0
    for b, t in seqs:
        group_off.append(off)
        for _ in range(b):
            seq_starts.append(off)
            off += t
            seq_ends.append(off)
    seq_starts, seq_ends = tuple(seq_starts), tuple(seq_ends)

    o_qkv, o_z, o_ab = GDN_CONV_CH, GDN_CONV_CH + GDN_VAL_W, GDN_CONV_CH + GDN_VAL_W + GDN_GATE_W
    w_main = jnp.concatenate([w_in[:, :o_z], w_in[:, o_ab:]], axis=1).astype(BF16)
    w_ab = w_in[:, o_z:o_ab].astype(BF16)

    p = _proj_in(x, w_main)
    gates = _gdn_gates(x, w_ab, a_log, dt_bias)
    g_t = gates.T.reshape(GDN_GATE_W, 1, n)
    act = _gdn_conv(p, conv_w.astype(F32), seq_starts, seq_ends)
    o_fwd = _gdn_scan(act, g_t, p, None, norm_w, seq_starts, seq_ends, reverse=False)
    o_gdn = _gdn_scan(act, g_t, p, o_fwd, norm_w, seq_starts, seq_ends, reverse=True)
    o_diff = jnp.concatenate(
        [_diff_attention(p, lam, subln_w, goff, b, t, layer_idx) for (b, t), goff in zip(seqs, group_off)], axis=0)
    merged = _merge(o_gdn, o_diff, w_branch_gdn.astype(BF16), w_branch_diff.astype(BF16), p)
    h = _out_ln(merged, w_out.astype(BF16), x, ln1_g, ln1_b)

    n_exp = router_w.shape[1]
    top_e, gw = _router(h, router_w, router_bias)
    slot_tok, pos_tok, block_e, n_used = _routing_tables(top_e, n_exp, moe_tb)
    xs_sorted = _gather_rows(h, slot_tok)
    ys = _experts(xs_sorted, block_e, n_used, w_gate, w_up, w_down, tb=moe_tb)
    y_tok = _gather_rows(ys, pos_tok).reshape(n, TOP_K * D_MODEL)
    sh_gu = jnp.concatenate([sh_gate, sh_up], axis=1).astype(BF16)
    outs = []
    for (b, t), goff in zip(seqs, group_off):
        y = _combine(h, y_tok, gw.T, sh_gu, sh_down.astype(BF16), ln2_g, ln2_b, goff, b * t)
        outs.append(y.reshape(b, t, D_MODEL))
    return outs


def kernel(x_prompt, x_sample, w_in, gdn_conv_w, gdn_a_log, gdn_dt_bias, gdn_norm_w, diff_lambda, diff_subln_w, w_branch_gdn, w_branch_diff, w_out, ln1_g, ln1_b, router_w, router_bias, exp_w_gate, exp_w_up, exp_w_down, sh_w_gate, sh_w_up, sh_w_down, ln2_g, ln2_b):
    xs = [x_prompt, x_sample]
    for l in range(DEPTH):
        xs = _layer(xs, l, w_in[l], gdn_conv_w[l], gdn_a_log[l], gdn_dt_bias[l], gdn_norm_w[l], diff_lambda[l],
                    diff_subln_w[l], w_branch_gdn[l], w_branch_diff[l], w_out[l], ln1_g[l], ln1_b[l], router_w[l],
                    router_bias[l], exp_w_gate[l], exp_w_up[l], exp_w_down[l], sh_w_gate[l], sh_w_up[l],
                    sh_w_down[l], ln2_g[l], ln2_b[l])
    return (xs[0], xs[1])
```

```python
import functools
import math

import jax
import jax.numpy as jnp
from jax import lax
from jax.experimental import pallas as pl
from jax.experimental.pallas import tpu as pltpu

F32 = jnp.float32
BF16 = jnp.bfloat16
HIGHEST = lax.Precision.HIGHEST

D_MODEL = 2048
GDN_QK_HEADS = 16
GDN_V_HEADS = 32
GDN_DK = 128
GDN_DV = 128
GDN_KEY_W = GDN_QK_HEADS * GDN_DK
GDN_VAL_W = GDN_V_HEADS * GDN_DV
GDN_CONV_CH = 2 * GDN_KEY_W + GDN_VAL_W
GDN_GATE_W = 4 * GDN_V_HEADS
GDN_CONV_K = 5
GDN_CHUNK = 64
DIFF_HEADS = 8
DIFF_HD = 128
DIFF_QK_W = 2 * DIFF_HEADS * DIFF_HD
DIFF_VAL_W = DIFF_HEADS * 2 * DIFF_HD
N_GROUPS = 8
TOPK_GROUPS = 4
TOP_K = 8
ROUTED_SCALE = 2.5
DEPTH = 1
DN_ALPHA = (2 * DEPTH) ** 0.25
LN_EPS = 1e-5
RMS_EPS = 1e-6
NEG_BIG = -1e30
LOG2E = math.log2(math.e)

P_QKV = 0
P_Z = P_QKV + GDN_CONV_CH
P_DQ = P_Z + GDN_VAL_W
P_DK = P_DQ + DIFF_QK_W
P_DV = P_DK + DIFF_QK_W
P_GATES = P_DV + DIFF_VAL_W
P_COLS = P_GATES + 2 * D_MODEL

VMEM_LIMIT = 56 * 1024 * 1024
SLAB = D_MODEL // 128


def _params(grid_rank, vmem=VMEM_LIMIT):
    return pltpu.CompilerParams(dimension_semantics=("arbitrary",) * grid_rank, vmem_limit_bytes=vmem)


def _sigmoid(x):
    return 1.0 / (1.0 + jnp.exp(-x))


def _silu(x):
    return x * _sigmoid(x)


def _layer_norm(x, g, b):
    mu = jnp.mean(x, axis=-1, keepdims=True)
    xc = x - mu
    var = jnp.mean(xc * xc, axis=-1, keepdims=True)
    return xc * lax.rsqrt(var + LN_EPS) * g + b


def _proj_in_body(x_ref, w_ref, o_ref, xb_ref, *, tn):
    j = pl.program_id(1)

    @pl.when(j == 0)
    def _():
        xb_ref[...] = x_ref[...].astype(BF16)

    acc = jnp.dot(xb_ref[...], w_ref[...], preferred_element_type=F32)
    col = j * tn
    is_gate = col >= P_GATES
    is_dq = jnp.logical_and(col >= P_DQ, col < P_DK)
    scale = jnp.where(is_dq, LOG2E * DIFF_HD ** -0.5, 1.0).astype(F32)

    @pl.when(is_gate)
    def _():
        o_ref[...] = _sigmoid(acc).astype(o_ref.dtype)

    @pl.when(jnp.logical_not(is_gate))
    def _():
        o_ref[...] = (acc * scale).astype(o_ref.dtype)


def _proj_in(x, w, *, tm=1024, tn=512):
    n, k = x.shape
    m = w.shape[1]
    tm = min(tm, n)
    return pl.pallas_call(
        functools.partial(_proj_in_body, tn=tn),
        out_shape=jax.ShapeDtypeStruct((n, m), BF16),
        grid=(n // tm, m // tn),
        in_specs=[pl.BlockSpec((tm, k), lambda i, j: (i, 0)),
                  pl.BlockSpec((k, tn), lambda i, j: (0, j))],
        out_specs=pl.BlockSpec((tm, tn), lambda i, j: (i, j)),
        scratch_shapes=[pltpu.VMEM((tm, k), BF16)],
        compiler_params=_params(2),
        name="proj_in",
    )(x, w)


def _gates_body(x_ref, w_ref, prm_ref, o_ref, *, tt):
    ab = jnp.dot(x_ref[...].astype(BF16), w_ref[...], preferred_element_type=F32)
    a_log = prm_ref[0:1, :]
    dt_bias = prm_ref[1:2, :]
    y = ab + dt_bias
    softplus = jnp.maximum(y, 0.0) + jnp.log(1.0 + jnp.exp(-jnp.abs(y)))
    log_decay = -jnp.exp(a_log) * softplus
    beta = _sigmoid(ab)
    ri = lax.broadcasted_iota(jnp.int32, (tt, tt), 0)
    ci = lax.broadcasted_iota(jnp.int32, (tt, tt), 1)
    same = (ri // GDN_CHUNK) == (ci // GDN_CHUNK)
    m_lo = jnp.where(same, jnp.where(ci <= ri, 1.0, 0.0), 0.0).astype(F32)
    m_up = jnp.where(same, jnp.where(ci >= ri, 1.0, 0.0), 0.0).astype(F32)
    c_lo = jnp.dot(m_lo, log_decay, precision=HIGHEST, preferred_element_type=F32)
    c_up = jnp.dot(m_up, log_decay, precision=HIGHEST, preferred_element_type=F32)
    lane = lax.broadcasted_iota(jnp.int32, (tt, GDN_GATE_W), 1)
    is_a = (lane % (2 * GDN_V_HEADS)) < GDN_V_HEADS
    is_bwd = lane >= 2 * GDN_V_HEADS
    o_ref[...] = jnp.where(is_a, jnp.where(is_bwd, c_up, c_lo), beta)


def _gdn_gates(x, w_ab, a_log, dt_bias, *, tt=512):
    n, k = x.shape
    tt = min(tt, n)
    zeros = jnp.zeros((GDN_V_HEADS,), F32)
    prm = jnp.zeros((8, GDN_GATE_W), F32)
    prm = prm.at[0].set(jnp.concatenate([a_log[0], zeros, a_log[1], zeros]).astype(F32))
    prm = prm.at[1].set(jnp.concatenate([dt_bias[0], zeros, dt_bias[1], zeros]).astype(F32))
    return pl.pallas_call(
        functools.partial(_gates_body, tt=tt),
        out_shape=jax.ShapeDtypeStruct((n, GDN_GATE_W), F32),
        grid=(n // tt,),
        in_specs=[pl.BlockSpec((tt, k), lambda i: (i, 0)),
                  pl.BlockSpec((k, GDN_GATE_W), lambda i: (0, 0)),
                  pl.BlockSpec((8, GDN_GATE_W), lambda i: (0, 0))],
        out_specs=pl.BlockSpec((tt, GDN_GATE_W), lambda i: (i, 0)),
        compiler_params=_params(1),
        name="gdn_gates",
    )(x, w_ab, prm)


CONV_HALO = 16


def _any_equal(value, constants):
    hit = value == constants[0]
    for c in constants[1:]:
        hit = jnp.logical_or(hit, value == c)
    return hit


def _conv_body(prev_ref, cur_ref, next_ref, w_ref, o_ref, ext_ref, *, tt, tc, seq_starts, seq_ends):
    i = pl.program_id(0)
    j = pl.program_id(1)
    t0 = i * tt
    at_start = _any_equal(t0, seq_starts)
    at_end = _any_equal(t0 + tt, seq_ends)
    prev = prev_ref[...].astype(F32)[CONV_HALO - 8:, :]
    nxt = next_ref[...].astype(F32)[:8, :]
    ext_ref[0:8, :] = jnp.where(at_start, 0.0, prev)
    ext_ref[8:8 + tt, :] = cur_ref[...].astype(F32)
    ext_ref[8 + tt:16 + tt, :] = jnp.where(at_end, 0.0, nxt)
    pad = (GDN_CONV_K - 1) // 2
    acc = jnp.zeros((tt, tc), F32)
    for tap in range(GDN_CONV_K):
        acc = acc + ext_ref[pl.ds(8 - pad + tap, tt), :] * w_ref[tap:tap + 1, :]
    act = _silu(acc)
    col = j * tc
    is_qk = col < 2 * GDN_KEY_W
    q_scale = jnp.where(col < GDN_KEY_W, GDN_DK ** -0.5, 1.0).astype(F32)
    for s in range(tc // GDN_DK):
        a = act[:, s * GDN_DK:(s + 1) * GDN_DK]
        ss = jnp.sum(a * a, axis=-1, keepdims=True)
        normed = a * (lax.rsqrt(ss + RMS_EPS) * q_scale)
        o_ref[:, s * GDN_DK:(s + 1) * GDN_DK] = jnp.where(is_qk, normed, a).astype(o_ref.dtype)


def _gdn_conv(p, conv_w, seq_starts, seq_ends, *, tt=512, tc=512):
    n = p.shape[0]
    tt = min(tt, n)
    hb = tt // CONV_HALO
    n_halo = n // CONV_HALO
    return pl.pallas_call(
        functools.partial(_conv_body, tt=tt, tc=tc, seq_starts=seq_starts, seq_ends=seq_ends),
        out_shape=jax.ShapeDtypeStruct((n, GDN_CONV_CH), BF16),
        grid=(n // tt, GDN_CONV_CH // tc),
        in_specs=[pl.BlockSpec((CONV_HALO, tc), lambda i, j: (jnp.maximum(i * hb - 1, 0), j)),
                  pl.BlockSpec((tt, tc), lambda i, j: (i, j)),
                  pl.BlockSpec((CONV_HALO, tc), lambda i, j: (jnp.minimum((i + 1) * hb, n_halo - 1), j)),
                  pl.BlockSpec((GDN_CONV_K, tc), lambda i, j: (0, j))],
        out_specs=pl.BlockSpec((tt, tc), lambda i, j: (i, j)),
        scratch_shapes=[pltpu.VMEM((tt + 16, tc), F32)],
        compiler_params=_params(2),
        name="gdn_conv",
    )(p, p, p, conv_w)


def _dot_f32(a, b):
    return jnp.dot(a, b, precision=HIGHEST, preferred_element_type=F32)


def _dot_bf16(a, b):
    return jnp.dot(a.astype(BF16), b.astype(BF16), preferred_element_type=F32)


def _gdn_scan_body(*refs, tt, nt, hp, reverse, seq_starts, seq_ends):
    if reverse:
        q_ref, k_ref, v_ref, g_ref, b_ref, ofwd_ref, z_ref, nw_ref, o_ref, s_ref = refs
    else:
        q_ref, k_ref, v_ref, g_ref, b_ref, o_ref, s_ref = refs
    C = GDN_CHUNK
    t = pl.program_id(1)
    tok0 = ((nt - 1 - t) if reverse else t) * tt
    reset = _any_equal(tok0 + tt, seq_ends) if reverse else _any_equal(tok0, seq_starts)

    @pl.when(reset)
    def _():
        s_ref[...] = jnp.zeros_like(s_ref)

    n_chunks = tt // C
    ri = lax.broadcasted_iota(jnp.int32, (tt, tt), 0)
    ci = lax.broadcasted_iota(jnp.int32, (tt, tt), 1)
    same = (ri // C) == (ci // C)
    eye = ri == ci
    causal = jnp.logical_and(same, (ci >= ri) if reverse else (ci <= ri))
    strict = jnp.logical_and(same, (ci > ri) if reverse else (ci < ri))
    eye_f = jnp.where(eye, 1.0, 0.0).astype(F32)
    last = 0 if reverse else C - 1
    order = range(n_chunks - 1, -1, -1) if reverse else range(n_chunks)
    chunk_of_lane = lax.broadcasted_iota(jnp.int32, (1, tt), 1) // C
    chunk_of_lane_k = lax.broadcasted_iota(jnp.int32, (GDN_DK, tt), 1) // C

    shared = []
    for pp in range(hp):
        q = q_ref[:, pp * GDN_DK:(pp + 1) * GDN_DK]
        k = k_ref[:, pp * GDN_DK:(pp + 1) * GDN_DK]
        kf = k.astype(F32)
        kt = kf.T
        ktb = kt.astype(BF16)
        kk = jnp.dot(k, ktb, preferred_element_type=F32)
        qk = jnp.dot(q, ktb, preferred_element_type=F32)
        shared.append((kf, q.astype(F32), kt, kk, qk))
    for hh in range(2 * hp):
        kf, qf, kt, kk, qk = shared[hh // 2]
        cols = slice(hh * GDN_DV, (hh + 1) * GDN_DV)
        g_row = g_ref[hh]
        b_row = b_ref[hh]
        g_col = jnp.sum(jnp.where(eye, g_row, 0.0), axis=1, keepdims=True)
        b_col = jnp.sum(jnp.where(eye, b_row, 0.0), axis=1, keepdims=True)
        g_last = [g_row[:, c * C + last:c * C + last + 1] for c in range(n_chunks)]
        g_last_row = jnp.zeros((1, tt), F32)
        for c in range(n_chunks):
            g_last_row = jnp.where(chunk_of_lane == c, g_last[c], g_last_row)
        decay = jnp.exp(jnp.where(causal, g_col - g_row, NEG_BIG))
        low = jnp.where(strict, kk * decay, 0.0) * b_col
        low_b = low.astype(BF16)
        pw = jnp.dot(low_b, low_b, preferred_element_type=F32)
        inv = eye_f - low
        n_factors = int(math.log2(C)) - 1
        for j in range(n_factors):
            pw_b = pw.astype(BF16)
            if j < n_factors - 1:
                both = jnp.dot(jnp.concatenate([inv, pw], axis=0).astype(BF16), pw_b, preferred_element_type=F32)
                inv = inv + both[:tt]
                pw = both[tt:]
            else:
                inv = inv + jnp.dot(inv.astype(BF16), pw_b, preferred_element_type=F32)
        eg_col = jnp.exp(g_col)
        v = v_ref[:, cols].astype(F32)
        x = jnp.concatenate([v * b_col, kf * (b_col * eg_col)], axis=1)
        uw = jnp.dot(inv.astype(BF16), x.astype(BF16), preferred_element_type=F32).astype(BF16)
        from_q = jnp.dot((qk * decay).astype(BF16), uw, preferred_element_type=F32)
        ktd = kt * jnp.exp(g_last_row - g_row)
        lhs_k = jnp.concatenate([jnp.where(chunk_of_lane_k == c, ktd, 0.0) for c in range(n_chunks)],
                                axis=0).astype(BF16)
        from_k = jnp.dot(lhs_k, uw, preferred_element_type=F32)
        q_eff = qf * eg_col - from_q[:, GDN_DV:]
        for c in order:
            rows = slice(c * C, (c + 1) * C)
            krows = slice(c * GDN_DK, (c + 1) * GDN_DK)
            lhs = jnp.concatenate([q_eff[rows], from_k[krows, GDN_DV:]], axis=0).astype(BF16)
            state = s_ref[hh]
            res = jnp.dot(lhs, state.astype(BF16), preferred_element_type=F32)
            o = res[:C] + from_q[rows, :GDN_DV]
            s_ref[hh] = state * jnp.exp(g_last[c]) - res[C:] + from_k[krows, :GDN_DV]
            if reverse:
                tot = o + ofwd_ref[rows, cols]
                ms = jnp.mean(tot * tot, axis=-1, keepdims=True)
                z = z_ref[rows, cols].astype(F32)
                o_ref[rows, cols] = (tot * lax.rsqrt(ms + RMS_EPS) * nw_ref[...] * _silu(z)).astype(o_ref.dtype)
            else:
                o_ref[rows, cols] = o


def _gdn_scan(act, g_t, p, o_fwd, norm_w, seq_starts, seq_ends, *, reverse, tt=256, hp=2):
    n = act.shape[0]
    tt = min(tt, n)
    nt = n // tt
    groups = GDN_QK_HEADS // hp
    kw = hp * GDN_DK
    vw = 2 * hp * GDN_DV

    def tmap(t):
        return (nt - 1 - t) if reverse else t

    gate_blocks = (2 * GDN_V_HEADS) // (2 * hp)
    dirn = 1 if reverse else 0
    in_specs = [
        pl.BlockSpec((tt, kw), lambda h, t: (tmap(t), h)),
        pl.BlockSpec((tt, kw), lambda h, t: (tmap(t), groups + h)),
        pl.BlockSpec((tt, vw), lambda h, t: (tmap(t), 2 * GDN_KEY_W // vw + h)),
        pl.BlockSpec((2 * hp, 1, tt), lambda h, t: (dirn * gate_blocks + h, 0, tmap(t))),
        pl.BlockSpec((2 * hp, 1, tt), lambda h, t: (dirn * gate_blocks + groups + h, 0, tmap(t))),
    ]
    args = [act, act, act, g_t, g_t]
    if reverse:
        in_specs += [
            pl.BlockSpec((tt, vw), lambda h, t: (tmap(t), h)),
            pl.BlockSpec((tt, vw), lambda h, t: (tmap(t), P_Z // vw + h)),
            pl.BlockSpec((1, GDN_DV), lambda h, t: (0, 0)),
        ]
        args += [o_fwd, p, norm_w.reshape(1, GDN_DV).astype(F32)]
        out_dtype = BF16
    else:
        out_dtype = F32
    return pl.pallas_call(
        functools.partial(_gdn_scan_body, tt=tt, nt=nt, hp=hp, reverse=reverse,
                          seq_starts=seq_starts, seq_ends=seq_ends),
        out_shape=jax.ShapeDtypeStruct((n, GDN_VAL_W), out_dtype),
        grid=(groups, nt),
        in_specs=in_specs,
        out_specs=pl.BlockSpec((tt, vw), lambda h, t: (tmap(t), h)),
        scratch_shapes=[pltpu.VMEM((2 * hp, GDN_DK, GDN_DV), F32)],
        compiler_params=_params(2),
        name="gdn_scan_bwd" if reverse else "gdn_scan_fwd",
    )(*args)


def _attn_body(slope_ref, lam_ref, sw_ref, q_ref, k_ref, v_ref, o_ref, m_ref, l_ref, acc_ref, sd_ref,
               *, tile, nk, lam_init):
    h = pl.program_id(1)
    qi = pl.program_id(2)
    kj = pl.program_id(3)
    slope = slope_ref[h]

    @pl.when(kj == 0)
    def _():
        m_ref[...] = jnp.full_like(m_ref, NEG_BIG)
        l_ref[...] = jnp.zeros_like(l_ref)
        acc_ref[...] = jnp.zeros_like(acc_ref)
        ii = lax.broadcasted_iota(jnp.int32, (tile, tile), 0)
        jj = lax.broadcasted_iota(jnp.int32, (tile, tile), 1)
        sd_ref[...] = (ii - jj).astype(F32) * slope

    v = v_ref[...]
    tile_dist = jnp.full((tile, 1), (qi - kj) * tile, jnp.int32).astype(F32) * slope

    def update(add_bias, shift):
        for m in range(2):
            cols = slice(m * DIFF_HD, (m + 1) * DIFF_HD)
            t = add_bias(lax.dot_general(q_ref[:, cols], k_ref[:, cols], (((1,), (1,)), ((), ())),
                                         preferred_element_type=F32))
            m_old = m_ref[m]
            m_new = jnp.maximum(m_old, jnp.max(t, axis=1, keepdims=True) - shift)
            alpha = jnp.exp2(m_old - m_new)
            p = jnp.exp2(t - (m_new + shift))
            l_ref[m] = alpha * l_ref[m] + jnp.sum(p, axis=1, keepdims=True)
            acc_ref[m] = alpha * acc_ref[m] + jnp.dot(p.astype(BF16), v, preferred_element_type=F32)
            m_ref[m] = m_new

    @pl.when(kj < qi)
    def _():
        update(lambda s: s - sd_ref[...], tile_dist)

    @pl.when(kj > qi)
    def _():
        update(lambda s: s + sd_ref[...], -tile_dist)

    @pl.when(kj == qi)
    def _():
        update(lambda s: s - jnp.abs(sd_ref[...]), jnp.zeros((tile, 1), F32))

    @pl.when(kj == nk - 1)
    def _():
        lam = lam_ref[...]
        lam_full = (jnp.exp(jnp.sum(lam[0:1, :] * lam[1:2, :], axis=-1, keepdims=True))
                    - jnp.exp(jnp.sum(lam[2:3, :] * lam[3:4, :], axis=-1, keepdims=True)) + lam_init)
        o = acc_ref[0] / l_ref[0] - lam_full * (acc_ref[1] / l_ref[1])
        ms = jnp.mean(o * o, axis=-1, keepdims=True)
        o_ref[...] = (o * lax.rsqrt(ms + LN_EPS) * sw_ref[...] * (1.0 - lam_init)).astype(o_ref.dtype)


def _diff_attention(p, lam, subln_w, tok_off, n_seq, seq_len, layer_idx, *, tile=512):
    tile = min(tile, seq_len)
    nt = seq_len // tile
    lam_init = 0.8 - 0.6 * math.exp(-0.3 * layer_idx)
    slopes = jnp.asarray([LOG2E * 2.0 ** (-8.0 * (i + 1) / DIFF_HEADS) for i in range(DIFF_HEADS)], F32)
    w2 = 2 * DIFF_HD
    off = tok_off // tile
    return pl.pallas_call(
        functools.partial(_attn_body, tile=tile, nk=nt, lam_init=lam_init),
        out_shape=jax.ShapeDtypeStruct((n_seq * seq_len, DIFF_VAL_W), BF16),
        grid=(n_seq, DIFF_HEADS, nt, nt),
        in_specs=[pl.BlockSpec(memory_space=pltpu.SMEM),
                  pl.BlockSpec((4, DIFF_HD), lambda b, h, i, j: (0, 0)),
                  pl.BlockSpec((1, w2), lambda b, h, i, j: (0, 0)),
                  pl.BlockSpec((tile, w2), lambda b, h, i, j: (off + b * nt + i, P_DQ // w2 + h)),
                  pl.BlockSpec((tile, w2), lambda b, h, i, j: (off + b * nt + j, P_DK // w2 + h)),
                  pl.BlockSpec((tile, w2), lambda b, h, i, j: (off + b * nt + j, P_DV // w2 + h))],
        out_specs=pl.BlockSpec((tile, w2), lambda b, h, i, j: (b * nt + i, h)),
        scratch_shapes=[pltpu.VMEM((2, tile, 1), F32), pltpu.VMEM((2, tile, 1), F32),
                        pltpu.VMEM((2, tile, w2), F32), pltpu.VMEM((tile, tile), F32)],
        compiler_params=_params(4),
        name="diff_attention",
    )(slopes, lam.astype(F32), subln_w.reshape(1, w2).astype(F32), p, p, p)


def _merge_body(og_ref, od_ref, wg_ref, wd_ref, gg_ref, gd_ref, o_ref):
    a = jnp.dot(og_ref[...], wg_ref[...], preferred_element_type=F32)
    b = jnp.dot(od_ref[...], wd_ref[...], preferred_element_type=F32)
    o_ref[...] = (gg_ref[...].astype(F32) * a + gd_ref[...].astype(F32) * b).astype(o_ref.dtype)


def _merge(o_gdn, o_diff, w_g, w_d, p, *, tm=512, tn=512):
    n = o_gdn.shape[0]
    tm = min(tm, n)
    return pl.pallas_call(
        _merge_body,
        out_shape=jax.ShapeDtypeStruct((n, D_MODEL), BF16),
        grid=(n // tm, D_MODEL // tn),
        in_specs=[pl.BlockSpec((tm, GDN_VAL_W), lambda i, j: (i, 0)),
                  pl.BlockSpec((tm, DIFF_VAL_W), lambda i, j: (i, 0)),
                  pl.BlockSpec((GDN_VAL_W, tn), lambda i, j: (0, j)),
                  pl.BlockSpec((DIFF_VAL_W, tn), lambda i, j: (0, j)),
                  pl.BlockSpec((tm, tn), lambda i, j: (i, P_GATES // tn + j)),
                  pl.BlockSpec((tm, tn), lambda i, j: (i, (P_GATES + D_MODEL) // tn + j))],
        out_specs=pl.BlockSpec((tm, tn), lambda i, j: (i, j)),
        compiler_params=_params(2),
        name="branch_merge",
    )(o_gdn, o_diff, w_g, w_d, p, p)


def _store_slabs(slab_ref, value, n_rows):
    for s in range(SLAB):
        slab_ref[pl.ds(s, n_rows, stride=SLAB), :] = value[:, s * 128:(s + 1) * 128]


def _load_slabs(slab_ref, n_rows, first=0, stride=SLAB):
    return jnp.concatenate([slab_ref[pl.ds(first + s, n_rows, stride=stride), :] for s in range(SLAB)], axis=1)


def _out_ln_body(m_ref, w_ref, x_ref, g_ref, b_ref, o_ref, slab_ref, *, tm):
    mix = jnp.dot(m_ref[...], w_ref[...], preferred_element_type=F32)
    h = _layer_norm(DN_ALPHA * x_ref[...] + mix, g_ref[...], b_ref[...])
    o_ref[...] = h
    _store_slabs(slab_ref, h, tm)


def _out_ln(merged, w_out, x, g, b, *, tm=512):
    n = x.shape[0]
    tm = min(tm, n)
    return pl.pallas_call(
        functools.partial(_out_ln_body, tm=tm),
        out_shape=(jax.ShapeDtypeStruct((n, D_MODEL), F32), jax.ShapeDtypeStruct((n * SLAB, 128), F32)),
        grid=(n // tm,),
        in_specs=[pl.BlockSpec((tm, D_MODEL), lambda i: (i, 0)),
                  pl.BlockSpec((D_MODEL, D_MODEL), lambda i: (0, 0)),
                  pl.BlockSpec((tm, D_MODEL), lambda i: (i, 0)),
                  pl.BlockSpec((1, D_MODEL), lambda i: (0, 0)),
                  pl.BlockSpec((1, D_MODEL), lambda i: (0, 0))],
        out_specs=(pl.BlockSpec((tm, D_MODEL), lambda i: (i, 0)), pl.BlockSpec((tm * SLAB, 128), lambda i: (i, 0))),
        compiler_params=_params(1),
        name="out_proj_ln1",
    )(merged, w_out, x, g.reshape(1, D_MODEL).astype(F32), b.reshape(1, D_MODEL).astype(F32))


def _first_argmax(x, row_ids, n_rows):
    mx = jnp.max(x, axis=0, keepdims=True)
    idx = jnp.min(jnp.where(x == mx, row_ids, n_rows), axis=0, keepdims=True)
    return mx, idx


def _router_body(h_ref, w_ref, bias_ref, e_ref, g_ref, *, n_exp, tm):
    per = n_exp // N_GROUPS
    logits = lax.dot_general(w_ref[...], h_ref[...], (((1,), (1,)), ((), ())),
                             precision=HIGHEST, preferred_element_type=F32)
    scores = _sigmoid(logits)
    choice = scores + bias_ref[...]
    ids_g = lax.broadcasted_iota(jnp.int32, (per, tm), 0)
    group_rows = []
    for g in range(N_GROUPS):
        xg = choice[g * per:(g + 1) * per, :]
        m1, i1 = _first_argmax(xg, ids_g, per)
        m2 = jnp.max(jnp.where(ids_g == i1, -jnp.inf, xg), axis=0, keepdims=True)
        group_rows.append(m1 + m2)
    gs = jnp.concatenate(group_rows, axis=0)
    ids_n = lax.broadcasted_iota(jnp.int32, (N_GROUPS, tm), 0)
    keep = jnp.zeros((N_GROUPS, tm), F32)
    for _ in range(TOPK_GROUPS):
        _, gi = _first_argmax(gs, ids_n, N_GROUPS)
        hit = ids_n == gi
        keep = jnp.where(hit, 1.0, keep)
        gs = jnp.where(hit, -jnp.inf, gs)
    masked = jnp.concatenate(
        [jnp.where(keep[g:g + 1, :] > 0.5, choice[g * per:(g + 1) * per, :], -jnp.inf)
         for g in range(N_GROUPS)], axis=0)
    ids_e = lax.broadcasted_iota(jnp.int32, (n_exp, tm), 0)
    top_idx = []
    top_w = []
    for _ in range(TOP_K):
        _, ei = _first_argmax(masked, ids_e, n_exp)
        hit = ids_e == ei
        top_idx.append(ei)
        top_w.append(jnp.sum(jnp.where(hit, scores, 0.0), axis=0, keepdims=True))
        masked = jnp.where(hit, -jnp.inf, masked)
    gw = jnp.concatenate(top_w, axis=0)
    gw = gw / (jnp.sum(gw, axis=0, keepdims=True) + 1e-20) * ROUTED_SCALE
    e_ref[...] = jnp.concatenate(top_idx, axis=0)
    g_ref[...] = gw


def _router(h, router_w, router_bias, *, tm=256):
    n = h.shape[0]
    n_exp = router_w.shape[1]
    tm = min(tm, n)
    return pl.pallas_call(
        functools.partial(_router_body, n_exp=n_exp, tm=tm),
        out_shape=(jax.ShapeDtypeStruct((TOP_K, n), jnp.int32), jax.ShapeDtypeStruct((TOP_K, n), F32)),
        grid=(n // tm,),
        in_specs=[pl.BlockSpec((tm, D_MODEL), lambda i: (i, 0)),
                  pl.BlockSpec((n_exp, D_MODEL), lambda i: (0, 0)),
                  pl.BlockSpec((n_exp, 1), lambda i: (0, 0))],
        out_specs=(pl.BlockSpec((TOP_K, tm), lambda i: (0, i)), pl.BlockSpec((TOP_K, tm), lambda i: (0, i))),
        compiler_params=_params(1),
        name="moe_router",
    )(h, router_w.T.astype(F32), router_bias.reshape(n_exp, 1).astype(F32))


def _move_body(si_ref, di_ref, src_ref, dst_ref, sem, *, rows, steps):
    i = pl.program_id(0)

    def issue(r, carry):
        s0 = pl.multiple_of(si_ref[0, 0, r] * SLAB, SLAB)
        d0 = pl.multiple_of(di_ref[0, 0, r] * SLAB, SLAB)
        pltpu.make_async_copy(src_ref.at[pl.ds(s0, SLAB)], dst_ref.at[pl.ds(d0, SLAB)], sem).start()
        return carry

    lax.fori_loop(0, rows, issue, 0, unroll=8)

    def drain():
        pltpu.make_async_copy(dst_ref.at[pl.ds(0, rows * SLAB)], dst_ref.at[pl.ds(0, rows * SLAB)], sem).wait()

    @pl.when(i > 0)
    def _():
        drain()

    @pl.when(i == steps - 1)
    def _():
        drain()


def _move_rows(src, src_idx, dst_idx, n_dst, *, rows=256):
    n_idx = src_idx.shape[0]
    rows = min(rows, n_idx)
    steps = n_idx // rows
    idx_spec = pl.BlockSpec((1, 1, rows), lambda i: (i, 0, 0), memory_space=pltpu.SMEM)
    return pl.pallas_call(
        functools.partial(_move_body, rows=rows, steps=steps),
        out_shape=jax.ShapeDtypeStruct((n_dst * SLAB, 128), src.dtype),
        grid=(steps,),
        in_specs=[idx_spec, idx_spec, pl.BlockSpec(memory_space=pl.ANY)],
        out_specs=pl.BlockSpec(memory_space=pl.ANY),
        scratch_shapes=[pltpu.SemaphoreType.DMA(())],
        compiler_params=_params(1),
        name="row_move",
    )(src_idx.reshape(steps, 1, rows).astype(jnp.int32), dst_idx.reshape(steps, 1, rows).astype(jnp.int32), src)


def _expert_body(be_ref, nu_ref, x_ref, wg_ref, wu_ref, wd_ref, o_ref, wgb_ref, wub_ref, wdb_ref, *, tb):
    i = pl.program_id(0)
    used = i < nu_ref[0]
    fresh = jnp.logical_or(i == 0, be_ref[i] != be_ref[jnp.maximum(i - 1, 0)])

    @pl.when(jnp.logical_and(used, fresh))
    def _():
        wgb_ref[...] = wg_ref[...].astype(BF16)
        wub_ref[...] = wu_ref[...].astype(BF16)
        wdb_ref[...] = wd_ref[...].astype(BF16)

    @pl.when(used)
    def _():
        xb = _load_slabs(x_ref, tb).astype(BF16)
        gate = jnp.dot(xb, wgb_ref[...], preferred_element_type=F32)
        up = jnp.dot(xb, wub_ref[...], preferred_element_type=F32)
        act = (_silu(gate) * up).astype(BF16)
        _store_slabs(o_ref, jnp.dot(act, wdb_ref[...], preferred_element_type=F32), tb)

    @pl.when(jnp.logical_not(used))
    def _():
        o_ref[...] = jnp.zeros_like(o_ref)


def _experts(xs, block_e, n_used, w_gate, w_up, w_down, *, tb):
    n_slots = xs.shape[0] // SLAB
    ff = w_gate.shape[2]
    grid_spec = pltpu.PrefetchScalarGridSpec(
        num_scalar_prefetch=2,
        grid=(n_slots // tb,),
        in_specs=[pl.BlockSpec((tb * SLAB, 128), lambda i, be, nu: (i, 0)),
                  pl.BlockSpec((None, D_MODEL, ff), lambda i, be, nu: (be[i], 0, 0)),
                  pl.BlockSpec((None, D_MODEL, ff), lambda i, be, nu: (be[i], 0, 0)),
                  pl.BlockSpec((None, ff, D_MODEL), lambda i, be, nu: (be[i], 0, 0))],
        out_specs=pl.BlockSpec((tb * SLAB, 128), lambda i, be, nu: (i, 0)),
        scratch_shapes=[pltpu.VMEM((D_MODEL, ff), BF16), pltpu.VMEM((D_MODEL, ff), BF16),
                        pltpu.VMEM((ff, D_MODEL), BF16)],
    )
    return pl.pallas_call(
        functools.partial(_expert_body, tb=tb),
        out_shape=jax.ShapeDtypeStruct((n_slots * SLAB, 128), F32),
        grid_spec=grid_spec,
        compiler_params=_params(1),
        name="moe_experts",
    )(block_e, n_used, xs, w_gate, w_up, w_down)


def _combine_body(h_ref, y_ref, gw_ref, sgu_ref, sd_ref, g_ref, b_ref, o_ref, *, ff, tm):
    h = h_ref[...]
    hb = h.astype(BF16)
    gu = jnp.dot(hb, sgu_ref[...], preferred_element_type=F32)
    act = (_silu(gu[:, :ff]) * gu[:, ff:]).astype(BF16)
    f = jnp.dot(act, sd_ref[...], preferred_element_type=F32)
    gw = gw_ref[...]
    for k in range(TOP_K):
        f = f + _load_slabs(y_ref, tm, first=k * SLAB, stride=TOP_K * SLAB) * gw[:, k:k + 1]
    o_ref[...] = _layer_norm(DN_ALPHA * h + f, g_ref[...], b_ref[...])


def _combine(h, y_tok, gw_t, sh_gu, sh_down, g, b, tok_off, n_rows, *, tm=128):
    ff = sh_down.shape[0]
    tm = min(tm, n_rows)
    off = tok_off // tm
    return pl.pallas_call(
        functools.partial(_combine_body, ff=ff, tm=tm),
        out_shape=jax.ShapeDtypeStruct((n_rows, D_MODEL), F32),
        grid=(n_rows // tm,),
        in_specs=[pl.BlockSpec((tm, D_MODEL), lambda i: (off + i, 0)),
                  pl.BlockSpec((tm * TOP_K * SLAB, 128), lambda i: (off + i, 0)),
                  pl.BlockSpec((tm, TOP_K), lambda i: (off + i, 0)),
                  pl.BlockSpec((D_MODEL, 2 * ff), lambda i: (0, 0)),
                  pl.BlockSpec((ff, D_MODEL), lambda i: (0, 0)),
                  pl.BlockSpec((1, D_MODEL), lambda i: (0, 0)),
                  pl.BlockSpec((1, D_MODEL), lambda i: (0, 0))],
        out_specs=pl.BlockSpec((tm, D_MODEL), lambda i: (i, 0)),
        compiler_params=_params(1),
        name="moe_combine_ln2",
    )(h, y_tok, gw_t, sh_gu, sh_down, g.reshape(1, D_MODEL).astype(F32), b.reshape(1, D_MODEL).astype(F32))


def _routing_tables(top_e, n_exp, tb):
    k, n = top_e.shape
    a = k * n
    n_blocks = a // tb + n_exp
    a_bits = max(a - 1, 1).bit_length()
    assert n_exp << a_bits < 2 ** 31
    ids = jnp.arange(a, dtype=jnp.int32)
    keys = lax.sort(top_e.reshape(-1) * (1 << a_bits) + ids)
    e_sorted = keys >> a_bits
    a_sorted = keys & ((1 << a_bits) - 1)
    tok_sorted = a_sorted % n
    k_sorted = a_sorted // n
    experts = jnp.arange(n_exp, dtype=jnp.int32)
    start = jnp.sum((e_sorted[:, None] < experts[None, :]).astype(jnp.int32), axis=0)
    counts = jnp.concatenate([start[1:], jnp.full((1,), a, jnp.int32)]) - start
    padded = (counts + tb - 1) // tb * tb
    pad_end = jnp.cumsum(padded)
    shift = pad_end - padded - start
    dest = ids + jnp.sum(jnp.where(e_sorted[:, None] == experts[None, :], shift[None, :], 0), axis=1)
    slot_tok = jnp.zeros((n_blocks * tb,), jnp.int32).at[dest].set(tok_sorted)
    block_start = jnp.arange(n_blocks, dtype=jnp.int32) * tb
    block_e = jnp.minimum(jnp.sum(block_start[:, None] >= pad_end[None, :], axis=1), n_exp - 1).astype(jnp.int32)
    n_used = (pad_end[-1] // tb).astype(jnp.int32).reshape(1)
    block_e = jnp.where(block_start < pad_end[-1], block_e, block_e[jnp.maximum(n_used[0] - 1, 0)])
    pos_tok = tok_sorted * k + k_sorted
    return slot_tok, dest, pos_tok, block_e, n_used


def _layer(xs, layer_idx, w_in, conv_w, a_log, dt_bias, norm_w, lam, subln_w, w_branch_gdn, w_branch_diff,
           w_out, ln1_g, ln1_b, router_w, router_bias, w_gate, w_up, w_down, sh_gate, sh_up, sh_down,
           ln2_g, ln2_b, *, moe_tb=256):
    seqs = [(x.shape[0], x.shape[1]) for x in xs]
    x = jnp.concatenate([x.reshape(-1, D_MODEL) for x in xs], axis=0)
    n = x.shape[0]
    seq_starts, seq_ends, group_off = [], [], []
    off = 0
    for b, t in seqs:
        group_off.append(off)
        for _ in range(b):
            seq_starts.append(off)
            off += t
            seq_ends.append(off)
    seq_starts, seq_ends = tuple(seq_starts), tuple(seq_ends)

    o_qkv, o_z, o_ab = GDN_CONV_CH, GDN_CONV_CH + GDN_VAL_W, GDN_CONV_CH + GDN_VAL_W + GDN_GATE_W
    w_main = jnp.concatenate([w_in[:, :o_z], w_in[:, o_ab:]], axis=1).astype(BF16)
    w_ab = w_in[:, o_z:o_ab].astype(BF16)

    p = _proj_in(x, w_main)
    gates = _gdn_gates(x, w_ab, a_log, dt_bias)
    g_t = gates.T.reshape(GDN_GATE_W, 1, n)
    act = _gdn_conv(p, conv_w.astype(F32), seq_starts, seq_ends)
    o_fwd = _gdn_scan(act, g_t, p, None, norm_w, seq_starts, seq_ends, reverse=False)
    o_gdn = _gdn_scan(act, g_t, p, o_fwd, norm_w, seq_starts, seq_ends, reverse=True)
    o_diff = jnp.concatenate(
        [_diff_attention(p, lam, subln_w, goff, b, t, layer_idx) for (b, t), goff in zip(seqs, group_off)], axis=0)
    merged = _merge(o_gdn, o_diff, w_branch_gdn.astype(BF16), w_branch_diff.astype(BF16), p)
    h, h_slab = _out_ln(merged, w_out.astype(BF16), x, ln1_g, ln1_b)

    n_exp = router_w.shape[1]
    top_e, gw = _router(h, router_w, router_bias)
    slot_tok, slot_of, pos_tok, block_e, n_used = _routing_tables(top_e, n_exp, moe_tb)
    n_slots = slot_tok.shape[0]
    xs_sorted = _move_rows(h_slab, slot_tok, jnp.arange(n_slots, dtype=jnp.int32), n_slots)
    ys = _experts(xs_sorted, block_e, n_used, w_gate, w_up, w_down, tb=moe_tb)
    y_tok = _move_rows(ys, slot_of, pos_tok, TOP_K * n)
    sh_gu = jnp.concatenate([sh_gate, sh_up], axis=1).astype(BF16)
    outs = []
    for (b, t), goff in zip(seqs, group_off):
        y = _combine(h, y_tok, gw.T, sh_gu, sh_down.astype(BF16), ln2_g, ln2_b, goff, b * t)
        outs.append(y.reshape(b, t, D_MODEL))
    return outs


def kernel(x_prompt, x_sample, w_in, gdn_conv_w, gdn_a_log, gdn_dt_bias, gdn_norm_w, diff_lambda, diff_subln_w, w_branch_gdn, w_branch_diff, w_out, ln1_g, ln1_b, router_w, router_bias, exp_w_gate, exp_w_up, exp_w_down, sh_w_gate, sh_w_up, sh_w_down, ln2_g, ln2_b):
    xs = [x_prompt, x_sample]
    for l in range(DEPTH):
        xs = _layer(xs, l, w_in[l], gdn_conv_w[l], gdn_a_log[l], gdn_dt_bias[l], gdn_norm_w[l], diff_lambda[l],
                    diff_subln_w[l], w_branch_gdn[l], w_branch_diff[l], w_out[l], ln1_g[l], ln1_b[l], router_w[l],
                    router_bias[l], exp_w_gate[l], exp_w_up[l], exp_w_down[l], sh_w_gate[l], sh_w_up[l],
                    sh_w_down[l], ln2_g[l], ln2_b[l])
    return (xs[0], xs[1])
```

```python
import functools
import math

import jax
import jax.numpy as jnp
from jax import lax
from jax.experimental import pallas as pl
from jax.experimental.pallas import tpu as pltpu

F32 = jnp.float32
BF16 = jnp.bfloat16
HIGHEST = lax.Precision.HIGHEST

D_MODEL = 2048
GDN_QK_HEADS = 16
GDN_V_HEADS = 32
GDN_DK = 128
GDN_DV = 128
GDN_KEY_W = GDN_QK_HEADS * GDN_DK
GDN_VAL_W = GDN_V_HEADS * GDN_DV
GDN_CONV_CH = 2 * GDN_KEY_W + GDN_VAL_W
GDN_GATE_W = 4 * GDN_V_HEADS
GDN_CONV_K = 5
GDN_CHUNK = 64
DIFF_HEADS = 8
DIFF_HD = 128
DIFF_QK_W = 2 * DIFF_HEADS * DIFF_HD
DIFF_VAL_W = DIFF_HEADS * 2 * DIFF_HD
N_GROUPS = 8
TOPK_GROUPS = 4
TOP_K = 8
ROUTED_SCALE = 2.5
DEPTH = 1
DN_ALPHA = (2 * DEPTH) ** 0.25
LN_EPS = 1e-5
RMS_EPS = 1e-6
NEG_BIG = -1e30
LOG2E = math.log2(math.e)

P_QKV = 0
P_Z = P_QKV + GDN_CONV_CH
P_DQ = P_Z + GDN_VAL_W
P_DK = P_DQ + DIFF_QK_W
P_DV = P_DK + DIFF_QK_W
P_GATES = P_DV + DIFF_VAL_W
P_COLS = P_GATES + 2 * D_MODEL

VMEM_LIMIT = 56 * 1024 * 1024
SLAB = D_MODEL // 128


def _params(grid_rank, vmem=VMEM_LIMIT):
    return pltpu.CompilerParams(dimension_semantics=("arbitrary",) * grid_rank, vmem_limit_bytes=vmem)


def _sigmoid(x):
    return 1.0 / (1.0 + jnp.exp(-x))


def _silu(x):
    return x * _sigmoid(x)


def _layer_norm(x, g, b):
    mu = jnp.mean(x, axis=-1, keepdims=True)
    xc = x - mu
    var = jnp.mean(xc * xc, axis=-1, keepdims=True)
    return xc * lax.rsqrt(var + LN_EPS) * g + b


def _proj_in_body(x_ref, w_ref, o_ref, xb_ref, *, tn):
    j = pl.program_id(1)

    @pl.when(j == 0)
    def _():
        xb_ref[...] = x_ref[...].astype(BF16)

    acc = jnp.dot(xb_ref[...], w_ref[...], preferred_element_type=F32)
    col = j * tn
    is_gate = col >= P_GATES
    is_dq = jnp.logical_and(col >= P_DQ, col < P_DK)
    scale = jnp.where(is_dq, LOG2E * DIFF_HD ** -0.5, 1.0).astype(F32)

    @pl.when(is_gate)
    def _():
        o_ref[...] = _sigmoid(acc).astype(o_ref.dtype)

    @pl.when(jnp.logical_not(is_gate))
    def _():
        o_ref[...] = (acc * scale).astype(o_ref.dtype)


def _proj_in(x, w, *, tm=1024, tn=512):
    n, k = x.shape
    m = w.shape[1]
    tm = min(tm, n)
    return pl.pallas_call(
        functools.partial(_proj_in_body, tn=tn),
        out_shape=jax.ShapeDtypeStruct((n, m), BF16),
        grid=(n // tm, m // tn),
        in_specs=[pl.BlockSpec((tm, k), lambda i, j: (i, 0)),
                  pl.BlockSpec((k, tn), lambda i, j: (0, j))],
        out_specs=pl.BlockSpec((tm, tn), lambda i, j: (i, j)),
        scratch_shapes=[pltpu.VMEM((tm, k), BF16)],
        compiler_params=_params(2),
        name="proj_in",
    )(x, w)


def _gates_body(x_ref, w_ref, prm_ref, o_ref, *, tt):
    ab = jnp.dot(x_ref[...].astype(BF16), w_ref[...], preferred_element_type=F32)
    a_log = prm_ref[0:1, :]
    dt_bias = prm_ref[1:2, :]
    y = ab + dt_bias
    softplus = jnp.maximum(y, 0.0) + jnp.log(1.0 + jnp.exp(-jnp.abs(y)))
    log_decay = -jnp.exp(a_log) * softplus
    beta = _sigmoid(ab)
    ri = lax.broadcasted_iota(jnp.int32, (tt, tt), 0)
    ci = lax.broadcasted_iota(jnp.int32, (tt, tt), 1)
    same = (ri // GDN_CHUNK) == (ci // GDN_CHUNK)
    m_lo = jnp.where(same, jnp.where(ci <= ri, 1.0, 0.0), 0.0).astype(F32)
    m_up = jnp.where(same, jnp.where(ci >= ri, 1.0, 0.0), 0.0).astype(F32)
    c_lo = jnp.dot(m_lo, log_decay, precision=HIGHEST, preferred_element_type=F32)
    c_up = jnp.dot(m_up, log_decay, precision=HIGHEST, preferred_element_type=F32)
    lane = lax.broadcasted_iota(jnp.int32, (tt, GDN_GATE_W), 1)
    is_a = (lane % (2 * GDN_V_HEADS)) < GDN_V_HEADS
    is_bwd = lane >= 2 * GDN_V_HEADS
    o_ref[...] = jnp.where(is_a, jnp.where(is_bwd, c_up, c_lo), beta)


def _gdn_gates(x, w_ab, a_log, dt_bias, *, tt=512):
    n, k = x.shape
    tt = min(tt, n)
    zeros = jnp.zeros((GDN_V_HEADS,), F32)
    prm = jnp.zeros((8, GDN_GATE_W), F32)
    prm = prm.at[0].set(jnp.concatenate([a_log[0], zeros, a_log[1], zeros]).astype(F32))
    prm = prm.at[1].set(jnp.concatenate([dt_bias[0], zeros, dt_bias[1], zeros]).astype(F32))
    return pl.pallas_call(
        functools.partial(_gates_body, tt=tt),
        out_shape=jax.ShapeDtypeStruct((n, GDN_GATE_W), F32),
        grid=(n // tt,),
        in_specs=[pl.BlockSpec((tt, k), lambda i: (i, 0)),
                  pl.BlockSpec((k, GDN_GATE_W), lambda i: (0, 0)),
                  pl.BlockSpec((8, GDN_GATE_W), lambda i: (0, 0))],
        out_specs=pl.BlockSpec((tt, GDN_GATE_W), lambda i: (i, 0)),
        compiler_params=_params(1),
        name="gdn_gates",
    )(x, w_ab, prm)


CONV_HALO = 16


def _any_equal(value, constants):
    hit = value == constants[0]
    for c in constants[1:]:
        hit = jnp.logical_or(hit, value == c)
    return hit


def _conv_body(prev_ref, cur_ref, next_ref, w_ref, o_ref, ext_ref, *, tt, tc, seq_starts, seq_ends):
    i = pl.program_id(0)
    j = pl.program_id(1)
    t0 = i * tt
    at_start = _any_equal(t0, seq_starts)
    at_end = _any_equal(t0 + tt, seq_ends)
    prev = prev_ref[...].astype(F32)[CONV_HALO - 8:, :]
    nxt = next_ref[...].astype(F32)[:8, :]
    ext_ref[0:8, :] = jnp.where(at_start, 0.0, prev)
    ext_ref[8:8 + tt, :] = cur_ref[...].astype(F32)
    ext_ref[8 + tt:16 + tt, :] = jnp.where(at_end, 0.0, nxt)
    pad = (GDN_CONV_K - 1) // 2
    acc = jnp.zeros((tt, tc), F32)
    for tap in range(GDN_CONV_K):
        acc = acc + ext_ref[pl.ds(8 - pad + tap, tt), :] * w_ref[tap:tap + 1, :]
    act = _silu(acc)
    col = j * tc
    is_qk = col < 2 * GDN_KEY_W
    q_scale = jnp.where(col < GDN_KEY_W, GDN_DK ** -0.5, 1.0).astype(F32)
    for s in range(tc // GDN_DK):
        a = act[:, s * GDN_DK:(s + 1) * GDN_DK]
        ss = jnp.sum(a * a, axis=-1, keepdims=True)
        normed = a * (lax.rsqrt(ss + RMS_EPS) * q_scale)
        o_ref[:, s * GDN_DK:(s + 1) * GDN_DK] = jnp.where(is_qk, normed, a).astype(o_ref.dtype)


def _gdn_conv(p, conv_w, seq_starts, seq_ends, *, tt=512, tc=512):
    n = p.shape[0]
    tt = min(tt, n)
    hb = tt // CONV_HALO
    n_halo = n // CONV_HALO
    return pl.pallas_call(
        functools.partial(_conv_body, tt=tt, tc=tc, seq_starts=seq_starts, seq_ends=seq_ends),
        out_shape=jax.ShapeDtypeStruct((n, GDN_CONV_CH), BF16),
        grid=(n // tt, GDN_CONV_CH // tc),
        in_specs=[pl.BlockSpec((CONV_HALO, tc), lambda i, j: (jnp.maximum(i * hb - 1, 0), j)),
                  pl.BlockSpec((tt, tc), lambda i, j: (i, j)),
                  pl.BlockSpec((CONV_HALO, tc), lambda i, j: (jnp.minimum((i + 1) * hb, n_halo - 1), j)),
                  pl.BlockSpec((GDN_CONV_K, tc), lambda i, j: (0, j))],
        out_specs=pl.BlockSpec((tt, tc), lambda i, j: (i, j)),
        scratch_shapes=[pltpu.VMEM((tt + 16, tc), F32)],
        compiler_params=_params(2),
        name="gdn_conv",
    )(p, p, p, conv_w)


def _dot_f32(a, b):
    return jnp.dot(a, b, precision=HIGHEST, preferred_element_type=F32)


def _dot_bf16(a, b):
    return jnp.dot(a.astype(BF16), b.astype(BF16), preferred_element_type=F32)


def _gdn_scan_body(*refs, tt, nt, hp, reverse, seq_starts, seq_ends):
    if reverse:
        q_ref, k_ref, v_ref, g_ref, b_ref, ofwd_ref, z_ref, nw_ref, o_ref, s_ref = refs
    else:
        q_ref, k_ref, v_ref, g_ref, b_ref, o_ref, s_ref = refs
    C = GDN_CHUNK
    t = pl.program_id(1)
    tok0 = ((nt - 1 - t) if reverse else t) * tt
    reset = _any_equal(tok0 + tt, seq_ends) if reverse else _any_equal(tok0, seq_starts)

    @pl.when(reset)
    def _():
        s_ref[...] = jnp.zeros_like(s_ref)

    n_chunks = tt // C
    ri = lax.broadcasted_iota(jnp.int32, (tt, tt), 0)
    ci = lax.broadcasted_iota(jnp.int32, (tt, tt), 1)
    same = (ri // C) == (ci // C)
    eye = ri == ci
    causal = jnp.logical_and(same, (ci >= ri) if reverse else (ci <= ri))
    strict = jnp.logical_and(same, (ci > ri) if reverse else (ci < ri))
    eye_f = jnp.where(eye, 1.0, 0.0).astype(F32)
    last = 0 if reverse else C - 1
    order = range(n_chunks - 1, -1, -1) if reverse else range(n_chunks)
    chunk_of_lane = lax.broadcasted_iota(jnp.int32, (1, tt), 1) // C
    chunk_of_lane_k = lax.broadcasted_iota(jnp.int32, (GDN_DK, tt), 1) // C

    shared = []
    for pp in range(hp):
        q = q_ref[:, pp * GDN_DK:(pp + 1) * GDN_DK]
        k = k_ref[:, pp * GDN_DK:(pp + 1) * GDN_DK]
        kf = k.astype(F32)
        kt = kf.T
        ktb = kt.astype(BF16)
        kk = jnp.dot(k, ktb, preferred_element_type=F32)
        qk = jnp.dot(q, ktb, preferred_element_type=F32)
        shared.append((kf, q.astype(F32), kt, kk, qk))
    for hh in range(2 * hp):
        kf, qf, kt, kk, qk = shared[hh // 2]
        cols = slice(hh * GDN_DV, (hh + 1) * GDN_DV)
        g_row = g_ref[hh]
        b_row = b_ref[hh]
        g_col = jnp.sum(jnp.where(eye, g_row, 0.0), axis=1, keepdims=True)
        b_col = jnp.sum(jnp.where(eye, b_row, 0.0), axis=1, keepdims=True)
        g_last = [g_row[:, c * C + last:c * C + last + 1] for c in range(n_chunks)]
        g_last_row = jnp.zeros((1, tt), F32)
        for c in range(n_chunks):
            g_last_row = jnp.where(chunk_of_lane == c, g_last[c], g_last_row)
        decay = jnp.exp(jnp.where(causal, g_col - g_row, NEG_BIG))
        low = jnp.where(strict, kk * decay, 0.0) * b_col
        low_b = low.astype(BF16)
        pw = jnp.dot(low_b, low_b, preferred_element_type=F32)
        inv = eye_f - low
        n_factors = int(math.log2(C)) - 1
        for j in range(n_factors):
            pw_b = pw.astype(BF16)
            if j < n_factors - 1:
                both = jnp.dot(jnp.concatenate([inv, pw], axis=0).astype(BF16), pw_b, preferred_element_type=F32)
                inv = inv + both[:tt]
                pw = both[tt:]
            else:
                inv = inv + jnp.dot(inv.astype(BF16), pw_b, preferred_element_type=F32)
        eg_col = jnp.exp(g_col)
        v = v_ref[:, cols].astype(F32)
        x = jnp.concatenate([v * b_col, kf * (b_col * eg_col)], axis=1)
        uw = jnp.dot(inv.astype(BF16), x.astype(BF16), preferred_element_type=F32).astype(BF16)
        from_q = jnp.dot((qk * decay).astype(BF16), uw, preferred_element_type=F32)
        ktd = kt * jnp.exp(g_last_row - g_row)
        lhs_k = jnp.concatenate([jnp.where(chunk_of_lane_k == c, ktd, 0.0) for c in range(n_chunks)],
                                axis=0).astype(BF16)
        from_k = jnp.dot(lhs_k, uw, preferred_element_type=F32)
        q_eff = qf * eg_col - from_q[:, GDN_DV:]
        for c in order:
            rows = slice(c * C, (c + 1) * C)
            krows = slice(c * GDN_DK, (c + 1) * GDN_DK)
            lhs = jnp.concatenate([q_eff[rows], from_k[krows, GDN_DV:]], axis=0).astype(BF16)
            state = s_ref[hh]
            res = jnp.dot(lhs, state.astype(BF16), preferred_element_type=F32)
            o = res[:C] + from_q[rows, :GDN_DV]
            s_ref[hh] = state * jnp.exp(g_last[c]) - res[C:] + from_k[krows, :GDN_DV]
            if reverse:
                tot = o + ofwd_ref[rows, cols]
                ms = jnp.mean(tot * tot, axis=-1, keepdims=True)
                z = z_ref[rows, cols].astype(F32)
                o_ref[rows, cols] = (tot * lax.rsqrt(ms + RMS_EPS) * nw_ref[...] * _silu(z)).astype(o_ref.dtype)
            else:
                o_ref[rows, cols] = o


def _gdn_scan(act, g_t, p, o_fwd, norm_w, seq_starts, seq_ends, *, reverse, tt=256, hp=2):
    n = act.shape[0]
    tt = min(tt, n)
    nt = n // tt
    groups = GDN_QK_HEADS // hp
    kw = hp * GDN_DK
    vw = 2 * hp * GDN_DV

    def tmap(t):
        return (nt - 1 - t) if reverse else t

    gate_blocks = (2 * GDN_V_HEADS) // (2 * hp)
    dirn = 1 if reverse else 0
    in_specs = [
        pl.BlockSpec((tt, kw), lambda h, t: (tmap(t), h)),
        pl.BlockSpec((tt, kw), lambda h, t: (tmap(t), groups + h)),
        pl.BlockSpec((tt, vw), lambda h, t: (tmap(t), 2 * GDN_KEY_W // vw + h)),
        pl.BlockSpec((2 * hp, 1, tt), lambda h, t: (dirn * gate_blocks + h, 0, tmap(t))),
        pl.BlockSpec((2 * hp, 1, tt), lambda h, t: (dirn * gate_blocks + groups + h, 0, tmap(t))),
    ]
    args = [act, act, act, g_t, g_t]
    if reverse:
        in_specs += [
            pl.BlockSpec((tt, vw), lambda h, t: (tmap(t), h)),
            pl.BlockSpec((tt, vw), lambda h, t: (tmap(t), P_Z // vw + h)),
            pl.BlockSpec((1, GDN_DV), lambda h, t: (0, 0)),
        ]
        args += [o_fwd, p, norm_w.reshape(1, GDN_DV).astype(F32)]
        out_dtype = BF16
    else:
        out_dtype = F32
    return pl.pallas_call(
        functools.partial(_gdn_scan_body, tt=tt, nt=nt, hp=hp, reverse=reverse,
                          seq_starts=seq_starts, seq_ends=seq_ends),
        out_shape=jax.ShapeDtypeStruct((n, GDN_VAL_W), out_dtype),
        grid=(groups, nt),
        in_specs=in_specs,
        out_specs=pl.BlockSpec((tt, vw), lambda h, t: (tmap(t), h)),
        scratch_shapes=[pltpu.VMEM((2 * hp, GDN_DK, GDN_DV), F32)],
        compiler_params=_params(2),
        name="gdn_scan_bwd" if reverse else "gdn_scan_fwd",
    )(*args)


def _attn_body(slope_ref, lam_ref, sw_ref, q_ref, k_ref, v_ref, o_ref, m_ref, l_ref, acc_ref, sd_ref,
               *, tile, nk, lam_init):
    h = pl.program_id(1)
    qi = pl.program_id(2)
    kj = pl.program_id(3)
    slope = slope_ref[h]

    @pl.when(kj == 0)
    def _():
        m_ref[...] = jnp.full_like(m_ref, NEG_BIG)
        l_ref[...] = jnp.zeros_like(l_ref)
        acc_ref[...] = jnp.zeros_like(acc_ref)
        ii = lax.broadcasted_iota(jnp.int32, (tile, tile), 0)
        jj = lax.broadcasted_iota(jnp.int32, (tile, tile), 1)
        sd_ref[...] = (ii - jj).astype(F32) * slope

    v = v_ref[...]
    tile_dist = jnp.full((tile, 1), (qi - kj) * tile, jnp.int32).astype(F32) * slope

    def update(add_bias, shift):
        for m in range(2):
            cols = slice(m * DIFF_HD, (m + 1) * DIFF_HD)
            t = add_bias(lax.dot_general(q_ref[:, cols], k_ref[:, cols], (((1,), (1,)), ((), ())),
                                         preferred_element_type=F32))
            m_old = m_ref[m]
            m_new = jnp.maximum(m_old, jnp.max(t, axis=1, keepdims=True) - shift)
            alpha = jnp.exp2(m_old - m_new)
            p = jnp.exp2(t - (m_new + shift))
            l_ref[m] = alpha * l_ref[m] + jnp.sum(p, axis=1, keepdims=True)
            acc_ref[m] = alpha * acc_ref[m] + jnp.dot(p.astype(BF16), v, preferred_element_type=F32)
            m_ref[m] = m_new

    @pl.when(kj < qi)
    def _():
        update(lambda s: s - sd_ref[...], tile_dist)

    @pl.when(kj > qi)
    def _():
        update(lambda s: s + sd_ref[...], -tile_dist)

    @pl.when(kj == qi)
    def _():
        update(lambda s: s - jnp.abs(sd_ref[...]), jnp.zeros((tile, 1), F32))

    @pl.when(kj == nk - 1)
    def _():
        lam = lam_ref[...]
        lam_full = (jnp.exp(jnp.sum(lam[0:1, :] * lam[1:2, :], axis=-1, keepdims=True))
                    - jnp.exp(jnp.sum(lam[2:3, :] * lam[3:4, :], axis=-1, keepdims=True)) + lam_init)
        o = acc_ref[0] / l_ref[0] - lam_full * (acc_ref[1] / l_ref[1])
        ms = jnp.mean(o * o, axis=-1, keepdims=True)
        o_ref[...] = (o * lax.rsqrt(ms + LN_EPS) * sw_ref[...] * (1.0 - lam_init)).astype(o_ref.dtype)


def _diff_attention(p, lam, subln_w, tok_off, n_seq, seq_len, layer_idx, *, tile=512):
    tile = min(tile, seq_len)
    nt = seq_len // tile
    lam_init = 0.8 - 0.6 * math.exp(-0.3 * layer_idx)
    slopes = jnp.asarray([LOG2E * 2.0 ** (-8.0 * (i + 1) / DIFF_HEADS) for i in range(DIFF_HEADS)], F32)
    w2 = 2 * DIFF_HD
    off = tok_off // tile
    return pl.pallas_call(
        functools.partial(_attn_body, tile=tile, nk=nt, lam_init=lam_init),
        out_shape=jax.ShapeDtypeStruct((n_seq * seq_len, DIFF_VAL_W), BF16),
        grid=(n_seq, DIFF_HEADS, nt, nt),
        in_specs=[pl.BlockSpec(memory_space=pltpu.SMEM),
                  pl.BlockSpec((4, DIFF_HD), lambda b, h, i, j: (0, 0)),
                  pl.BlockSpec((1, w2), lambda b, h, i, j: (0, 0)),
                  pl.BlockSpec((tile, w2), lambda b, h, i, j: (off + b * nt + i, P_DQ // w2 + h)),
                  pl.BlockSpec((tile, w2), lambda b, h, i, j: (off + b * nt + j, P_DK // w2 + h)),
                  pl.BlockSpec((tile, w2), lambda b, h, i, j: (off + b * nt + j, P_DV // w2 + h))],
        out_specs=pl.BlockSpec((tile, w2), lambda b, h, i, j: (b * nt + i, h)),
        scratch_shapes=[pltpu.VMEM((2, tile, 1), F32), pltpu.VMEM((2, tile, 1), F32),
                        pltpu.VMEM((2, tile, w2), F32), pltpu.VMEM((tile, tile), F32)],
        compiler_params=_params(4),
        name="diff_attention",
    )(slopes, lam.astype(F32), subln_w.reshape(1, w2).astype(F32), p, p, p)


def _merge_body(og_ref, od_ref, wg_ref, wd_ref, gg_ref, gd_ref, o_ref):
    a = jnp.dot(og_ref[...], wg_ref[...], preferred_element_type=F32)
    b = jnp.dot(od_ref[...], wd_ref[...], preferred_element_type=F32)
    o_ref[...] = (gg_ref[...].astype(F32) * a + gd_ref[...].astype(F32) * b).astype(o_ref.dtype)


def _merge(o_gdn, o_diff, w_g, w_d, p, *, tm=512, tn=512):
    n = o_gdn.shape[0]
    tm = min(tm, n)
    return pl.pallas_call(
        _merge_body,
        out_shape=jax.ShapeDtypeStruct((n, D_MODEL), BF16),
        grid=(n // tm, D_MODEL // tn),
        in_specs=[pl.BlockSpec((tm, GDN_VAL_W), lambda i, j: (i, 0)),
                  pl.BlockSpec((tm, DIFF_VAL_W), lambda i, j: (i, 0)),
                  pl.BlockSpec((GDN_VAL_W, tn), lambda i, j: (0, j)),
                  pl.BlockSpec((DIFF_VAL_W, tn), lambda i, j: (0, j)),
                  pl.BlockSpec((tm, tn), lambda i, j: (i, P_GATES // tn + j)),
                  pl.BlockSpec((tm, tn), lambda i, j: (i, (P_GATES + D_MODEL) // tn + j))],
        out_specs=pl.BlockSpec((tm, tn), lambda i, j: (i, j)),
        compiler_params=_params(2),
        name="branch_merge",
    )(o_gdn, o_diff, w_g, w_d, p, p)


def _store_slabs(slab_ref, value, n_rows):
    for s in range(SLAB):
        slab_ref[pl.ds(s, n_rows, stride=SLAB), :] = value[:, s * 128:(s + 1) * 128]


def _load_slabs(slab_ref, n_rows, first=0, stride=SLAB):
    return jnp.concatenate([slab_ref[pl.ds(first + s, n_rows, stride=stride), :] for s in range(SLAB)], axis=1)


def _out_ln_body(m_ref, w_ref, x_ref, g_ref, b_ref, o_ref, slab_ref, *, tm):
    mix = jnp.dot(m_ref[...], w_ref[...], preferred_element_type=F32)
    h = _layer_norm(DN_ALPHA * x_ref[...] + mix, g_ref[...], b_ref[...])
    o_ref[...] = h
    _store_slabs(slab_ref, h, tm)


def _out_ln(merged, w_out, x, g, b, *, tm=512):
    n = x.shape[0]
    tm = min(tm, n)
    return pl.pallas_call(
        functools.partial(_out_ln_body, tm=tm),
        out_shape=(jax.ShapeDtypeStruct((n, D_MODEL), F32), jax.ShapeDtypeStruct((n * SLAB, 128), F32)),
        grid=(n // tm,),
        in_specs=[pl.BlockSpec((tm, D_MODEL), lambda i: (i, 0)),
                  pl.BlockSpec((D_MODEL, D_MODEL), lambda i: (0, 0)),
                  pl.BlockSpec((tm, D_MODEL), lambda i: (i, 0)),
                  pl.BlockSpec((1, D_MODEL), lambda i: (0, 0)),
                  pl.BlockSpec((1, D_MODEL), lambda i: (0, 0))],
        out_specs=(pl.BlockSpec((tm, D_MODEL), lambda i: (i, 0)), pl.BlockSpec((tm * SLAB, 128), lambda i: (i, 0))),
        compiler_params=_params(1),
        name="out_proj_ln1",
    )(merged, w_out, x, g.reshape(1, D_MODEL).astype(F32), b.reshape(1, D_MODEL).astype(F32))


def _first_argmax(x, row_ids, n_rows):
    mx = jnp.max(x, axis=0, keepdims=True)
    idx = jnp.min(jnp.where(x == mx, row_ids, n_rows), axis=0, keepdims=True)
    return mx, idx


def _router_body(h_ref, w_ref, bias_ref, e_ref, g_ref, *, n_exp, tm):
    per = n_exp // N_GROUPS
    logits = lax.dot_general(w_ref[...], h_ref[...], (((1,), (1,)), ((), ())),
                             precision=HIGHEST, preferred_element_type=F32)
    scores = _sigmoid(logits)
    choice = scores + bias_ref[...]
    ids_g = lax.broadcasted_iota(jnp.int32, (per, tm), 0)
    group_rows = []
    for g in range(N_GROUPS):
        xg = choice[g * per:(g + 1) * per, :]
        m1, i1 = _first_argmax(xg, ids_g, per)
        m2 = jnp.max(jnp.where(ids_g == i1, -jnp.inf, xg), axis=0, keepdims=True)
        group_rows.append(m1 + m2)
    gs = jnp.concatenate(group_rows, axis=0)
    ids_n = lax.broadcasted_iota(jnp.int32, (N_GROUPS, tm), 0)
    keep = jnp.zeros((N_GROUPS, tm), F32)
    for _ in range(TOPK_GROUPS):
        _, gi = _first_argmax(gs, ids_n, N_GROUPS)
        hit = ids_n == gi
        keep = jnp.where(hit, 1.0, keep)
        gs = jnp.where(hit, -jnp.inf, gs)
    masked = jnp.concatenate(
        [jnp.where(keep[g:g + 1, :] > 0.5, choice[g * per:(g + 1) * per, :], -jnp.inf)
         for g in range(N_GROUPS)], axis=0)
    ids_e = lax.broadcasted_iota(jnp.int32, (n_exp, tm), 0)
    top_idx = []
    top_w = []
    for _ in range(TOP_K):
        _, ei = _first_argmax(masked, ids_e, n_exp)
        hit = ids_e == ei
        top_idx.append(ei)
        top_w.append(jnp.sum(jnp.where(hit, scores, 0.0), axis=0, keepdims=True))
        masked = jnp.where(hit, -jnp.inf, masked)
    gw = jnp.concatenate(top_w, axis=0)
    gw = gw / (jnp.sum(gw, axis=0, keepdims=True) + 1e-20) * ROUTED_SCALE
    e_ref[...] = jnp.concatenate(top_idx, axis=0)
    g_ref[...] = gw


def _router(h, router_w, router_bias, *, tm=256):
    n = h.shape[0]
    n_exp = router_w.shape[1]
    tm = min(tm, n)
    return pl.pallas_call(
        functools.partial(_router_body, n_exp=n_exp, tm=tm),
        out_shape=(jax.ShapeDtypeStruct((TOP_K, n), jnp.int32), jax.ShapeDtypeStruct((TOP_K, n), F32)),
        grid=(n // tm,),
        in_specs=[pl.BlockSpec((tm, D_MODEL), lambda i: (i, 0)),
                  pl.BlockSpec((n_exp, D_MODEL), lambda i: (0, 0)),
                  pl.BlockSpec((n_exp, 1), lambda i: (0, 0))],
        out_specs=(pl.BlockSpec((TOP_K, tm), lambda i: (0, i)), pl.BlockSpec((TOP_K, tm), lambda i: (0, i))),
        compiler_params=_params(1),
        name="moe_router",
    )(h, router_w.T.astype(F32), router_bias.reshape(n_exp, 1).astype(F32))


def _expert_body(be_ref, nu_ref, tok_ref, tok_next_ref, dst_ref, h_ref, wg_ref, wu_ref, wd_ref, y_ref,
                 x_buf, y_buf, wgb_ref, wub_ref, wdb_ref, gather_sem, scatter_sem, *, tb, n_blocks):
    i = pl.program_id(0)
    n_used = nu_ref[0]
    slot = i % 2

    def start_gather(rows_ref, buf):
        def issue(r, carry):
            src = pl.multiple_of(rows_ref[0, 0, r] * SLAB, SLAB)
            pltpu.make_async_copy(h_ref.at[pl.ds(src, SLAB)], x_buf.at[buf, pl.ds(r * SLAB, SLAB)],
                                  gather_sem.at[buf]).start()
            return carry
        lax.fori_loop(0, tb, issue, 0, unroll=8)

    def wait_buffer(buf_ref, sem_ref, buf):
        pltpu.make_async_copy(buf_ref.at[buf], buf_ref.at[buf], sem_ref.at[buf]).wait()

    @pl.when(jnp.logical_and(i == 0, n_used > 0))
    def _():
        start_gather(tok_ref, 0)

    @pl.when(i + 1 < n_used)
    def _():
        start_gather(tok_next_ref, 1 - slot)

    @pl.when(i < n_used)
    def _():
        fresh = jnp.logical_or(i == 0, be_ref[i] != be_ref[jnp.maximum(i - 1, 0)])

        @pl.when(fresh)
        def _():
            wgb_ref[...] = wg_ref[...].astype(BF16)
            wub_ref[...] = wu_ref[...].astype(BF16)
            wdb_ref[...] = wd_ref[...].astype(BF16)

        wait_buffer(x_buf, gather_sem, slot)

        @pl.when(i >= 2)
        def _():
            wait_buffer(y_buf, scatter_sem, slot)

        xb = _load_slabs(x_buf.at[slot], tb).astype(BF16)
        gate = jnp.dot(xb, wgb_ref[...], preferred_element_type=F32)
        up = jnp.dot(xb, wub_ref[...], preferred_element_type=F32)
        act = (_silu(gate) * up).astype(BF16)
        _store_slabs(y_buf.at[slot], jnp.dot(act, wdb_ref[...], preferred_element_type=F32), tb)

        def issue(r, carry):
            dst = pl.multiple_of(dst_ref[0, 0, r] * SLAB, SLAB)
            pltpu.make_async_copy(y_buf.at[slot, pl.ds(r * SLAB, SLAB)], y_ref.at[pl.ds(dst, SLAB)],
                                  scatter_sem.at[slot]).start()
            return carry
        lax.fori_loop(0, tb, issue, 0, unroll=8)

    @pl.when(i == n_blocks - 1)
    def _():
        @pl.when(n_used >= 1)
        def _():
            wait_buffer(y_buf, scatter_sem, (n_used - 1) % 2)

        @pl.when(n_used >= 2)
        def _():
            wait_buffer(y_buf, scatter_sem, n_used % 2)


def _experts(h_slab, slot_tok, slot_dst, block_e, n_used, w_gate, w_up, w_down, *, tb):
    n_slots = slot_tok.shape[0]
    n_blocks = n_slots // tb
    ff = w_gate.shape[2]
    idx_shape = (n_blocks, 1, tb)
    grid_spec = pltpu.PrefetchScalarGridSpec(
        num_scalar_prefetch=2,
        grid=(n_blocks,),
        in_specs=[pl.BlockSpec((1, 1, tb), lambda i, be, nu: (i, 0, 0), memory_space=pltpu.SMEM),
                  pl.BlockSpec((1, 1, tb), lambda i, be, nu: (jnp.minimum(i + 1, n_blocks - 1), 0, 0),
                               memory_space=pltpu.SMEM),
                  pl.BlockSpec((1, 1, tb), lambda i, be, nu: (i, 0, 0), memory_space=pltpu.SMEM),
                  pl.BlockSpec(memory_space=pl.ANY),
                  pl.BlockSpec((None, D_MODEL, ff), lambda i, be, nu: (be[i], 0, 0)),
                  pl.BlockSpec((None, D_MODEL, ff), lambda i, be, nu: (be[i], 0, 0)),
                  pl.BlockSpec((None, ff, D_MODEL), lambda i, be, nu: (be[i], 0, 0))],
        out_specs=pl.BlockSpec(memory_space=pl.ANY),
        scratch_shapes=[pltpu.VMEM((2, tb * SLAB, 128), F32), pltpu.VMEM((2, tb * SLAB, 128), F32),
                        pltpu.VMEM((D_MODEL, ff), BF16), pltpu.VMEM((D_MODEL, ff), BF16),
                        pltpu.VMEM((ff, D_MODEL), BF16),
                        pltpu.SemaphoreType.DMA((2,)), pltpu.SemaphoreType.DMA((2,))],
    )
    tok3 = slot_tok.reshape(idx_shape)
    return pl.pallas_call(
        functools.partial(_expert_body, tb=tb, n_blocks=n_blocks),
        out_shape=jax.ShapeDtypeStruct((n_slots * SLAB, 128), F32),
        grid_spec=grid_spec,
        compiler_params=_params(1),
        name="moe_experts",
    )(block_e, n_used, tok3, tok3, slot_dst.reshape(idx_shape), h_slab, w_gate, w_up, w_down)


def _combine_body(h_ref, y_ref, gw_ref, sgu_ref, sd_ref, g_ref, b_ref, o_ref, *, ff, tm):
    h = h_ref[...]
    hb = h.astype(BF16)
    gu = jnp.dot(hb, sgu_ref[...], preferred_element_type=F32)
    act = (_silu(gu[:, :ff]) * gu[:, ff:]).astype(BF16)
    f = jnp.dot(act, sd_ref[...], preferred_element_type=F32)
    gw = gw_ref[...]
    for k in range(TOP_K):
        f = f + _load_slabs(y_ref, tm, first=k * SLAB, stride=TOP_K * SLAB) * gw[:, k:k + 1]
    o_ref[...] = _layer_norm(DN_ALPHA * h + f, g_ref[...], b_ref[...])


def _combine(h, y_tok, gw_t, sh_gu, sh_down, g, b, tok_off, n_rows, *, tm=128):
    ff = sh_down.shape[0]
    tm = min(tm, n_rows)
    off = tok_off // tm
    return pl.pallas_call(
        functools.partial(_combine_body, ff=ff, tm=tm),
        out_shape=jax.ShapeDtypeStruct((n_rows, D_MODEL), F32),
        grid=(n_rows // tm,),
        in_specs=[pl.BlockSpec((tm, D_MODEL), lambda i: (off + i, 0)),
                  pl.BlockSpec((tm * TOP_K * SLAB, 128), lambda i: (off + i, 0)),
                  pl.BlockSpec((tm, TOP_K), lambda i: (off + i, 0)),
                  pl.BlockSpec((D_MODEL, 2 * ff), lambda i: (0, 0)),
                  pl.BlockSpec((ff, D_MODEL), lambda i: (0, 0)),
                  pl.BlockSpec((1, D_MODEL), lambda i: (0, 0)),
                  pl.BlockSpec((1, D_MODEL), lambda i: (0, 0))],
        out_specs=pl.BlockSpec((tm, D_MODEL), lambda i: (i, 0)),
        compiler_params=_params(1),
        name="moe_combine_ln2",
    )(h, y_tok, gw_t, sh_gu, sh_down, g.reshape(1, D_MODEL).astype(F32), b.reshape(1, D_MODEL).astype(F32))


def _routing_tables(top_e, n_exp, tb):
    k, n = top_e.shape
    a = k * n
    n_blocks = a // tb + n_exp
    a_bits = max(a - 1, 1).bit_length()
    assert n_exp << a_bits < 2 ** 31
    ids = jnp.arange(a, dtype=jnp.int32)
    keys = lax.sort(top_e.reshape(-1) * (1 << a_bits) + ids)
    e_sorted = keys >> a_bits
    a_sorted = keys & ((1 << a_bits) - 1)
    tok_sorted = a_sorted % n
    k_sorted = a_sorted // n
    experts = jnp.arange(n_exp, dtype=jnp.int32)
    start = jnp.sum((e_sorted[:, None] < experts[None, :]).astype(jnp.int32), axis=0)
    counts = jnp.concatenate([start[1:], jnp.full((1,), a, jnp.int32)]) - start
    padded = (counts + tb - 1) // tb * tb
    pad_end = jnp.cumsum(padded)
    shift = pad_end - padded - start
    dest = ids + jnp.sum(jnp.where(e_sorted[:, None] == experts[None, :], shift[None, :], 0), axis=1)
    slot_row = jnp.full((n_blocks * tb,), -1, jnp.int32).at[dest].set(tok_sorted * k + k_sorted)
    is_pad = slot_row < 0
    pad_rank = jnp.cumsum(is_pad.astype(jnp.int32)) - 1
    slot_tok = jnp.where(is_pad, 0, slot_row // k)
    slot_dst = jnp.where(is_pad, a + pad_rank, slot_row)
    block_start = jnp.arange(n_blocks, dtype=jnp.int32) * tb
    block_e = jnp.minimum(jnp.sum(block_start[:, None] >= pad_end[None, :], axis=1), n_exp - 1).astype(jnp.int32)
    n_used = (pad_end[-1] // tb).astype(jnp.int32).reshape(1)
    block_e = jnp.where(block_start < pad_end[-1], block_e, block_e[jnp.maximum(n_used[0] - 1, 0)])
    return slot_tok, slot_dst, block_e, n_used


def _layer(xs, layer_idx, w_in, conv_w, a_log, dt_bias, norm_w, lam, subln_w, w_branch_gdn, w_branch_diff,
           w_out, ln1_g, ln1_b, router_w, router_bias, w_gate, w_up, w_down, sh_gate, sh_up, sh_down,
           ln2_g, ln2_b, *, moe_tb=256):
    seqs = [(x.shape[0], x.shape[1]) for x in xs]
    x = jnp.concatenate([x.reshape(-1, D_MODEL) for x in xs], axis=0)
    n = x.shape[0]
    seq_starts, seq_ends, group_off = [], [], []
    off = 0
    for b, t in seqs:
        group_off.append(off)
        for _ in range(b):
            seq_starts.append(off)
            off += t
            seq_ends.append(off)
    seq_starts, seq_ends = tuple(seq_starts), tuple(seq_ends)

    o_qkv, o_z, o_ab = GDN_CONV_CH, GDN_CONV_CH + GDN_VAL_W, GDN_CONV_CH + GDN_VAL_W + GDN_GATE_W
    w_main = jnp.concatenate([w_in[:, :o_z], w_in[:, o_ab:]], axis=1).astype(BF16)
    w_ab = w_in[:, o_z:o_ab].astype(BF16)

    p = _proj_in(x, w_main)
    gates = _gdn_gates(x, w_ab, a_log, dt_bias)
    g_t = gates.T.reshape(GDN_GATE_W, 1, n)
    act = _gdn_conv(p, conv_w.astype(F32), seq_starts, seq_ends)
    o_fwd = _gdn_scan(act, g_t, p, None, norm_w, seq_starts, seq_ends, reverse=False)
    o_gdn = _gdn_scan(act, g_t, p, o_fwd, norm_w, seq_starts, seq_ends, reverse=True)
    o_diff = jnp.concatenate(
        [_diff_attention(p, lam, subln_w, goff, b, t, layer_idx) for (b, t), goff in zip(seqs, group_off)], axis=0)
    merged = _merge(o_gdn, o_diff, w_branch_gdn.astype(BF16), w_branch_diff.astype(BF16), p)
    h, h_slab = _out_ln(merged, w_out.astype(BF16), x, ln1_g, ln1_b)

    n_exp = router_w.shape[1]
    top_e, gw = _router(h, router_w, router_bias)
    slot_tok, slot_dst, block_e, n_used = _routing_tables(top_e, n_exp, moe_tb)
    y_tok = _experts(h_slab, slot_tok, slot_dst, block_e, n_used, w_gate, w_up, w_down, tb=moe_tb)
    sh_gu = jnp.concatenate([sh_gate, sh_up], axis=1).astype(BF16)
    outs = []
    for (b, t), goff in zip(seqs, group_off):
        y = _combine(h, y_tok, gw.T, sh_gu, sh_down.astype(BF16), ln2_g, ln2_b, goff, b * t)
        outs.append(y.reshape(b, t, D_MODEL))
    return outs


def kernel(x_prompt, x_sample, w_in, gdn_conv_w, gdn_a_log, gdn_dt_bias, gdn_norm_w, diff_lambda, diff_subln_w, w_branch_gdn, w_branch_diff, w_out, ln1_g, ln1_b, router_w, router_bias, exp_w_gate, exp_w_up, exp_w_down, sh_w_gate, sh_w_up, sh_w_down, ln2_g, ln2_b):
    xs = [x_prompt, x_sample]
    for l in range(DEPTH):
        xs = _layer(xs, l, w_in[l], gdn_conv_w[l], gdn_a_log[l], gdn_dt_bias[l], gdn_norm_w[l], diff_lambda[l],
                    diff_subln_w[l], w_branch_gdn[l], w_branch_diff[l], w_out[l], ln1_g[l], ln1_b[l], router_w[l],
                    router_bias[l], exp_w_gate[l], exp_w_up[l], exp_w_down[l], sh_w_gate[l], sh_w_up[l],
                    sh_w_down[l], ln2_g[l], ln2_b[l])
    return (xs[0], xs[1])
```

```python
import functools
import math

import jax
import jax.numpy as jnp
from jax import lax
from jax.experimental import pallas as pl
from jax.experimental.pallas import tpu as pltpu

F32 = jnp.float32
BF16 = jnp.bfloat16
HIGHEST = lax.Precision.HIGHEST

D_MODEL = 2048
GDN_QK_HEADS = 16
GDN_V_HEADS = 32
GDN_DK = 128
GDN_DV = 128
GDN_KEY_W = GDN_QK_HEADS * GDN_DK
GDN_VAL_W = GDN_V_HEADS * GDN_DV
GDN_CONV_CH = 2 * GDN_KEY_W + GDN_VAL_W
GDN_GATE_W = 4 * GDN_V_HEADS
GDN_CONV_K = 5
GDN_CHUNK = 64
DIFF_HEADS = 8
DIFF_HD = 128
DIFF_QK_W = 2 * DIFF_HEADS * DIFF_HD
DIFF_VAL_W = DIFF_HEADS * 2 * DIFF_HD
N_GROUPS = 8
TOPK_GROUPS = 4
TOP_K = 8
ROUTED_SCALE = 2.5
DEPTH = 1
DN_ALPHA = (2 * DEPTH) ** 0.25
LN_EPS = 1e-5
RMS_EPS = 1e-6
NEG_BIG = -1e30
LOG2E = math.log2(math.e)

P_QKV = 0
P_Z = P_QKV + GDN_CONV_CH
P_DQ = P_Z + GDN_VAL_W
P_DK = P_DQ + DIFF_QK_W
P_DV = P_DK + DIFF_QK_W
P_GATES = P_DV + DIFF_VAL_W
P_COLS = P_GATES + 2 * D_MODEL

VMEM_LIMIT = 56 * 1024 * 1024
SLAB = D_MODEL // 128


def _params(grid_rank, vmem=VMEM_LIMIT):
    return pltpu.CompilerParams(dimension_semantics=("arbitrary",) * grid_rank, vmem_limit_bytes=vmem)


def _sigmoid(x):
    return 1.0 / (1.0 + jnp.exp(-x))


def _silu(x):
    return x * _sigmoid(x)


def _layer_norm(x, g, b):
    mu = jnp.mean(x, axis=-1, keepdims=True)
    xc = x - mu
    var = jnp.mean(xc * xc, axis=-1, keepdims=True)
    return xc * lax.rsqrt(var + LN_EPS) * g + b


def _proj_in_body(x_ref, w_ref, o_ref, xb_ref, *, tn):
    j = pl.program_id(1)

    @pl.when(j == 0)
    def _():
        xb_ref[...] = x_ref[...].astype(BF16)

    acc = jnp.dot(xb_ref[...], w_ref[...], preferred_element_type=F32)
    col = j * tn
    is_gate = col >= P_GATES
    is_dq = jnp.logical_and(col >= P_DQ, col < P_DK)
    scale = jnp.where(is_dq, LOG2E * DIFF_HD ** -0.5, 1.0).astype(F32)

    @pl.when(is_gate)
    def _():
        o_ref[...] = _sigmoid(acc).astype(o_ref.dtype)

    @pl.when(jnp.logical_not(is_gate))
    def _():
        o_ref[...] = (acc * scale).astype(o_ref.dtype)


def _proj_in(x, w, *, tm=1024, tn=512):
    n, k = x.shape
    m = w.shape[1]
    tm = min(tm, n)
    return pl.pallas_call(
        functools.partial(_proj_in_body, tn=tn),
        out_shape=jax.ShapeDtypeStruct((n, m), BF16),
        grid=(n // tm, m // tn),
        in_specs=[pl.BlockSpec((tm, k), lambda i, j: (i, 0)),
                  pl.BlockSpec((k, tn), lambda i, j: (0, j))],
        out_specs=pl.BlockSpec((tm, tn), lambda i, j: (i, j)),
        scratch_shapes=[pltpu.VMEM((tm, k), BF16)],
        compiler_params=_params(2),
        name="proj_in",
    )(x, w)


def _gates_body(x_ref, w_ref, prm_ref, o_ref, *, tt):
    ab = jnp.dot(x_ref[...].astype(BF16), w_ref[...], preferred_element_type=F32)
    a_log = prm_ref[0:1, :]
    dt_bias = prm_ref[1:2, :]
    y = ab + dt_bias
    softplus = jnp.maximum(y, 0.0) + jnp.log(1.0 + jnp.exp(-jnp.abs(y)))
    log_decay = -jnp.exp(a_log) * softplus
    beta = _sigmoid(ab)
    ri = lax.broadcasted_iota(jnp.int32, (tt, tt), 0)
    ci = lax.broadcasted_iota(jnp.int32, (tt, tt), 1)
    same = (ri // GDN_CHUNK) == (ci // GDN_CHUNK)
    m_lo = jnp.where(same, jnp.where(ci <= ri, 1.0, 0.0), 0.0).astype(F32)
    m_up = jnp.where(same, jnp.where(ci >= ri, 1.0, 0.0), 0.0).astype(F32)
    c_lo = jnp.dot(m_lo, log_decay, precision=HIGHEST, preferred_element_type=F32)
    c_up = jnp.dot(m_up, log_decay, precision=HIGHEST, preferred_element_type=F32)
    lane = lax.broadcasted_iota(jnp.int32, (tt, GDN_GATE_W), 1)
    is_a = (lane % (2 * GDN_V_HEADS)) < GDN_V_HEADS
    is_bwd = lane >= 2 * GDN_V_HEADS
    o_ref[...] = jnp.where(is_a, jnp.where(is_bwd, c_up, c_lo), beta)


def _gdn_gates(x, w_ab, a_log, dt_bias, *, tt=512):
    n, k = x.shape
    tt = min(tt, n)
    zeros = jnp.zeros((GDN_V_HEADS,), F32)
    prm = jnp.zeros((8, GDN_GATE_W), F32)
    prm = prm.at[0].set(jnp.concatenate([a_log[0], zeros, a_log[1], zeros]).astype(F32))
    prm = prm.at[1].set(jnp.concatenate([dt_bias[0], zeros, dt_bias[1], zeros]).astype(F32))
    return pl.pallas_call(
        functools.partial(_gates_body, tt=tt),
        out_shape=jax.ShapeDtypeStruct((n, GDN_GATE_W), F32),
        grid=(n // tt,),
        in_specs=[pl.BlockSpec((tt, k), lambda i: (i, 0)),
                  pl.BlockSpec((k, GDN_GATE_W), lambda i: (0, 0)),
                  pl.BlockSpec((8, GDN_GATE_W), lambda i: (0, 0))],
        out_specs=pl.BlockSpec((tt, GDN_GATE_W), lambda i: (i, 0)),
        compiler_params=_params(1),
        name="gdn_gates",
    )(x, w_ab, prm)


CONV_HALO = 16


def _any_equal(value, constants):
    hit = value == constants[0]
    for c in constants[1:]:
        hit = jnp.logical_or(hit, value == c)
    return hit


def _conv_body(prev_ref, cur_ref, next_ref, w_ref, o_ref, ext_ref, *, tt, tc, seq_starts, seq_ends):
    i = pl.program_id(0)
    j = pl.program_id(1)
    t0 = i * tt
    at_start = _any_equal(t0, seq_starts)
    at_end = _any_equal(t0 + tt, seq_ends)
    prev = prev_ref[...].astype(F32)[CONV_HALO - 8:, :]
    nxt = next_ref[...].astype(F32)[:8, :]
    ext_ref[0:8, :] = jnp.where(at_start, 0.0, prev)
    ext_ref[8:8 + tt, :] = cur_ref[...].astype(F32)
    ext_ref[8 + tt:16 + tt, :] = jnp.where(at_end, 0.0, nxt)
    pad = (GDN_CONV_K - 1) // 2
    acc = jnp.zeros((tt, tc), F32)
    for tap in range(GDN_CONV_K):
        acc = acc + ext_ref[pl.ds(8 - pad + tap, tt), :] * w_ref[tap:tap + 1, :]
    act = _silu(acc)
    col = j * tc
    is_qk = col < 2 * GDN_KEY_W
    q_scale = jnp.where(col < GDN_KEY_W, GDN_DK ** -0.5, 1.0).astype(F32)
    for s in range(tc // GDN_DK):
        a = act[:, s * GDN_DK:(s + 1) * GDN_DK]
        ss = jnp.sum(a * a, axis=-1, keepdims=True)
        normed = a * (lax.rsqrt(ss + RMS_EPS) * q_scale)
        o_ref[:, s * GDN_DK:(s + 1) * GDN_DK] = jnp.where(is_qk, normed, a).astype(o_ref.dtype)


def _gdn_conv(p, conv_w, seq_starts, seq_ends, *, tt=512, tc=512):
    n = p.shape[0]
    tt = min(tt, n)
    hb = tt // CONV_HALO
    n_halo = n // CONV_HALO
    return pl.pallas_call(
        functools.partial(_conv_body, tt=tt, tc=tc, seq_starts=seq_starts, seq_ends=seq_ends),
        out_shape=jax.ShapeDtypeStruct((n, GDN_CONV_CH), BF16),
        grid=(n // tt, GDN_CONV_CH // tc),
        in_specs=[pl.BlockSpec((CONV_HALO, tc), lambda i, j: (jnp.maximum(i * hb - 1, 0), j)),
                  pl.BlockSpec((tt, tc), lambda i, j: (i, j)),
                  pl.BlockSpec((CONV_HALO, tc), lambda i, j: (jnp.minimum((i + 1) * hb, n_halo - 1), j)),
                  pl.BlockSpec((GDN_CONV_K, tc), lambda i, j: (0, j))],
        out_specs=pl.BlockSpec((tt, tc), lambda i, j: (i, j)),
        scratch_shapes=[pltpu.VMEM((tt + 16, tc), F32)],
        compiler_params=_params(2),
        name="gdn_conv",
    )(p, p, p, conv_w)


def _dot_f32(a, b):
    return jnp.dot(a, b, precision=HIGHEST, preferred_element_type=F32)


def _dot_bf16(a, b):
    return jnp.dot(a.astype(BF16), b.astype(BF16), preferred_element_type=F32)


def _gdn_scan_body(*refs, tt, nt, hp, reverse, seq_starts, seq_ends):
    if reverse:
        q_ref, k_ref, v_ref, g_ref, b_ref, ofwd_ref, z_ref, nw_ref, o_ref, s_ref = refs
    else:
        q_ref, k_ref, v_ref, g_ref, b_ref, o_ref, s_ref = refs
    C = GDN_CHUNK
    t = pl.program_id(1)
    tok0 = ((nt - 1 - t) if reverse else t) * tt
    reset = _any_equal(tok0 + tt, seq_ends) if reverse else _any_equal(tok0, seq_starts)

    @pl.when(reset)
    def _():
        s_ref[...] = jnp.zeros_like(s_ref)

    n_chunks = tt // C
    ri = lax.broadcasted_iota(jnp.int32, (tt, tt), 0)
    ci = lax.broadcasted_iota(jnp.int32, (tt, tt), 1)
    same = (ri // C) == (ci // C)
    eye = ri == ci
    causal = jnp.logical_and(same, (ci >= ri) if reverse else (ci <= ri))
    strict = jnp.logical_and(same, (ci > ri) if reverse else (ci < ri))
    eye_f = jnp.where(eye, 1.0, 0.0).astype(F32)
    last = 0 if reverse else C - 1
    order = range(n_chunks - 1, -1, -1) if reverse else range(n_chunks)
    chunk_of_lane = lax.broadcasted_iota(jnp.int32, (1, tt), 1) // C
    chunk_of_lane_k = lax.broadcasted_iota(jnp.int32, (GDN_DK, tt), 1) // C

    shared = []
    for pp in range(hp):
        q = q_ref[:, pp * GDN_DK:(pp + 1) * GDN_DK]
        k = k_ref[:, pp * GDN_DK:(pp + 1) * GDN_DK]
        kf = k.astype(F32)
        kt = kf.T
        ktb = kt.astype(BF16)
        kk = jnp.dot(k, ktb, preferred_element_type=F32)
        qk = jnp.dot(q, ktb, preferred_element_type=F32)
        shared.append((kf, q.astype(F32), kt, kk, qk))
    for hh in range(2 * hp):
        kf, qf, kt, kk, qk = shared[hh // 2]
        cols = slice(hh * GDN_DV, (hh + 1) * GDN_DV)
        g_row = g_ref[hh]
        b_row = b_ref[hh]
        g_col = jnp.sum(jnp.where(eye, g_row, 0.0), axis=1, keepdims=True)
        b_col = jnp.sum(jnp.where(eye, b_row, 0.0), axis=1, keepdims=True)
        g_last = [g_row[:, c * C + last:c * C + last + 1] for c in range(n_chunks)]
        g_last_row = jnp.zeros((1, tt), F32)
        for c in range(n_chunks):
            g_last_row = jnp.where(chunk_of_lane == c, g_last[c], g_last_row)
        decay = jnp.exp(jnp.where(causal, g_col - g_row, NEG_BIG))
        low = jnp.where(strict, kk * decay, 0.0) * b_col
        low_b = low.astype(BF16)
        pw = jnp.dot(low_b, low_b, preferred_element_type=F32)
        inv = eye_f - low
        n_factors = int(math.log2(C)) - 1
        for j in range(n_factors):
            pw_b = pw.astype(BF16)
            if j < n_factors - 1:
                both = jnp.dot(jnp.concatenate([inv, pw], axis=0).astype(BF16), pw_b, preferred_element_type=F32)
                inv = inv + both[:tt]
                pw = both[tt:]
            else:
                inv = inv + jnp.dot(inv.astype(BF16), pw_b, preferred_element_type=F32)
        eg_col = jnp.exp(g_col)
        v = v_ref[:, cols].astype(F32)
        x = jnp.concatenate([v * b_col, kf * (b_col * eg_col)], axis=1)
        uw = jnp.dot(inv.astype(BF16), x.astype(BF16), preferred_element_type=F32).astype(BF16)
        from_q = jnp.dot((qk * decay).astype(BF16), uw, preferred_element_type=F32)
        ktd = kt * jnp.exp(g_last_row - g_row)
        lhs_k = jnp.concatenate([jnp.where(chunk_of_lane_k == c, ktd, 0.0) for c in range(n_chunks)],
                                axis=0).astype(BF16)
        from_k = jnp.dot(lhs_k, uw, preferred_element_type=F32)
        q_eff = qf * eg_col - from_q[:, GDN_DV:]
        for c in order:
            rows = slice(c * C, (c + 1) * C)
            krows = slice(c * GDN_DK, (c + 1) * GDN_DK)
            lhs = jnp.concatenate([q_eff[rows], from_k[krows, GDN_DV:]], axis=0).astype(BF16)
            state = s_ref[hh]
            res = jnp.dot(lhs, state.astype(BF16), preferred_element_type=F32)
            o = res[:C] + from_q[rows, :GDN_DV]
            s_ref[hh] = state * jnp.exp(g_last[c]) - res[C:] + from_k[krows, :GDN_DV]
            if reverse:
                tot = o + ofwd_ref[rows, cols]
                ms = jnp.mean(tot * tot, axis=-1, keepdims=True)
                z = z_ref[rows, cols].astype(F32)
                o_ref[rows, cols] = (tot * lax.rsqrt(ms + RMS_EPS) * nw_ref[...] * _silu(z)).astype(o_ref.dtype)
            else:
                o_ref[rows, cols] = o


def _gdn_scan(act, g_t, p, o_fwd, norm_w, seq_starts, seq_ends, *, reverse, tt=256, hp=2):
    n = act.shape[0]
    tt = min(tt, n)
    nt = n // tt
    groups = GDN_QK_HEADS // hp
    kw = hp * GDN_DK
    vw = 2 * hp * GDN_DV

    def tmap(t):
        return (nt - 1 - t) if reverse else t

    gate_blocks = (2 * GDN_V_HEADS) // (2 * hp)
    dirn = 1 if reverse else 0
    in_specs = [
        pl.BlockSpec((tt, kw), lambda h, t: (tmap(t), h)),
        pl.BlockSpec((tt, kw), lambda h, t: (tmap(t), groups + h)),
        pl.BlockSpec((tt, vw), lambda h, t: (tmap(t), 2 * GDN_KEY_W // vw + h)),
        pl.BlockSpec((2 * hp, 1, tt), lambda h, t: (dirn * gate_blocks + h, 0, tmap(t))),
        pl.BlockSpec((2 * hp, 1, tt), lambda h, t: (dirn * gate_blocks + groups + h, 0, tmap(t))),
    ]
    args = [act, act, act, g_t, g_t]
    if reverse:
        in_specs += [
            pl.BlockSpec((tt, vw), lambda h, t: (tmap(t), h)),
            pl.BlockSpec((tt, vw), lambda h, t: (tmap(t), P_Z // vw + h)),
            pl.BlockSpec((1, GDN_DV), lambda h, t: (0, 0)),
        ]
        args += [o_fwd, p, norm_w.reshape(1, GDN_DV).astype(F32)]
        out_dtype = BF16
    else:
        out_dtype = F32
    return pl.pallas_call(
        functools.partial(_gdn_scan_body, tt=tt, nt=nt, hp=hp, reverse=reverse,
                          seq_starts=seq_starts, seq_ends=seq_ends),
        out_shape=jax.ShapeDtypeStruct((n, GDN_VAL_W), out_dtype),
        grid=(groups, nt),
        in_specs=in_specs,
        out_specs=pl.BlockSpec((tt, vw), lambda h, t: (tmap(t), h)),
        scratch_shapes=[pltpu.VMEM((2 * hp, GDN_DK, GDN_DV), F32)],
        compiler_params=_params(2),
        name="gdn_scan_bwd" if reverse else "gdn_scan_fwd",
    )(*args)


def _attn_body(slope_ref, lam_ref, sw_ref, q_ref, k_ref, v_ref, o_ref, m_ref, l_ref, acc_ref, sd_ref,
               *, tile, nk, lam_init):
    h = pl.program_id(1)
    qi = pl.program_id(2)
    kj = pl.program_id(3)
    slope = slope_ref[h]

    n_sub = m_ref.shape[1]
    sub = tile // n_sub

    @pl.when(kj == 0)
    def _():
        m_ref[...] = jnp.full_like(m_ref, NEG_BIG)
        l_ref[...] = jnp.zeros_like(l_ref)
        acc_ref[...] = jnp.zeros_like(acc_ref)
        for r in range(n_sub):
            ii = lax.broadcasted_iota(jnp.int32, (sub, tile), 0) + r * sub
            jj = lax.broadcasted_iota(jnp.int32, (sub, tile), 1)
            sd_ref[r] = (ii - jj).astype(F32) * slope

    v = v_ref[...]
    tile_dist = jnp.full((sub, 1), (qi - kj) * tile, jnp.int32).astype(F32) * slope

    def update(add_bias, shift):
        for r in range(n_sub):
            rows = slice(r * sub, (r + 1) * sub)
            for m in range(2):
                cols = slice(m * DIFF_HD, (m + 1) * DIFF_HD)
                t = add_bias(lax.dot_general(q_ref[rows, cols], k_ref[:, cols], (((1,), (1,)), ((), ())),
                                             preferred_element_type=F32), r)
                m_old = m_ref[m, r]
                m_new = jnp.maximum(m_old, jnp.max(t, axis=1, keepdims=True) - shift)
                alpha = jnp.exp2(m_old - m_new)
                p = jnp.exp2(t - (m_new + shift))
                l_ref[m, r] = alpha * l_ref[m, r] + jnp.sum(p, axis=1, keepdims=True)
                acc_ref[m, r] = alpha * acc_ref[m, r] + jnp.dot(p.astype(BF16), v, preferred_element_type=F32)
                m_ref[m, r] = m_new

    @pl.when(kj < qi)
    def _():
        update(lambda s, r: s - sd_ref[r], tile_dist)

    @pl.when(kj > qi)
    def _():
        update(lambda s, r: s + sd_ref[r], -tile_dist)

    @pl.when(kj == qi)
    def _():
        update(lambda s, r: s - jnp.abs(sd_ref[r]), jnp.zeros((sub, 1), F32))

    @pl.when(kj == nk - 1)
    def _():
        lam = lam_ref[...]
        lam_full = (jnp.exp(jnp.sum(lam[0:1, :] * lam[1:2, :], axis=-1, keepdims=True))
                    - jnp.exp(jnp.sum(lam[2:3, :] * lam[3:4, :], axis=-1, keepdims=True)) + lam_init)
        for r in range(n_sub):
            o = acc_ref[0, r] / l_ref[0, r] - lam_full * (acc_ref[1, r] / l_ref[1, r])
            ms = jnp.mean(o * o, axis=-1, keepdims=True)
            o_ref[r * sub:(r + 1) * sub, :] = (o * lax.rsqrt(ms + LN_EPS) * sw_ref[...]
                                               * (1.0 - lam_init)).astype(o_ref.dtype)


def _diff_attention(p, lam, subln_w, tok_off, n_seq, seq_len, layer_idx, *, tile=1024, sub=512):
    tile = min(tile, seq_len)
    sub = min(sub, tile)
    n_sub = tile // sub
    nt = seq_len // tile
    lam_init = 0.8 - 0.6 * math.exp(-0.3 * layer_idx)
    slopes = jnp.asarray([LOG2E * 2.0 ** (-8.0 * (i + 1) / DIFF_HEADS) for i in range(DIFF_HEADS)], F32)
    w2 = 2 * DIFF_HD
    off = tok_off // tile
    return pl.pallas_call(
        functools.partial(_attn_body, tile=tile, nk=nt, lam_init=lam_init),
        out_shape=jax.ShapeDtypeStruct((n_seq * seq_len, DIFF_VAL_W), BF16),
        grid=(n_seq, DIFF_HEADS, nt, nt),
        in_specs=[pl.BlockSpec(memory_space=pltpu.SMEM),
                  pl.BlockSpec((4, DIFF_HD), lambda b, h, i, j: (0, 0)),
                  pl.BlockSpec((1, w2), lambda b, h, i, j: (0, 0)),
                  pl.BlockSpec((tile, w2), lambda b, h, i, j: (off + b * nt + i, P_DQ // w2 + h)),
                  pl.BlockSpec((tile, w2), lambda b, h, i, j: (off + b * nt + j, P_DK // w2 + h)),
                  pl.BlockSpec((tile, w2), lambda b, h, i, j: (off + b * nt + j, P_DV // w2 + h))],
        out_specs=pl.BlockSpec((tile, w2), lambda b, h, i, j: (b * nt + i, h)),
        scratch_shapes=[pltpu.VMEM((2, n_sub, sub, 1), F32), pltpu.VMEM((2, n_sub, sub, 1), F32),
                        pltpu.VMEM((2, n_sub, sub, w2), F32), pltpu.VMEM((n_sub, sub, tile), F32)],
        compiler_params=_params(4),
        name="diff_attention",
    )(slopes, lam.astype(F32), subln_w.reshape(1, w2).astype(F32), p, p, p)


def _merge_body(og_ref, od_ref, wg_ref, wd_ref, gg_ref, gd_ref, o_ref):
    a = jnp.dot(og_ref[...], wg_ref[...], preferred_element_type=F32)
    b = jnp.dot(od_ref[...], wd_ref[...], preferred_element_type=F32)
    o_ref[...] = (gg_ref[...].astype(F32) * a + gd_ref[...].astype(F32) * b).astype(o_ref.dtype)


def _merge(o_gdn, o_diff, w_g, w_d, p, *, tm=512, tn=512):
    n = o_gdn.shape[0]
    tm = min(tm, n)
    return pl.pallas_call(
        _merge_body,
        out_shape=jax.ShapeDtypeStruct((n, D_MODEL), BF16),
        grid=(n // tm, D_MODEL // tn),
        in_specs=[pl.BlockSpec((tm, GDN_VAL_W), lambda i, j: (i, 0)),
                  pl.BlockSpec((tm, DIFF_VAL_W), lambda i, j: (i, 0)),
                  pl.BlockSpec((GDN_VAL_W, tn), lambda i, j: (0, j)),
                  pl.BlockSpec((DIFF_VAL_W, tn), lambda i, j: (0, j)),
                  pl.BlockSpec((tm, tn), lambda i, j: (i, P_GATES // tn + j)),
                  pl.BlockSpec((tm, tn), lambda i, j: (i, (P_GATES + D_MODEL) // tn + j))],
        out_specs=pl.BlockSpec((tm, tn), lambda i, j: (i, j)),
        compiler_params=_params(2),
        name="branch_merge",
    )(o_gdn, o_diff, w_g, w_d, p, p)


def _store_slabs(slab_ref, value, n_rows):
    for s in range(SLAB):
        slab_ref[pl.ds(s, n_rows, stride=SLAB), :] = value[:, s * 128:(s + 1) * 128]


def _load_slabs(slab_ref, n_rows, first=0, stride=SLAB):
    return jnp.concatenate([slab_ref[pl.ds(first + s, n_rows, stride=stride), :] for s in range(SLAB)], axis=1)


def _out_ln_body(m_ref, w_ref, x_ref, g_ref, b_ref, o_ref, slab_ref, *, tm):
    mix = jnp.dot(m_ref[...], w_ref[...], preferred_element_type=F32)
    h = _layer_norm(DN_ALPHA * x_ref[...] + mix, g_ref[...], b_ref[...])
    o_ref[...] = h
    _store_slabs(slab_ref, h, tm)


def _out_ln(merged, w_out, x, g, b, *, tm=512):
    n = x.shape[0]
    tm = min(tm, n)
    return pl.pallas_call(
        functools.partial(_out_ln_body, tm=tm),
        out_shape=(jax.ShapeDtypeStruct((n, D_MODEL), F32), jax.ShapeDtypeStruct((n * SLAB, 128), F32)),
        grid=(n // tm,),
        in_specs=[pl.BlockSpec((tm, D_MODEL), lambda i: (i, 0)),
                  pl.BlockSpec((D_MODEL, D_MODEL), lambda i: (0, 0)),
                  pl.BlockSpec((tm, D_MODEL), lambda i: (i, 0)),
                  pl.BlockSpec((1, D_MODEL), lambda i: (0, 0)),
                  pl.BlockSpec((1, D_MODEL), lambda i: (0, 0))],
        out_specs=(pl.BlockSpec((tm, D_MODEL), lambda i: (i, 0)), pl.BlockSpec((tm * SLAB, 128), lambda i: (i, 0))),
        compiler_params=_params(1),
        name="out_proj_ln1",
    )(merged, w_out, x, g.reshape(1, D_MODEL).astype(F32), b.reshape(1, D_MODEL).astype(F32))


def _first_argmax(x, row_ids, n_rows):
    mx = jnp.max(x, axis=0, keepdims=True)
    idx = jnp.min(jnp.where(x == mx, row_ids, n_rows), axis=0, keepdims=True)
    return mx, idx


def _router_body(h_ref, w_ref, bias_ref, e_ref, g_ref, *, n_exp, tm):
    per = n_exp // N_GROUPS
    logits = lax.dot_general(w_ref[...], h_ref[...], (((1,), (1,)), ((), ())),
                             precision=HIGHEST, preferred_element_type=F32)
    scores = _sigmoid(logits)
    choice = scores + bias_ref[...]
    ids_g = lax.broadcasted_iota(jnp.int32, (per, tm), 0)
    group_rows = []
    for g in range(N_GROUPS):
        xg = choice[g * per:(g + 1) * per, :]
        m1, i1 = _first_argmax(xg, ids_g, per)
        m2 = jnp.max(jnp.where(ids_g == i1, -jnp.inf, xg), axis=0, keepdims=True)
        group_rows.append(m1 + m2)
    gs = jnp.concatenate(group_rows, axis=0)
    ids_n = lax.broadcasted_iota(jnp.int32, (N_GROUPS, tm), 0)
    keep = jnp.zeros((N_GROUPS, tm), F32)
    for _ in range(TOPK_GROUPS):
        _, gi = _first_argmax(gs, ids_n, N_GROUPS)
        hit = ids_n == gi
        keep = jnp.where(hit, 1.0, keep)
        gs = jnp.where(hit, -jnp.inf, gs)
    masked = jnp.concatenate(
        [jnp.where(keep[g:g + 1, :] > 0.5, choice[g * per:(g + 1) * per, :], -jnp.inf)
         for g in range(N_GROUPS)], axis=0)
    ids_e = lax.broadcasted_iota(jnp.int32, (n_exp, tm), 0)
    top_idx = []
    top_w = []
    for _ in range(TOP_K):
        _, ei = _first_argmax(masked, ids_e, n_exp)
        hit = ids_e == ei
        top_idx.append(ei)
        top_w.append(jnp.sum(jnp.where(hit, scores, 0.0), axis=0, keepdims=True))
        masked = jnp.where(hit, -jnp.inf, masked)
    gw = jnp.concatenate(top_w, axis=0)
    gw = gw / (jnp.sum(gw, axis=0, keepdims=True) + 1e-20) * ROUTED_SCALE
    e_ref[...] = jnp.concatenate(top_idx, axis=0)
    g_ref[...] = gw


def _router(h, router_w, router_bias, *, tm=256):
    n = h.shape[0]
    n_exp = router_w.shape[1]
    tm = min(tm, n)
    return pl.pallas_call(
        functools.partial(_router_body, n_exp=n_exp, tm=tm),
        out_shape=(jax.ShapeDtypeStruct((TOP_K, n), jnp.int32), jax.ShapeDtypeStruct((TOP_K, n), F32)),
        grid=(n // tm,),
        in_specs=[pl.BlockSpec((tm, D_MODEL), lambda i: (i, 0)),
                  pl.BlockSpec((n_exp, D_MODEL), lambda i: (0, 0)),
                  pl.BlockSpec((n_exp, 1), lambda i: (0, 0))],
        out_specs=(pl.BlockSpec((TOP_K, tm), lambda i: (0, i)), pl.BlockSpec((TOP_K, tm), lambda i: (0, i))),
        compiler_params=_params(1),
        name="moe_router",
    )(h, router_w.T.astype(F32), router_bias.reshape(n_exp, 1).astype(F32))


def _expert_body(be_ref, nu_ref, tok_ref, tok_next_ref, dst_ref, h_ref, wg_ref, wu_ref, wd_ref, y_ref,
                 x_buf, y_buf, wgb_ref, wub_ref, wdb_ref, gather_sem, scatter_sem, *, tb, n_blocks):
    i = pl.program_id(0)
    n_used = nu_ref[0]
    slot = i % 2

    def start_gather(rows_ref, buf):
        def issue(r, carry):
            src = pl.multiple_of(rows_ref[0, 0, r] * SLAB, SLAB)
            pltpu.make_async_copy(h_ref.at[pl.ds(src, SLAB)], x_buf.at[buf, pl.ds(r * SLAB, SLAB)],
                                  gather_sem.at[buf]).start()
            return carry
        lax.fori_loop(0, tb, issue, 0, unroll=8)

    def wait_buffer(buf_ref, sem_ref, buf):
        pltpu.make_async_copy(buf_ref.at[buf], buf_ref.at[buf], sem_ref.at[buf]).wait()

    @pl.when(jnp.logical_and(i == 0, n_used > 0))
    def _():
        start_gather(tok_ref, 0)

    @pl.when(i + 1 < n_used)
    def _():
        start_gather(tok_next_ref, 1 - slot)

    @pl.when(i < n_used)
    def _():
        fresh = jnp.logical_or(i == 0, be_ref[i] != be_ref[jnp.maximum(i - 1, 0)])

        @pl.when(fresh)
        def _():
            wgb_ref[...] = wg_ref[...].astype(BF16)
            wub_ref[...] = wu_ref[...].astype(BF16)
            wdb_ref[...] = wd_ref[...].astype(BF16)

        wait_buffer(x_buf, gather_sem, slot)

        @pl.when(i >= 2)
        def _():
            wait_buffer(y_buf, scatter_sem, slot)

        xb = _load_slabs(x_buf.at[slot], tb).astype(BF16)
        gate = jnp.dot(xb, wgb_ref[...], preferred_element_type=F32)
        up = jnp.dot(xb, wub_ref[...], preferred_element_type=F32)
        act = (_silu(gate) * up).astype(BF16)
        _store_slabs(y_buf.at[slot], jnp.dot(act, wdb_ref[...], preferred_element_type=F32), tb)

        def issue(r, carry):
            dst = pl.multiple_of(dst_ref[0, 0, r] * SLAB, SLAB)
            pltpu.make_async_copy(y_buf.at[slot, pl.ds(r * SLAB, SLAB)], y_ref.at[pl.ds(dst, SLAB)],
                                  scatter_sem.at[slot]).start()
            return carry
        lax.fori_loop(0, tb, issue, 0, unroll=8)

    @pl.when(i == n_blocks - 1)
    def _():
        @pl.when(n_used >= 1)
        def _():
            wait_buffer(y_buf, scatter_sem, (n_used - 1) % 2)

        @pl.when(n_used >= 2)
        def _():
            wait_buffer(y_buf, scatter_sem, n_used % 2)


def _experts(h_slab, slot_tok, slot_dst, block_e, n_used, w_gate, w_up, w_down, *, tb):
    n_slots = slot_tok.shape[0]
    n_blocks = n_slots // tb
    ff = w_gate.shape[2]
    idx_shape = (n_blocks, 1, tb)
    grid_spec = pltpu.PrefetchScalarGridSpec(
        num_scalar_prefetch=2,
        grid=(n_blocks,),
        in_specs=[pl.BlockSpec((1, 1, tb), lambda i, be, nu: (i, 0, 0), memory_space=pltpu.SMEM),
                  pl.BlockSpec((1, 1, tb), lambda i, be, nu: (jnp.minimum(i + 1, n_blocks - 1), 0, 0),
                               memory_space=pltpu.SMEM),
                  pl.BlockSpec((1, 1, tb), lambda i, be, nu: (i, 0, 0), memory_space=pltpu.SMEM),
                  pl.BlockSpec(memory_space=pl.ANY),
                  pl.BlockSpec((None, D_MODEL, ff), lambda i, be, nu: (be[i], 0, 0)),
                  pl.BlockSpec((None, D_MODEL, ff), lambda i, be, nu: (be[i], 0, 0)),
                  pl.BlockSpec((None, ff, D_MODEL), lambda i, be, nu: (be[i], 0, 0))],
        out_specs=pl.BlockSpec(memory_space=pl.ANY),
        scratch_shapes=[pltpu.VMEM((2, tb * SLAB, 128), F32), pltpu.VMEM((2, tb * SLAB, 128), F32),
                        pltpu.VMEM((D_MODEL, ff), BF16), pltpu.VMEM((D_MODEL, ff), BF16),
                        pltpu.VMEM((ff, D_MODEL), BF16),
                        pltpu.SemaphoreType.DMA((2,)), pltpu.SemaphoreType.DMA((2,))],
    )
    tok3 = slot_tok.reshape(idx_shape)
    return pl.pallas_call(
        functools.partial(_expert_body, tb=tb, n_blocks=n_blocks),
        out_shape=jax.ShapeDtypeStruct((n_slots * SLAB, 128), F32),
        grid_spec=grid_spec,
        compiler_params=_params(1),
        name="moe_experts",
    )(block_e, n_used, tok3, tok3, slot_dst.reshape(idx_shape), h_slab, w_gate, w_up, w_down)


def _combine_body(h_ref, y_ref, gw_ref, sgu_ref, sd_ref, g_ref, b_ref, o_ref, *, ff, tm):
    h = h_ref[...]
    hb = h.astype(BF16)
    gu = jnp.dot(hb, sgu_ref[...], preferred_element_type=F32)
    act = (_silu(gu[:, :ff]) * gu[:, ff:]).astype(BF16)
    f = jnp.dot(act, sd_ref[...], preferred_element_type=F32)
    gw = gw_ref[...]
    for k in range(TOP_K):
        f = f + _load_slabs(y_ref, tm, first=k * SLAB, stride=TOP_K * SLAB) * gw[:, k:k + 1]
    o_ref[...] = _layer_norm(DN_ALPHA * h + f, g_ref[...], b_ref[...])


def _combine(h, y_tok, gw_t, sh_gu, sh_down, g, b, tok_off, n_rows, *, tm=128):
    ff = sh_down.shape[0]
    tm = min(tm, n_rows)
    off = tok_off // tm
    return pl.pallas_call(
        functools.partial(_combine_body, ff=ff, tm=tm),
        out_shape=jax.ShapeDtypeStruct((n_rows, D_MODEL), F32),
        grid=(n_rows // tm,),
        in_specs=[pl.BlockSpec((tm, D_MODEL), lambda i: (off + i, 0)),
                  pl.BlockSpec((tm * TOP_K * SLAB, 128), lambda i: (off + i, 0)),
                  pl.BlockSpec((tm, TOP_K), lambda i: (off + i, 0)),
                  pl.BlockSpec((D_MODEL, 2 * ff), lambda i: (0, 0)),
                  pl.BlockSpec((ff, D_MODEL), lambda i: (0, 0)),
                  pl.BlockSpec((1, D_MODEL), lambda i: (0, 0)),
                  pl.BlockSpec((1, D_MODEL), lambda i: (0, 0))],
        out_specs=pl.BlockSpec((tm, D_MODEL), lambda i: (i, 0)),
        compiler_params=_params(1),
        name="moe_combine_ln2",
    )(h, y_tok, gw_t, sh_gu, sh_down, g.reshape(1, D_MODEL).astype(F32), b.reshape(1, D_MODEL).astype(F32))


def _routing_tables(top_e, n_exp, tb):
    k, n = top_e.shape
    a = k * n
    n_blocks = a // tb + n_exp
    a_bits = max(a - 1, 1).bit_length()
    assert n_exp << a_bits < 2 ** 31
    ids = jnp.arange(a, dtype=jnp.int32)
    keys = lax.sort(top_e.reshape(-1) * (1 << a_bits) + ids)
    e_sorted = keys >> a_bits
    a_sorted = keys & ((1 << a_bits) - 1)
    tok_sorted = a_sorted % n
    k_sorted = a_sorted // n
    experts = jnp.arange(n_exp, dtype=jnp.int32)
    start = jnp.sum((e_sorted[:, None] < experts[None, :]).astype(jnp.int32), axis=0)
    counts = jnp.concatenate([start[1:], jnp.full((1,), a, jnp.int32)]) - start
    padded = (counts + tb - 1) // tb * tb
    pad_end = jnp.cumsum(padded)
    shift = pad_end - padded - start
    dest = ids + jnp.sum(jnp.where(e_sorted[:, None] == experts[None, :], shift[None, :], 0), axis=1)
    slot_row = jnp.full((n_blocks * tb,), -1, jnp.int32).at[dest].set(tok_sorted * k + k_sorted)
    is_pad = slot_row < 0
    pad_rank = jnp.cumsum(is_pad.astype(jnp.int32)) - 1
    slot_tok = jnp.where(is_pad, 0, slot_row // k)
    slot_dst = jnp.where(is_pad, a + pad_rank, slot_row)
    block_start = jnp.arange(n_blocks, dtype=jnp.int32) * tb
    block_e = jnp.minimum(jnp.sum(block_start[:, None] >= pad_end[None, :], axis=1), n_exp - 1).astype(jnp.int32)
    n_used = (pad_end[-1] // tb).astype(jnp.int32).reshape(1)
    block_e = jnp.where(block_start < pad_end[-1], block_e, block_e[jnp.maximum(n_used[0] - 1, 0)])
    return slot_tok, slot_dst, block_e, n_used


def _layer(xs, layer_idx, w_in, conv_w, a_log, dt_bias, norm_w, lam, subln_w, w_branch_gdn, w_branch_diff,
           w_out, ln1_g, ln1_b, router_w, router_bias, w_gate, w_up, w_down, sh_gate, sh_up, sh_down,
           ln2_g, ln2_b, *, moe_tb=256):
    seqs = [(x.shape[0], x.shape[1]) for x in xs]
    x = jnp.concatenate([x.reshape(-1, D_MODEL) for x in xs], axis=0)
    n = x.shape[0]
    seq_starts, seq_ends, group_off = [], [], []
    off = 0
    for b, t in seqs:
        group_off.append(off)
        for _ in range(b):
            seq_starts.append(off)
            off += t
            seq_ends.append(off)
    seq_starts, seq_ends = tuple(seq_starts), tuple(seq_ends)

    o_qkv, o_z, o_ab = GDN_CONV_CH, GDN_CONV_CH + GDN_VAL_W, GDN_CONV_CH + GDN_VAL_W + GDN_GATE_W
    w_main = jnp.concatenate([w_in[:, :o_z], w_in[:, o_ab:]], axis=1).astype(BF16)
    w_ab = w_in[:, o_z:o_ab].astype(BF16)

    p = _proj_in(x, w_main)
    gates = _gdn_gates(x, w_ab, a_log, dt_bias)
    g_t = gates.T.reshape(GDN_GATE_W, 1, n)
    act = _gdn_conv(p, conv_w.astype(F32), seq_starts, seq_ends)
    o_fwd = _gdn_scan(act, g_t, p, None, norm_w, seq_starts, seq_ends, reverse=False)
    o_gdn = _gdn_scan(act, g_t, p, o_fwd, norm_w, seq_starts, seq_ends, reverse=True)
    o_diff = jnp.concatenate(
        [_diff_attention(p, lam, subln_w, goff, b, t, layer_idx) for (b, t), goff in zip(seqs, group_off)], axis=0)
    merged = _merge(o_gdn, o_diff, w_branch_gdn.astype(BF16), w_branch_diff.astype(BF16), p)
    h, h_slab = _out_ln(merged, w_out.astype(BF16), x, ln1_g, ln1_b)

    n_exp = router_w.shape[1]
    top_e, gw = _router(h, router_w, router_bias)
    slot_tok, slot_dst, block_e, n_used = _routing_tables(top_e, n_exp, moe_tb)
    y_tok = _experts(h_slab, slot_tok, slot_dst, block_e, n_used, w_gate, w_up, w_down, tb=moe_tb)
    sh_gu = jnp.concatenate([sh_gate, sh_up], axis=1).astype(BF16)
    outs = []
    for (b, t), goff in zip(seqs, group_off):
        y = _combine(h, y_tok, gw.T, sh_gu, sh_down.astype(BF16), ln2_g, ln2_b, goff, b * t)
        outs.append(y.reshape(b, t, D_MODEL))
    return outs


def kernel(x_prompt, x_sample, w_in, gdn_conv_w, gdn_a_log, gdn_dt_bias, gdn_norm_w, diff_lambda, diff_subln_w, w_branch_gdn, w_branch_diff, w_out, ln1_g, ln1_b, router_w, router_bias, exp_w_gate, exp_w_up, exp_w_down, sh_w_gate, sh_w_up, sh_w_down, ln2_g, ln2_b):
    xs = [x_prompt, x_sample]
    for l in range(DEPTH):
        xs = _layer(xs, l, w_in[l], gdn_conv_w[l], gdn_a_log[l], gdn_dt_bias[l], gdn_norm_w[l], diff_lambda[l],
                    diff_subln_w[l], w_branch_gdn[l], w_branch_diff[l], w_out[l], ln1_g[l], ln1_b[l], router_w[l],
                    router_bias[l], exp_w_gate[l], exp_w_up[l], exp_w_down[l], sh_w_gate[l], sh_w_up[l],
                    sh_w_down[l], ln2_g[l], ln2_b[l])
    return (xs[0], xs[1])
```

```python
import functools
import itertools
import math

import jax
import jax.numpy as jnp
from jax import lax
from jax.experimental import pallas as pl
from jax.experimental.pallas import tpu as pltpu

F32 = jnp.float32
BF16 = jnp.bfloat16
HIGHEST = lax.Precision.HIGHEST

D_MODEL = 2048
GDN_QK_HEADS = 16
GDN_V_HEADS = 32
GDN_DK = 128
GDN_DV = 128
GDN_KEY_W = GDN_QK_HEADS * GDN_DK
GDN_VAL_W = GDN_V_HEADS * GDN_DV
GDN_CONV_CH = 2 * GDN_KEY_W + GDN_VAL_W
GDN_GATE_W = 4 * GDN_V_HEADS
GDN_CONV_K = 5
GDN_CHUNK = 64
DIFF_HEADS = 8
DIFF_HD = 128
DIFF_QK_W = 2 * DIFF_HEADS * DIFF_HD
DIFF_VAL_W = DIFF_HEADS * 2 * DIFF_HD
N_GROUPS = 8
TOPK_GROUPS = 4
TOP_K = 8
ROUTED_SCALE = 2.5
DEPTH = 1
DN_ALPHA = (2 * DEPTH) ** 0.25
LN_EPS = 1e-5
RMS_EPS = 1e-6
NEG_BIG = -1e30
LOG2E = math.log2(math.e)
ATTN_STAGGER = 2

P_QKV = 0
P_Z = P_QKV + GDN_CONV_CH
P_DQ = P_Z + GDN_VAL_W
P_DK = P_DQ + DIFF_QK_W
P_DV = P_DK + DIFF_QK_W
P_GATES = P_DV + DIFF_VAL_W
P_COLS = P_GATES + 2 * D_MODEL

VMEM_LIMIT = 56 * 1024 * 1024
SLAB = D_MODEL // 128


def _params(grid_rank, vmem=VMEM_LIMIT):
    return pltpu.CompilerParams(dimension_semantics=("arbitrary",) * grid_rank, vmem_limit_bytes=vmem)


def _sigmoid(x):
    return 1.0 / (1.0 + jnp.exp(-x))


def _silu(x):
    return x * _sigmoid(x)


def _layer_norm(x, g, b):
    mu = jnp.mean(x, axis=-1, keepdims=True)
    xc = x - mu
    var = jnp.mean(xc * xc, axis=-1, keepdims=True)
    return xc * lax.rsqrt(var + LN_EPS) * g + b


def _proj_in_body(x_ref, w_ref, o_ref, xb_ref, *, tn):
    j = pl.program_id(1)

    @pl.when(j == 0)
    def _():
        xb_ref[...] = x_ref[...].astype(BF16)

    acc = jnp.dot(xb_ref[...], w_ref[...], preferred_element_type=F32)
    col = j * tn
    is_gate = col >= P_GATES
    is_dq = jnp.logical_and(col >= P_DQ, col < P_DK)
    scale = jnp.where(is_dq, LOG2E * DIFF_HD ** -0.5, 1.0).astype(F32)

    @pl.when(is_gate)
    def _():
        o_ref[...] = _sigmoid(acc).astype(o_ref.dtype)

    @pl.when(jnp.logical_not(is_gate))
    def _():
        o_ref[...] = (acc * scale).astype(o_ref.dtype)


def _proj_in(x, w, *, tm=1024, tn=512):
    n, k = x.shape
    m = w.shape[1]
    tm = min(tm, n)
    return pl.pallas_call(
        functools.partial(_proj_in_body, tn=tn),
        out_shape=jax.ShapeDtypeStruct((n, m), BF16),
        grid=(n // tm, m // tn),
        in_specs=[pl.BlockSpec((tm, k), lambda i, j: (i, 0)),
                  pl.BlockSpec((k, tn), lambda i, j: (0, j))],
        out_specs=pl.BlockSpec((tm, tn), lambda i, j: (i, j)),
        scratch_shapes=[pltpu.VMEM((tm, k), BF16)],
        compiler_params=_params(2),
        name="proj_in",
    )(x, w)


def _gates_body(x_ref, w_ref, prm_ref, o_ref, *, tt):
    ab = jnp.dot(x_ref[...].astype(BF16), w_ref[...], preferred_element_type=F32)
    a_log = prm_ref[0:1, :]
    dt_bias = prm_ref[1:2, :]
    y = ab + dt_bias
    softplus = jnp.maximum(y, 0.0) + jnp.log(1.0 + jnp.exp(-jnp.abs(y)))
    log_decay = -jnp.exp(a_log) * softplus
    beta = _sigmoid(ab)
    ri = lax.broadcasted_iota(jnp.int32, (tt, tt), 0)
    ci = lax.broadcasted_iota(jnp.int32, (tt, tt), 1)
    same = (ri // GDN_CHUNK) == (ci // GDN_CHUNK)
    m_lo = jnp.where(same, jnp.where(ci <= ri, 1.0, 0.0), 0.0).astype(F32)
    m_up = jnp.where(same, jnp.where(ci >= ri, 1.0, 0.0), 0.0).astype(F32)
    c_lo = jnp.dot(m_lo, log_decay, precision=HIGHEST, preferred_element_type=F32)
    c_up = jnp.dot(m_up, log_decay, precision=HIGHEST, preferred_element_type=F32)
    lane = lax.broadcasted_iota(jnp.int32, (tt, GDN_GATE_W), 1)
    is_a = (lane % (2 * GDN_V_HEADS)) < GDN_V_HEADS
    is_bwd = lane >= 2 * GDN_V_HEADS
    o_ref[...] = jnp.where(is_a, jnp.where(is_bwd, c_up, c_lo), beta)


def _gdn_gates(x, w_ab, a_log, dt_bias, *, tt=512):
    n, k = x.shape
    tt = min(tt, n)
    zeros = jnp.zeros((GDN_V_HEADS,), F32)
    prm = jnp.zeros((8, GDN_GATE_W), F32)
    prm = prm.at[0].set(jnp.concatenate([a_log[0], zeros, a_log[1], zeros]).astype(F32))
    prm = prm.at[1].set(jnp.concatenate([dt_bias[0], zeros, dt_bias[1], zeros]).astype(F32))
    return pl.pallas_call(
        functools.partial(_gates_body, tt=tt),
        out_shape=jax.ShapeDtypeStruct((n, GDN_GATE_W), F32),
        grid=(n // tt,),
        in_specs=[pl.BlockSpec((tt, k), lambda i: (i, 0)),
                  pl.BlockSpec((k, GDN_GATE_W), lambda i: (0, 0)),
                  pl.BlockSpec((8, GDN_GATE_W), lambda i: (0, 0))],
        out_specs=pl.BlockSpec((tt, GDN_GATE_W), lambda i: (i, 0)),
        compiler_params=_params(1),
        name="gdn_gates",
    )(x, w_ab, prm)


CONV_HALO = 16


def _any_equal(value, constants):
    hit = value == constants[0]
    for c in constants[1:]:
        hit = jnp.logical_or(hit, value == c)
    return hit


def _conv_body(prev_ref, cur_ref, next_ref, w_ref, o_ref, ext_ref, *, tt, tc, seq_starts, seq_ends):
    i = pl.program_id(0)
    j = pl.program_id(1)
    t0 = i * tt
    at_start = _any_equal(t0, seq_starts)
    at_end = _any_equal(t0 + tt, seq_ends)
    prev = prev_ref[...].astype(F32)[CONV_HALO - 8:, :]
    nxt = next_ref[...].astype(F32)[:8, :]
    ext_ref[0:8, :] = jnp.where(at_start, 0.0, prev)
    ext_ref[8:8 + tt, :] = cur_ref[...].astype(F32)
    ext_ref[8 + tt:16 + tt, :] = jnp.where(at_end, 0.0, nxt)
    pad = (GDN_CONV_K - 1) // 2
    acc = jnp.zeros((tt, tc), F32)
    for tap in range(GDN_CONV_K):
        acc = acc + ext_ref[pl.ds(8 - pad + tap, tt), :] * w_ref[tap:tap + 1, :]
    act = _silu(acc)
    col = j * tc
    is_qk = col < 2 * GDN_KEY_W
    q_scale = jnp.where(col < GDN_KEY_W, GDN_DK ** -0.5, 1.0).astype(F32)
    for s in range(tc // GDN_DK):
        a = act[:, s * GDN_DK:(s + 1) * GDN_DK]
        ss = jnp.sum(a * a, axis=-1, keepdims=True)
        normed = a * (lax.rsqrt(ss + RMS_EPS) * q_scale)
        o_ref[:, s * GDN_DK:(s + 1) * GDN_DK] = jnp.where(is_qk, normed, a).astype(o_ref.dtype)


def _gdn_conv(p, conv_w, seq_starts, seq_ends, *, tt=512, tc=512):
    n = p.shape[0]
    tt = min(tt, n)
    hb = tt // CONV_HALO
    n_halo = n // CONV_HALO
    return pl.pallas_call(
        functools.partial(_conv_body, tt=tt, tc=tc, seq_starts=seq_starts, seq_ends=seq_ends),
        out_shape=jax.ShapeDtypeStruct((n, GDN_CONV_CH), BF16),
        grid=(n // tt, GDN_CONV_CH // tc),
        in_specs=[pl.BlockSpec((CONV_HALO, tc), lambda i, j: (jnp.maximum(i * hb - 1, 0), j)),
                  pl.BlockSpec((tt, tc), lambda i, j: (i, j)),
                  pl.BlockSpec((CONV_HALO, tc), lambda i, j: (jnp.minimum((i + 1) * hb, n_halo - 1), j)),
                  pl.BlockSpec((GDN_CONV_K, tc), lambda i, j: (0, j))],
        out_specs=pl.BlockSpec((tt, tc), lambda i, j: (i, j)),
        scratch_shapes=[pltpu.VMEM((tt + 16, tc), F32)],
        compiler_params=_params(2),
        name="gdn_conv",
    )(p, p, p, conv_w)


def _dot_f32(a, b):
    return jnp.dot(a, b, precision=HIGHEST, preferred_element_type=F32)


def _dot_bf16(a, b):
    return jnp.dot(a.astype(BF16), b.astype(BF16), preferred_element_type=F32)


def _gdn_scan_body(*refs, tt, nt, hp, reverse, seq_starts, seq_ends):
    if reverse:
        q_ref, k_ref, v_ref, g_ref, b_ref, ofwd_ref, z_ref, nw_ref, o_ref, s_ref = refs
    else:
        q_ref, k_ref, v_ref, g_ref, b_ref, o_ref, s_ref = refs
    C = GDN_CHUNK
    t = pl.program_id(1)
    tok0 = ((nt - 1 - t) if reverse else t) * tt
    reset = _any_equal(tok0 + tt, seq_ends) if reverse else _any_equal(tok0, seq_starts)

    @pl.when(reset)
    def _():
        s_ref[...] = jnp.zeros_like(s_ref)

    n_chunks = tt // C
    ri = lax.broadcasted_iota(jnp.int32, (tt, tt), 0)
    ci = lax.broadcasted_iota(jnp.int32, (tt, tt), 1)
    same = (ri // C) == (ci // C)
    eye = ri == ci
    causal = jnp.logical_and(same, (ci >= ri) if reverse else (ci <= ri))
    strict = jnp.logical_and(same, (ci > ri) if reverse else (ci < ri))
    eye_f = jnp.where(eye, 1.0, 0.0).astype(F32)
    last = 0 if reverse else C - 1
    order = range(n_chunks - 1, -1, -1) if reverse else range(n_chunks)
    chunk_of_lane = lax.broadcasted_iota(jnp.int32, (1, tt), 1) // C
    chunk_of_lane_k = lax.broadcasted_iota(jnp.int32, (GDN_DK, tt), 1) // C

    shared = []
    for pp in range(hp):
        q = q_ref[:, pp * GDN_DK:(pp + 1) * GDN_DK]
        k = k_ref[:, pp * GDN_DK:(pp + 1) * GDN_DK]
        kf = k.astype(F32)
        kt = kf.T
        ktb = kt.astype(BF16)
        kq = jnp.dot(jnp.concatenate([k, q], axis=0), ktb, preferred_element_type=F32)
        shared.append((kf, q.astype(F32), kt, kq[:tt], kq[tt:]))
    per_head = [None] * (2 * hp)

    def head_stages(hh):
        kf, qf, kt, kk, qk = shared[hh // 2]
        cols = slice(hh * GDN_DV, (hh + 1) * GDN_DV)
        g_row = g_ref[hh]
        b_row = b_ref[hh]
        g_col = jnp.sum(jnp.where(eye, g_row, 0.0), axis=1, keepdims=True)
        b_col = jnp.sum(jnp.where(eye, b_row, 0.0), axis=1, keepdims=True)
        g_last = [g_row[:, c * C + last:c * C + last + 1] for c in range(n_chunks)]
        g_last_row = jnp.zeros((1, tt), F32)
        for c in range(n_chunks):
            g_last_row = jnp.where(chunk_of_lane == c, g_last[c], g_last_row)
        decay = jnp.exp(jnp.where(causal, g_col - g_row, NEG_BIG))
        low = jnp.where(strict, kk * decay, 0.0) * b_col
        low_b = low.astype(BF16)
        pw = jnp.dot(low_b, low_b, preferred_element_type=F32)
        yield
        inv = eye_f - low
        n_factors = int(math.log2(C)) - 1
        for j in range(n_factors):
            pw_b = pw.astype(BF16)
            if j < n_factors - 1:
                both = jnp.dot(jnp.concatenate([inv, pw], axis=0).astype(BF16), pw_b, preferred_element_type=F32)
                inv = inv + both[:tt]
                pw = both[tt:]
            else:
                inv = inv + jnp.dot(inv.astype(BF16), pw_b, preferred_element_type=F32)
            yield
        eg_col = jnp.exp(g_col)
        v = v_ref[:, cols].astype(F32)
        x = jnp.concatenate([v * b_col, kf * (b_col * eg_col)], axis=1)
        uw = jnp.dot(inv.astype(BF16), x.astype(BF16), preferred_element_type=F32).astype(BF16)
        yield
        from_q = jnp.dot((qk * decay).astype(BF16), uw, preferred_element_type=F32)
        yield
        ktd = kt * jnp.exp(g_last_row - g_row)
        lhs_k = jnp.concatenate([jnp.where(chunk_of_lane_k == c, ktd, 0.0) for c in range(n_chunks)],
                                axis=0).astype(BF16)
        from_k = jnp.dot(lhs_k, uw, preferred_element_type=F32)
        per_head[hh] = (qf * eg_col - from_q[:, GDN_DV:], from_q, from_k, g_last)
        yield

    for _ in itertools.zip_longest(*[head_stages(hh) for hh in range(2 * hp)]):
        pass

    for c in order:
        rows = slice(c * C, (c + 1) * C)
        krows = slice(c * GDN_DK, (c + 1) * GDN_DK)
        for pp in range(hp):
            heads = (2 * pp, 2 * pp + 1)
            lhs = jnp.concatenate([part for hh in heads
                                   for part in (per_head[hh][0][rows], per_head[hh][2][krows, GDN_DV:])],
                                  axis=0).astype(BF16)
            states = [s_ref[hh] for hh in heads]
            res = jnp.dot(lhs, jnp.concatenate(states, axis=1).astype(BF16), preferred_element_type=F32)
            for j, hh in enumerate(heads):
                _, from_q, from_k, g_last = per_head[hh]
                cols = slice(hh * GDN_DV, (hh + 1) * GDN_DV)
                blk = res[j * (C + GDN_DK):(j + 1) * (C + GDN_DK), j * GDN_DV:(j + 1) * GDN_DV]
                o = blk[:C] + from_q[rows, :GDN_DV]
                s_ref[hh] = states[j] * jnp.exp(g_last[c]) - blk[C:] + from_k[krows, :GDN_DV]
                if reverse:
                    tot = o + ofwd_ref[rows, cols]
                    ms = jnp.mean(tot * tot, axis=-1, keepdims=True)
                    z = z_ref[rows, cols].astype(F32)
                    o_ref[rows, cols] = (tot * lax.rsqrt(ms + RMS_EPS) * nw_ref[...] * _silu(z)).astype(o_ref.dtype)
                else:
                    o_ref[rows, cols] = o


def _gdn_scan(act, g_t, p, o_fwd, norm_w, seq_starts, seq_ends, *, reverse, tt=256, hp=4):
    n = act.shape[0]
    tt = min(tt, n)
    nt = n // tt
    groups = GDN_QK_HEADS // hp
    kw = hp * GDN_DK
    vw = 2 * hp * GDN_DV

    def tmap(t):
        return (nt - 1 - t) if reverse else t

    gate_blocks = (2 * GDN_V_HEADS) // (2 * hp)
    dirn = 1 if reverse else 0
    in_specs = [
        pl.BlockSpec((tt, kw), lambda h, t: (tmap(t), h)),
        pl.BlockSpec((tt, kw), lambda h, t: (tmap(t), groups + h)),
        pl.BlockSpec((tt, vw), lambda h, t: (tmap(t), 2 * GDN_KEY_W // vw + h)),
        pl.BlockSpec((2 * hp, 1, tt), lambda h, t: (dirn * gate_blocks + h, 0, tmap(t))),
        pl.BlockSpec((2 * hp, 1, tt), lambda h, t: (dirn * gate_blocks + groups + h, 0, tmap(t))),
    ]
    args = [act, act, act, g_t, g_t]
    if reverse:
        in_specs += [
            pl.BlockSpec((tt, vw), lambda h, t: (tmap(t), h)),
            pl.BlockSpec((tt, vw), lambda h, t: (tmap(t), P_Z // vw + h)),
            pl.BlockSpec((1, GDN_DV), lambda h, t: (0, 0)),
        ]
        args += [o_fwd, p, norm_w.reshape(1, GDN_DV).astype(F32)]
        out_dtype = BF16
    else:
        out_dtype = F32
    return pl.pallas_call(
        functools.partial(_gdn_scan_body, tt=tt, nt=nt, hp=hp, reverse=reverse,
                          seq_starts=seq_starts, seq_ends=seq_ends),
        out_shape=jax.ShapeDtypeStruct((n, GDN_VAL_W), out_dtype),
        grid=(groups, nt),
        in_specs=in_specs,
        out_specs=pl.BlockSpec((tt, vw), lambda h, t: (tmap(t), h)),
        scratch_shapes=[pltpu.VMEM((2 * hp, GDN_DK, GDN_DV), F32)],
        compiler_params=_params(2),
        name="gdn_scan_bwd" if reverse else "gdn_scan_fwd",
    )(*args)


def _attn_body(slope_ref, lam_ref, sw_ref, q_ref, k_ref, v_ref, o_ref, m_ref, l_ref, acc_ref, sd_ref,
               *, tile, nk, lam_init):
    h = pl.program_id(1)
    qi = pl.program_id(2)
    kj = pl.program_id(3)
    slope = slope_ref[h]

    n_sub = m_ref.shape[1]
    sub = tile // n_sub

    @pl.when(kj == 0)
    def _():
        m_ref[...] = jnp.full_like(m_ref, NEG_BIG)
        l_ref[...] = jnp.zeros_like(l_ref)
        acc_ref[...] = jnp.zeros_like(acc_ref)
        for r in range(n_sub):
            ii = lax.broadcasted_iota(jnp.int32, (sub, tile), 0) + r * sub
            jj = lax.broadcasted_iota(jnp.int32, (sub, tile), 1)
            sd_ref[r] = (ii - jj).astype(F32) * slope

    v = v_ref[...]
    tile_dist = jnp.full((sub, 1), (qi - kj) * tile, jnp.int32).astype(F32) * slope

    def update(add_bias, shift):
        def stages(r, m):
            rows = slice(r * sub, (r + 1) * sub)
            cols = slice(m * DIFF_HD, (m + 1) * DIFF_HD)
            t = add_bias(lax.dot_general(q_ref[rows, cols], k_ref[:, cols], (((1,), (1,)), ((), ())),
                                         preferred_element_type=F32), r)
            yield
            m_old = m_ref[m, r]
            m_new = jnp.maximum(m_old, jnp.max(t, axis=1, keepdims=True) - shift)
            alpha = jnp.exp2(m_old - m_new)
            m_ref[m, r] = m_new
            yield
            p = jnp.exp2(t - (m_new + shift))
            l_ref[m, r] = alpha * l_ref[m, r] + jnp.sum(p, axis=1, keepdims=True)
            p_b = p.astype(BF16)
            yield
            acc_ref[m, r] = alpha * acc_ref[m, r] + jnp.dot(p_b, v, preferred_element_type=F32)
            yield

        waiting = [stages(r, m) for r in range(n_sub) for m in range(2)]
        running = []
        step = 0
        while waiting or running:
            if waiting and step % ATTN_STAGGER == 0:
                running.append(waiting.pop(0))
            running = [g for g in running if next(g, "done") != "done"]
            step += 1

    @pl.when(kj < qi)
    def _():
        update(lambda s, r: s - sd_ref[r], tile_dist)

    @pl.when(kj > qi)
    def _():
        update(lambda s, r: s + sd_ref[r], -tile_dist)

    @pl.when(kj == qi)
    def _():
        update(lambda s, r: s - jnp.abs(sd_ref[r]), jnp.zeros((sub, 1), F32))

    @pl.when(kj == nk - 1)
    def _():
        lam = lam_ref[...]
        lam_full = (jnp.exp(jnp.sum(lam[0:1, :] * lam[1:2, :], axis=-1, keepdims=True))
                    - jnp.exp(jnp.sum(lam[2:3, :] * lam[3:4, :], axis=-1, keepdims=True)) + lam_init)
        for r in range(n_sub):
            o = acc_ref[0, r] / l_ref[0, r] - lam_full * (acc_ref[1, r] / l_ref[1, r])
            ms = jnp.mean(o * o, axis=-1, keepdims=True)
            o_ref[r * sub:(r + 1) * sub, :] = (o * lax.rsqrt(ms + LN_EPS) * sw_ref[...]
                                               * (1.0 - lam_init)).astype(o_ref.dtype)


def _diff_attention(p, lam, subln_w, tok_off, n_seq, seq_len, layer_idx, *, tile=1024, sub=512):
    tile = min(tile, seq_len)
    sub = min(sub, tile)
    n_sub = tile // sub
    nt = seq_len // tile
    lam_init = 0.8 - 0.6 * math.exp(-0.3 * layer_idx)
    slopes = jnp.asarray([LOG2E * 2.0 ** (-8.0 * (i + 1) / DIFF_HEADS) for i in range(DIFF_HEADS)], F32)
    w2 = 2 * DIFF_HD
    off = tok_off // tile
    return pl.pallas_call(
        functools.partial(_attn_body, tile=tile, nk=nt, lam_init=lam_init),
        out_shape=jax.ShapeDtypeStruct((n_seq * seq_len, DIFF_VAL_W), BF16),
        grid=(n_seq, DIFF_HEADS, nt, nt),
        in_specs=[pl.BlockSpec(memory_space=pltpu.SMEM),
                  pl.BlockSpec((4, DIFF_HD), lambda b, h, i, j: (0, 0)),
                  pl.BlockSpec((1, w2), lambda b, h, i, j: (0, 0)),
                  pl.BlockSpec((tile, w2), lambda b, h, i, j: (off + b * nt + i, P_DQ // w2 + h)),
                  pl.BlockSpec((tile, w2), lambda b, h, i, j: (off + b * nt + j, P_DK // w2 + h)),
                  pl.BlockSpec((tile, w2), lambda b, h, i, j: (off + b * nt + j, P_DV // w2 + h))],
        out_specs=pl.BlockSpec((tile, w2), lambda b, h, i, j: (b * nt + i, h)),
        scratch_shapes=[pltpu.VMEM((2, n_sub, sub, 1), F32), pltpu.VMEM((2, n_sub, sub, 1), F32),
                        pltpu.VMEM((2, n_sub, sub, w2), F32), pltpu.VMEM((n_sub, sub, tile), F32)],
        compiler_params=_params(4),
        name="diff_attention",
    )(slopes, lam.astype(F32), subln_w.reshape(1, w2).astype(F32), p, p, p)


def _merge_body(og_ref, od_ref, wg_ref, wd_ref, gg_ref, gd_ref, o_ref):
    a = jnp.dot(og_ref[...], wg_ref[...], preferred_element_type=F32)
    b = jnp.dot(od_ref[...], wd_ref[...], preferred_element_type=F32)
    o_ref[...] = (gg_ref[...].astype(F32) * a + gd_ref[...].astype(F32) * b).astype(o_ref.dtype)


def _merge(o_gdn, o_diff, w_g, w_d, p, *, tm=512, tn=512):
    n = o_gdn.shape[0]
    tm = min(tm, n)
    return pl.pallas_call(
        _merge_body,
        out_shape=jax.ShapeDtypeStruct((n, D_MODEL), BF16),
        grid=(n // tm, D_MODEL // tn),
        in_specs=[pl.BlockSpec((tm, GDN_VAL_W), lambda i, j: (i, 0)),
                  pl.BlockSpec((tm, DIFF_VAL_W), lambda i, j: (i, 0)),
                  pl.BlockSpec((GDN_VAL_W, tn), lambda i, j: (0, j)),
                  pl.BlockSpec((DIFF_VAL_W, tn), lambda i, j: (0, j)),
                  pl.BlockSpec((tm, tn), lambda i, j: (i, P_GATES // tn + j)),
                  pl.BlockSpec((tm, tn), lambda i, j: (i, (P_GATES + D_MODEL) // tn + j))],
        out_specs=pl.BlockSpec((tm, tn), lambda i, j: (i, j)),
        compiler_params=_params(2),
        name="branch_merge",
    )(o_gdn, o_diff, w_g, w_d, p, p)


def _store_slabs(slab_ref, value, n_rows):
    for s in range(SLAB):
        slab_ref[pl.ds(s, n_rows, stride=SLAB), :] = value[:, s * 128:(s + 1) * 128]


def _load_slabs(slab_ref, n_rows, first=0, stride=SLAB):
    return jnp.concatenate([slab_ref[pl.ds(first + s, n_rows, stride=stride), :] for s in range(SLAB)], axis=1)


def _out_ln_body(m_ref, w_ref, x_ref, g_ref, b_ref, o_ref, slab_ref, *, tm):
    mix = jnp.dot(m_ref[...], w_ref[...], preferred_element_type=F32)
    h = _layer_norm(DN_ALPHA * x_ref[...] + mix, g_ref[...], b_ref[...])
    o_ref[...] = h
    _store_slabs(slab_ref, h, tm)


def _out_ln(merged, w_out, x, g, b, *, tm=512):
    n = x.shape[0]
    tm = min(tm, n)
    return pl.pallas_call(
        functools.partial(_out_ln_body, tm=tm),
        out_shape=(jax.ShapeDtypeStruct((n, D_MODEL), F32), jax.ShapeDtypeStruct((n * SLAB, 128), F32)),
        grid=(n // tm,),
        in_specs=[pl.BlockSpec((tm, D_MODEL), lambda i: (i, 0)),
                  pl.BlockSpec((D_MODEL, D_MODEL), lambda i: (0, 0)),
                  pl.BlockSpec((tm, D_MODEL), lambda i: (i, 0)),
                  pl.BlockSpec((1, D_MODEL), lambda i: (0, 0)),
                  pl.BlockSpec((1, D_MODEL), lambda i: (0, 0))],
        out_specs=(pl.BlockSpec((tm, D_MODEL), lambda i: (i, 0)), pl.BlockSpec((tm * SLAB, 128), lambda i: (i, 0))),
        compiler_params=_params(1),
        name="out_proj_ln1",
    )(merged, w_out, x, g.reshape(1, D_MODEL).astype(F32), b.reshape(1, D_MODEL).astype(F32))


def _first_argmax(x, row_ids, n_rows):
    mx = jnp.max(x, axis=0, keepdims=True)
    idx = jnp.min(jnp.where(x == mx, row_ids, n_rows), axis=0, keepdims=True)
    return mx, idx


def _router_body(h_ref, w_ref, bias_ref, e_ref, g_ref, *, n_exp, tm):
    per = n_exp // N_GROUPS
    logits = lax.dot_general(w_ref[...], h_ref[...], (((1,), (1,)), ((), ())),
                             precision=HIGHEST, preferred_element_type=F32)
    scores = _sigmoid(logits)
    choice = scores + bias_ref[...]
    ids_g = lax.broadcasted_iota(jnp.int32, (per, tm), 0)
    group_rows = []
    for g in range(N_GROUPS):
        xg = choice[g * per:(g + 1) * per, :]
        m1, i1 = _first_argmax(xg, ids_g, per)
        m2 = jnp.max(jnp.where(ids_g == i1, -jnp.inf, xg), axis=0, keepdims=True)
        group_rows.append(m1 + m2)
    gs = jnp.concatenate(group_rows, axis=0)
    ids_n = lax.broadcasted_iota(jnp.int32, (N_GROUPS, tm), 0)
    keep = jnp.zeros((N_GROUPS, tm), F32)
    for _ in range(TOPK_GROUPS):
        _, gi = _first_argmax(gs, ids_n, N_GROUPS)
        hit = ids_n == gi
        keep = jnp.where(hit, 1.0, keep)
        gs = jnp.where(hit, -jnp.inf, gs)
    masked = jnp.concatenate(
        [jnp.where(keep[g:g + 1, :] > 0.5, choice[g * per:(g + 1) * per, :], -jnp.inf)
         for g in range(N_GROUPS)], axis=0)
    ids_e = lax.broadcasted_iota(jnp.int32, (n_exp, tm), 0)
    top_idx = []
    top_w = []
    for _ in range(TOP_K):
        _, ei = _first_argmax(masked, ids_e, n_exp)
        hit = ids_e == ei
        top_idx.append(ei)
        top_w.append(jnp.sum(jnp.where(hit, scores, 0.0), axis=0, keepdims=True))
        masked = jnp.where(hit, -jnp.inf, masked)
    gw = jnp.concatenate(top_w, axis=0)
    gw = gw / (jnp.sum(gw, axis=0, keepdims=True) + 1e-20) * ROUTED_SCALE
    e_ref[...] = jnp.concatenate(top_idx, axis=0)
    g_ref[...] = gw


def _router(h, router_w, router_bias, *, tm=256):
    n = h.shape[0]
    n_exp = router_w.shape[1]
    tm = min(tm, n)
    return pl.pallas_call(
        functools.partial(_router_body, n_exp=n_exp, tm=tm),
        out_shape=(jax.ShapeDtypeStruct((TOP_K, n), jnp.int32), jax.ShapeDtypeStruct((TOP_K, n), F32)),
        grid=(n // tm,),
        in_specs=[pl.BlockSpec((tm, D_MODEL), lambda i: (i, 0)),
                  pl.BlockSpec((n_exp, D_MODEL), lambda i: (0, 0)),
                  pl.BlockSpec((n_exp, 1), lambda i: (0, 0))],
        out_specs=(pl.BlockSpec((TOP_K, tm), lambda i: (0, i)), pl.BlockSpec((TOP_K, tm), lambda i: (0, i))),
        compiler_params=_params(1),
        name="moe_router",
    )(h, router_w.T.astype(F32), router_bias.reshape(n_exp, 1).astype(F32))


def _expert_body(be_ref, nu_ref, tok_ref, tok_next_ref, dst_ref, h_ref, wg_ref, wu_ref, wd_ref, y_ref,
                 x_buf, y_buf, wgb_ref, wub_ref, wdb_ref, gather_sem, scatter_sem, *, tb, n_blocks):
    i = pl.program_id(0)
    n_used = nu_ref[0]
    slot = i % 2

    def start_gather(rows_ref, buf):
        def issue(r, carry):
            src = pl.multiple_of(rows_ref[0, 0, r] * SLAB, SLAB)
            pltpu.make_async_copy(h_ref.at[pl.ds(src, SLAB)], x_buf.at[buf, pl.ds(r * SLAB, SLAB)],
                                  gather_sem.at[buf]).start()
            return carry
        lax.fori_loop(0, tb, issue, 0, unroll=8)

    def wait_buffer(buf_ref, sem_ref, buf):
        pltpu.make_async_copy(buf_ref.at[buf], buf_ref.at[buf], sem_ref.at[buf]).wait()

    @pl.when(jnp.logical_and(i == 0, n_used > 0))
    def _():
        start_gather(tok_ref, 0)

    @pl.when(i + 1 < n_used)
    def _():
        start_gather(tok_next_ref, 1 - slot)

    @pl.when(i < n_used)
    def _():
        fresh = jnp.logical_or(i == 0, be_ref[i] != be_ref[jnp.maximum(i - 1, 0)])

        @pl.when(fresh)
        def _():
            wgb_ref[...] = wg_ref[...].astype(BF16)
            wub_ref[...] = wu_ref[...].astype(BF16)
            wdb_ref[...] = wd_ref[...].astype(BF16)

        wait_buffer(x_buf, gather_sem, slot)

        @pl.when(i >= 2)
        def _():
            wait_buffer(y_buf, scatter_sem, slot)

        xb = _load_slabs(x_buf.at[slot], tb).astype(BF16)
        gate = jnp.dot(xb, wgb_ref[...], preferred_element_type=F32)
        up = jnp.dot(xb, wub_ref[...], preferred_element_type=F32)
        act = (_silu(gate) * up).astype(BF16)
        _store_slabs(y_buf.at[slot], jnp.dot(act, wdb_ref[...], preferred_element_type=F32), tb)

        def issue(r, carry):
            dst = pl.multiple_of(dst_ref[0, 0, r] * SLAB, SLAB)
            pltpu.make_async_copy(y_buf.at[slot, pl.ds(r * SLAB, SLAB)], y_ref.at[pl.ds(dst, SLAB)],
                                  scatter_sem.at[slot]).start()
            return carry
        lax.fori_loop(0, tb, issue, 0, unroll=8)

    @pl.when(i == n_blocks - 1)
    def _():
        @pl.when(n_used >= 1)
        def _():
            wait_buffer(y_buf, scatter_sem, (n_used - 1) % 2)

        @pl.when(n_used >= 2)
        def _():
            wait_buffer(y_buf, scatter_sem, n_used % 2)


def _experts(h_slab, slot_tok, slot_dst, block_e, n_used, w_gate, w_up, w_down, *, tb):
    n_slots = slot_tok.shape[0]
    n_blocks = n_slots // tb
    ff = w_gate.shape[2]
    idx_shape = (n_blocks, 1, tb)
    grid_spec = pltpu.PrefetchScalarGridSpec(
        num_scalar_prefetch=2,
        grid=(n_blocks,),
        in_specs=[pl.BlockSpec((1, 1, tb), lambda i, be, nu: (i, 0, 0), memory_space=pltpu.SMEM),
                  pl.BlockSpec((1, 1, tb), lambda i, be, nu: (jnp.minimum(i + 1, n_blocks - 1), 0, 0),
                               memory_space=pltpu.SMEM),
                  pl.BlockSpec((1, 1, tb), lambda i, be, nu: (i, 0, 0), memory_space=pltpu.SMEM),
                  pl.BlockSpec(memory_space=pl.ANY),
                  pl.BlockSpec((None, D_MODEL, ff), lambda i, be, nu: (be[i], 0, 0)),
                  pl.BlockSpec((None, D_MODEL, ff), lambda i, be, nu: (be[i], 0, 0)),
                  pl.BlockSpec((None, ff, D_MODEL), lambda i, be, nu: (be[i], 0, 0))],
        out_specs=pl.BlockSpec(memory_space=pl.ANY),
        scratch_shapes=[pltpu.VMEM((2, tb * SLAB, 128), F32), pltpu.VMEM((2, tb * SLAB, 128), F32),
                        pltpu.VMEM((D_MODEL, ff), BF16), pltpu.VMEM((D_MODEL, ff), BF16),
                        pltpu.VMEM((ff, D_MODEL), BF16),
                        pltpu.SemaphoreType.DMA((2,)), pltpu.SemaphoreType.DMA((2,))],
    )
    tok3 = slot_tok.reshape(idx_shape)
    return pl.pallas_call(
        functools.partial(_expert_body, tb=tb, n_blocks=n_blocks),
        out_shape=jax.ShapeDtypeStruct((n_slots * SLAB, 128), F32),
        grid_spec=grid_spec,
        compiler_params=_params(1),
        name="moe_experts",
    )(block_e, n_used, tok3, tok3, slot_dst.reshape(idx_shape), h_slab, w_gate, w_up, w_down)


def _combine_body(h_ref, y_ref, gw_ref, sgu_ref, sd_ref, g_ref, b_ref, o_ref, *, ff, tm):
    h = h_ref[...]
    hb = h.astype(BF16)
    gu = jnp.dot(hb, sgu_ref[...], preferred_element_type=F32)
    act = (_silu(gu[:, :ff]) * gu[:, ff:]).astype(BF16)
    f = jnp.dot(act, sd_ref[...], preferred_element_type=F32)
    gw = gw_ref[...]
    for k in range(TOP_K):
        f = f + _load_slabs(y_ref, tm, first=k * SLAB, stride=TOP_K * SLAB) * gw[:, k:k + 1]
    o_ref[...] = _layer_norm(DN_ALPHA * h + f, g_ref[...], b_ref[...])


def _combine(h, y_tok, gw_t, sh_gu, sh_down, g, b, tok_off, n_rows, *, tm=128):
    ff = sh_down.shape[0]
    tm = min(tm, n_rows)
    off = tok_off // tm
    return pl.pallas_call(
        functools.partial(_combine_body, ff=ff, tm=tm),
        out_shape=jax.ShapeDtypeStruct((n_rows, D_MODEL), F32),
        grid=(n_rows // tm,),
        in_specs=[pl.BlockSpec((tm, D_MODEL), lambda i: (off + i, 0)),
                  pl.BlockSpec((tm * TOP_K * SLAB, 128), lambda i: (off + i, 0)),
                  pl.BlockSpec((tm, TOP_K), lambda i: (off + i, 0)),
                  pl.BlockSpec((D_MODEL, 2 * ff), lambda i: (0, 0)),
                  pl.BlockSpec((ff, D_MODEL), lambda i: (0, 0)),
                  pl.BlockSpec((1, D_MODEL), lambda i: (0, 0)),
                  pl.BlockSpec((1, D_MODEL), lambda i: (0, 0))],
        out_specs=pl.BlockSpec((tm, D_MODEL), lambda i: (i, 0)),
        compiler_params=_params(1),
        name="moe_combine_ln2",
    )(h, y_tok, gw_t, sh_gu, sh_down, g.reshape(1, D_MODEL).astype(F32), b.reshape(1, D_MODEL).astype(F32))


def _routing_tables(top_e, n_exp, tb):
    k, n = top_e.shape
    a = k * n
    n_blocks = a // tb + n_exp
    a_bits = max(a - 1, 1).bit_length()
    assert n_exp << a_bits < 2 ** 31
    ids = jnp.arange(a, dtype=jnp.int32)
    keys = lax.sort(top_e.reshape(-1) * (1 << a_bits) + ids)
    e_sorted = keys >> a_bits
    a_sorted = keys & ((1 << a_bits) - 1)
    tok_sorted = a_sorted % n
    k_sorted = a_sorted // n
    experts = jnp.arange(n_exp, dtype=jnp.int32)
    start = jnp.sum((e_sorted[:, None] < experts[None, :]).astype(jnp.int32), axis=0)
    counts = jnp.concatenate([start[1:], jnp.full((1,), a, jnp.int32)]) - start
    padded = (counts + tb - 1) // tb * tb
    pad_end = jnp.cumsum(padded)
    shift = pad_end - padded - start
    dest = ids + jnp.sum(jnp.where(e_sorted[:, None] == experts[None, :], shift[None, :], 0), axis=1)
    slot_row = jnp.full((n_blocks * tb,), -1, jnp.int32).at[dest].set(tok_sorted * k + k_sorted)
    is_pad = slot_row < 0
    pad_rank = jnp.cumsum(is_pad.astype(jnp.int32)) - 1
    slot_tok = jnp.where(is_pad, 0, slot_row // k)
    slot_dst = jnp.where(is_pad, a + pad_rank, slot_row)
    block_start = jnp.arange(n_blocks, dtype=jnp.int32) * tb
    block_e = jnp.minimum(jnp.sum(block_start[:, None] >= pad_end[None, :], axis=1), n_exp - 1).astype(jnp.int32)
    n_used = (pad_end[-1] // tb).astype(jnp.int32).reshape(1)
    block_e = jnp.where(block_start < pad_end[-1], block_e, block_e[jnp.maximum(n_used[0] - 1, 0)])
    return slot_tok, slot_dst, block_e, n_used


def _layer(xs, layer_idx, w_in, conv_w, a_log, dt_bias, norm_w, lam, subln_w, w_branch_gdn, w_branch_diff,
           w_out, ln1_g, ln1_b, router_w, router_bias, w_gate, w_up, w_down, sh_gate, sh_up, sh_down,
           ln2_g, ln2_b, *, moe_tb=256):
    seqs = [(x.shape[0], x.shape[1]) for x in xs]
    x = jnp.concatenate([x.reshape(-1, D_MODEL) for x in xs], axis=0)
    n = x.shape[0]
    seq_starts, seq_ends, group_off = [], [], []
    off = 0
    for b, t in seqs:
        group_off.append(off)
        for _ in range(b):
            seq_starts.append(off)
            off += t
            seq_ends.append(off)
    seq_starts, seq_ends = tuple(seq_starts), tuple(seq_ends)

    o_qkv, o_z, o_ab = GDN_CONV_CH, GDN_CONV_CH + GDN_VAL_W, GDN_CONV_CH + GDN_VAL_W + GDN_GATE_W
    w_main = jnp.concatenate([w_in[:, :o_z], w_in[:, o_ab:]], axis=1).astype(BF16)
    w_ab = w_in[:, o_z:o_ab].astype(BF16)

    p = _proj_in(x, w_main)
    gates = _gdn_gates(x, w_ab, a_log, dt_bias)
    g_t = gates.T.reshape(GDN_GATE_W, 1, n)
    act = _gdn_conv(p, conv_w.astype(F32), seq_starts, seq_ends)
    o_fwd = _gdn_scan(act, g_t, p, None, norm_w, seq_starts, seq_ends, reverse=False)
    o_gdn = _gdn_scan(act, g_t, p, o_fwd, norm_w, seq_starts, seq_ends, reverse=True)
    o_diff = jnp.concatenate(
        [_diff_attention(p, lam, subln_w, goff, b, t, layer_idx) for (b, t), goff in zip(seqs, group_off)], axis=0)
    merged = _merge(o_gdn, o_diff, w_branch_gdn.astype(BF16), w_branch_diff.astype(BF16), p)
    h, h_slab = _out_ln(merged, w_out.astype(BF16), x, ln1_g, ln1_b)

    n_exp = router_w.shape[1]
    top_e, gw = _router(h, router_w, router_bias)
    slot_tok, slot_dst, block_e, n_used = _routing_tables(top_e, n_exp, moe_tb)
    y_tok = _experts(h_slab, slot_tok, slot_dst, block_e, n_used, w_gate, w_up, w_down, tb=moe_tb)
    sh_gu = jnp.concatenate([sh_gate, sh_up], axis=1).astype(BF16)
    outs = []
    for (b, t), goff in zip(seqs, group_off):
        y = _combine(h, y_tok, gw.T, sh_gu, sh_down.astype(BF16), ln2_g, ln2_b, goff, b * t)
        outs.append(y.reshape(b, t, D_MODEL))
    return outs


def kernel(x_prompt, x_sample, w_in, gdn_conv_w, gdn_a_log, gdn_dt_bias, gdn_norm_w, diff_lambda, diff_subln_w, w_branch_gdn, w_branch_diff, w_out, ln1_g, ln1_b, router_w, router_bias, exp_w_gate, exp_w_up, exp_w_down, sh_w_gate, sh_w_up, sh_w_down, ln2_g, ln2_b):
    xs = [x_prompt, x_sample]
    for l in range(DEPTH):
        xs = _layer(xs, l, w_in[l], gdn_conv_w[l], gdn_a_log[l], gdn_dt_bias[l], gdn_norm_w[l], diff_lambda[l],
                    diff_subln_w[l], w_branch_gdn[l], w_branch_diff[l], w_out[l], ln1_g[l], ln1_b[l], router_w[l],
                    router_bias[l], exp_w_gate[l], exp_w_up[l], exp_w_down[l], sh_w_gate[l], sh_w_up[l],
                    sh_w_down[l], ln2_g[l], ln2_b[l])
    return (xs[0], xs[1])
```

```python
import functools
import itertools
import math

import jax
import jax.numpy as jnp
from jax import lax
from jax.experimental import pallas as pl
from jax.experimental.pallas import tpu as pltpu

F32 = jnp.float32
BF16 = jnp.bfloat16
HIGHEST = lax.Precision.HIGHEST

D_MODEL = 2048
GDN_QK_HEADS = 16
GDN_V_HEADS = 32
GDN_DK = 128
GDN_DV = 128
GDN_KEY_W = GDN_QK_HEADS * GDN_DK
GDN_VAL_W = GDN_V_HEADS * GDN_DV
GDN_CONV_CH = 2 * GDN_KEY_W + GDN_VAL_W
GDN_GATE_W = 4 * GDN_V_HEADS
GDN_CONV_K = 5
GDN_CHUNK = 64
DIFF_HEADS = 8
DIFF_HD = 128
DIFF_QK_W = 2 * DIFF_HEADS * DIFF_HD
DIFF_VAL_W = DIFF_HEADS * 2 * DIFF_HD
N_GROUPS = 8
TOPK_GROUPS = 4
TOP_K = 8
ROUTED_SCALE = 2.5
DEPTH = 1
DN_ALPHA = (2 * DEPTH) ** 0.25
LN_EPS = 1e-5
RMS_EPS = 1e-6
NEG_BIG = -1e30
LOG2E = math.log2(math.e)
ATTN_STAGGER = 2

P_QKV = 0
P_Z = P_QKV + GDN_CONV_CH
P_DQ = P_Z + GDN_VAL_W
P_DK = P_DQ + DIFF_QK_W
P_DV = P_DK + DIFF_QK_W
P_GATES = P_DV + DIFF_VAL_W
P_COLS = P_GATES + 2 * D_MODEL

VMEM_LIMIT = 56 * 1024 * 1024
SLAB = D_MODEL // 128


def _params(grid_rank, vmem=VMEM_LIMIT):
    return pltpu.CompilerParams(dimension_semantics=("arbitrary",) * grid_rank, vmem_limit_bytes=vmem)


def _sigmoid(x):
    return 1.0 / (1.0 + jnp.exp(-x))


def _silu(x):
    return x * _sigmoid(x)


def _layer_norm(x, g, b):
    mu = jnp.mean(x, axis=-1, keepdims=True)
    xc = x - mu
    var = jnp.mean(xc * xc, axis=-1, keepdims=True)
    return xc * lax.rsqrt(var + LN_EPS) * g + b


def _proj_in_body(x_ref, w_ref, o_ref, xb_ref, *, tn):
    j = pl.program_id(1)

    @pl.when(j == 0)
    def _():
        xb_ref[...] = x_ref[...].astype(BF16)

    acc = jnp.dot(xb_ref[...], w_ref[...], preferred_element_type=F32)
    col = j * tn
    is_gate = col >= P_GATES
    is_dq = jnp.logical_and(col >= P_DQ, col < P_DK)
    scale = jnp.where(is_dq, LOG2E * DIFF_HD ** -0.5, 1.0).astype(F32)

    @pl.when(is_gate)
    def _():
        o_ref[...] = _sigmoid(acc).astype(o_ref.dtype)

    @pl.when(jnp.logical_not(is_gate))
    def _():
        o_ref[...] = (acc * scale).astype(o_ref.dtype)


def _proj_in(x, w, *, tm=1024, tn=512):
    n, k = x.shape
    m = w.shape[1]
    tm = min(tm, n)
    return pl.pallas_call(
        functools.partial(_proj_in_body, tn=tn),
        out_shape=jax.ShapeDtypeStruct((n, m), BF16),
        grid=(n // tm, m // tn),
        in_specs=[pl.BlockSpec((tm, k), lambda i, j: (i, 0)),
                  pl.BlockSpec((k, tn), lambda i, j: (0, j))],
        out_specs=pl.BlockSpec((tm, tn), lambda i, j: (i, j)),
        scratch_shapes=[pltpu.VMEM((tm, k), BF16)],
        compiler_params=_params(2),
        name="proj_in",
    )(x, w)


def _gates_body(x_ref, w_ref, prm_ref, o_ref, *, tt):
    ab = jnp.dot(x_ref[...].astype(BF16), w_ref[...], preferred_element_type=F32)
    a_log = prm_ref[0:1, :]
    dt_bias = prm_ref[1:2, :]
    y = ab + dt_bias
    softplus = jnp.maximum(y, 0.0) + jnp.log(1.0 + jnp.exp(-jnp.abs(y)))
    log_decay = -jnp.exp(a_log) * softplus
    beta = _sigmoid(ab)
    ri = lax.broadcasted_iota(jnp.int32, (tt, tt), 0)
    ci = lax.broadcasted_iota(jnp.int32, (tt, tt), 1)
    same = (ri // GDN_CHUNK) == (ci // GDN_CHUNK)
    m_lo = jnp.where(same, jnp.where(ci <= ri, 1.0, 0.0), 0.0).astype(F32)
    m_up = jnp.where(same, jnp.where(ci >= ri, 1.0, 0.0), 0.0).astype(F32)
    c_lo = jnp.dot(m_lo, log_decay, precision=HIGHEST, preferred_element_type=F32)
    c_up = jnp.dot(m_up, log_decay, precision=HIGHEST, preferred_element_type=F32)
    lane = lax.broadcasted_iota(jnp.int32, (tt, GDN_GATE_W), 1)
    is_a = (lane % (2 * GDN_V_HEADS)) < GDN_V_HEADS
    is_bwd = lane >= 2 * GDN_V_HEADS
    o_ref[...] = jnp.where(is_a, jnp.where(is_bwd, c_up, c_lo), beta)


def _gdn_gates(x, w_ab, a_log, dt_bias, *, tt=512):
    n, k = x.shape
    tt = min(tt, n)
    zeros = jnp.zeros((GDN_V_HEADS,), F32)
    prm = jnp.zeros((8, GDN_GATE_W), F32)
    prm = prm.at[0].set(jnp.concatenate([a_log[0], zeros, a_log[1], zeros]).astype(F32))
    prm = prm.at[1].set(jnp.concatenate([dt_bias[0], zeros, dt_bias[1], zeros]).astype(F32))
    return pl.pallas_call(
        functools.partial(_gates_body, tt=tt),
        out_shape=jax.ShapeDtypeStruct((n, GDN_GATE_W), F32),
        grid=(n // tt,),
        in_specs=[pl.BlockSpec((tt, k), lambda i: (i, 0)),
                  pl.BlockSpec((k, GDN_GATE_W), lambda i: (0, 0)),
                  pl.BlockSpec((8, GDN_GATE_W), lambda i: (0, 0))],
        out_specs=pl.BlockSpec((tt, GDN_GATE_W), lambda i: (i, 0)),
        compiler_params=_params(1),
        name="gdn_gates",
    )(x, w_ab, prm)


CONV_HALO = 16


def _any_equal(value, constants):
    hit = value == constants[0]
    for c in constants[1:]:
        hit = jnp.logical_or(hit, value == c)
    return hit


def _conv_body(prev_ref, cur_ref, next_ref, w_ref, o_ref, ext_ref, *, tt, tc, seq_starts, seq_ends):
    i = pl.program_id(0)
    j = pl.program_id(1)
    t0 = i * tt
    at_start = _any_equal(t0, seq_starts)
    at_end = _any_equal(t0 + tt, seq_ends)
    prev = prev_ref[...].astype(F32)[CONV_HALO - 8:, :]
    nxt = next_ref[...].astype(F32)[:8, :]
    ext_ref[0:8, :] = jnp.where(at_start, 0.0, prev)
    ext_ref[8:8 + tt, :] = cur_ref[...].astype(F32)
    ext_ref[8 + tt:16 + tt, :] = jnp.where(at_end, 0.0, nxt)
    pad = (GDN_CONV_K - 1) // 2
    acc = jnp.zeros((tt, tc), F32)
    for tap in range(GDN_CONV_K):
        acc = acc + ext_ref[pl.ds(8 - pad + tap, tt), :] * w_ref[tap:tap + 1, :]
    act = _silu(acc)
    col = j * tc
    is_qk = col < 2 * GDN_KEY_W
    q_scale = jnp.where(col < GDN_KEY_W, GDN_DK ** -0.5, 1.0).astype(F32)
    for s in range(tc // GDN_DK):
        a = act[:, s * GDN_DK:(s + 1) * GDN_DK]
        ss = jnp.sum(a * a, axis=-1, keepdims=True)
        normed = a * (lax.rsqrt(ss + RMS_EPS) * q_scale)
        o_ref[:, s * GDN_DK:(s + 1) * GDN_DK] = jnp.where(is_qk, normed, a).astype(o_ref.dtype)


def _gdn_conv(p, conv_w, seq_starts, seq_ends, *, tt=512, tc=512):
    n = p.shape[0]
    tt = min(tt, n)
    hb = tt // CONV_HALO
    n_halo = n // CONV_HALO
    return pl.pallas_call(
        functools.partial(_conv_body, tt=tt, tc=tc, seq_starts=seq_starts, seq_ends=seq_ends),
        out_shape=jax.ShapeDtypeStruct((n, GDN_CONV_CH), BF16),
        grid=(n // tt, GDN_CONV_CH // tc),
        in_specs=[pl.BlockSpec((CONV_HALO, tc), lambda i, j: (jnp.maximum(i * hb - 1, 0), j)),
                  pl.BlockSpec((tt, tc), lambda i, j: (i, j)),
                  pl.BlockSpec((CONV_HALO, tc), lambda i, j: (jnp.minimum((i + 1) * hb, n_halo - 1), j)),
                  pl.BlockSpec((GDN_CONV_K, tc), lambda i, j: (0, j))],
        out_specs=pl.BlockSpec((tt, tc), lambda i, j: (i, j)),
        scratch_shapes=[pltpu.VMEM((tt + 16, tc), F32)],
        compiler_params=_params(2),
        name="gdn_conv",
    )(p, p, p, conv_w)


def _dot_f32(a, b):
    return jnp.dot(a, b, precision=HIGHEST, preferred_element_type=F32)


def _dot_bf16(a, b):
    return jnp.dot(a.astype(BF16), b.astype(BF16), preferred_element_type=F32)


def _gdn_scan_body(*refs, tt, nt, hp, reverse, seq_starts, seq_ends):
    if reverse:
        q_ref, k_ref, v_ref, g_ref, b_ref, ofwd_ref, z_ref, nw_ref, o_ref, s_ref = refs
    else:
        q_ref, k_ref, v_ref, g_ref, b_ref, o_ref, s_ref = refs
    C = GDN_CHUNK
    t = pl.program_id(1)
    tok0 = ((nt - 1 - t) if reverse else t) * tt
    reset = _any_equal(tok0 + tt, seq_ends) if reverse else _any_equal(tok0, seq_starts)

    @pl.when(reset)
    def _():
        s_ref[...] = jnp.zeros_like(s_ref)

    n_chunks = tt // C
    ri = lax.broadcasted_iota(jnp.int32, (tt, tt), 0)
    ci = lax.broadcasted_iota(jnp.int32, (tt, tt), 1)
    same = (ri // C) == (ci // C)
    eye = ri == ci
    causal = jnp.logical_and(same, (ci >= ri) if reverse else (ci <= ri))
    strict = jnp.logical_and(same, (ci > ri) if reverse else (ci < ri))
    eye_f = jnp.where(eye, 1.0, 0.0).astype(F32)
    last = 0 if reverse else C - 1
    order = range(n_chunks - 1, -1, -1) if reverse else range(n_chunks)
    chunk_of_lane = lax.broadcasted_iota(jnp.int32, (1, tt), 1) // C
    chunk_of_lane_k = lax.broadcasted_iota(jnp.int32, (GDN_DK, tt), 1) // C

    shared = []
    for pp in range(hp):
        q = q_ref[:, pp * GDN_DK:(pp + 1) * GDN_DK]
        k = k_ref[:, pp * GDN_DK:(pp + 1) * GDN_DK]
        kf = k.astype(F32)
        kt = kf.T
        ktb = kt.astype(BF16)
        kq = jnp.dot(jnp.concatenate([k, q], axis=0), ktb, preferred_element_type=F32)
        shared.append((kf, q.astype(F32), kt, kq[:tt], kq[tt:]))
    per_head = [None] * (2 * hp)

    def head_stages(hh):
        kf, qf, kt, kk, qk = shared[hh // 2]
        cols = slice(hh * GDN_DV, (hh + 1) * GDN_DV)
        g_row = g_ref[hh]
        b_row = b_ref[hh]
        g_col = jnp.sum(jnp.where(eye, g_row, 0.0), axis=1, keepdims=True)
        b_col = jnp.sum(jnp.where(eye, b_row, 0.0), axis=1, keepdims=True)
        g_last = [g_row[:, c * C + last:c * C + last + 1] for c in range(n_chunks)]
        g_last_row = jnp.zeros((1, tt), F32)
        for c in range(n_chunks):
            g_last_row = jnp.where(chunk_of_lane == c, g_last[c], g_last_row)
        decay = jnp.exp(jnp.where(causal, g_col - g_row, NEG_BIG))
        low = jnp.where(strict, kk * decay, 0.0) * b_col
        low_b = low.astype(BF16)
        pw = jnp.dot(low_b, low_b, preferred_element_type=F32)
        yield
        inv = eye_f - low
        n_factors = int(math.log2(C)) - 1
        for j in range(n_factors):
            pw_b = pw.astype(BF16)
            if j < n_factors - 1:
                both = jnp.dot(jnp.concatenate([inv, pw], axis=0).astype(BF16), pw_b, preferred_element_type=F32)
                inv = inv + both[:tt]
                pw = both[tt:]
            else:
                inv = inv + jnp.dot(inv.astype(BF16), pw_b, preferred_element_type=F32)
            yield
        eg_col = jnp.exp(g_col)
        v = v_ref[:, cols].astype(F32)
        x = jnp.concatenate([v * b_col, kf * (b_col * eg_col)], axis=1)
        uw = jnp.dot(inv.astype(BF16), x.astype(BF16), preferred_element_type=F32).astype(BF16)
        yield
        from_q = jnp.dot((qk * decay).astype(BF16), uw, preferred_element_type=F32)
        yield
        ktd = kt * jnp.exp(g_last_row - g_row)
        lhs_k = jnp.concatenate([jnp.where(chunk_of_lane_k == c, ktd, 0.0) for c in range(n_chunks)],
                                axis=0).astype(BF16)
        from_k = jnp.dot(lhs_k, uw, preferred_element_type=F32)
        per_head[hh] = (qf * eg_col - from_q[:, GDN_DV:], from_q, from_k, g_last)
        yield

    for _ in itertools.zip_longest(*[head_stages(hh) for hh in range(2 * hp)]):
        pass

    for c in order:
        rows = slice(c * C, (c + 1) * C)
        krows = slice(c * GDN_DK, (c + 1) * GDN_DK)
        for pp in range(hp):
            heads = (2 * pp, 2 * pp + 1)
            lhs = jnp.concatenate([part for hh in heads
                                   for part in (per_head[hh][0][rows], per_head[hh][2][krows, GDN_DV:])],
                                  axis=0).astype(BF16)
            states = [s_ref[hh] for hh in heads]
            res = jnp.dot(lhs, jnp.concatenate(states, axis=1).astype(BF16), preferred_element_type=F32)
            for j, hh in enumerate(heads):
                _, from_q, from_k, g_last = per_head[hh]
                cols = slice(hh * GDN_DV, (hh + 1) * GDN_DV)
                blk = res[j * (C + GDN_DK):(j + 1) * (C + GDN_DK), j * GDN_DV:(j + 1) * GDN_DV]
                o = blk[:C] + from_q[rows, :GDN_DV]
                s_ref[hh] = states[j] * jnp.exp(g_last[c]) - blk[C:] + from_k[krows, :GDN_DV]
                if reverse:
                    tot = o + ofwd_ref[rows, cols]
                    ms = jnp.mean(tot * tot, axis=-1, keepdims=True)
                    z = z_ref[rows, cols].astype(F32)
                    o_ref[rows, cols] = (tot * lax.rsqrt(ms + RMS_EPS) * nw_ref[...] * _silu(z)).astype(o_ref.dtype)
                else:
                    o_ref[rows, cols] = o


def _gdn_scan(act, g_t, p, o_fwd, norm_w, seq_starts, seq_ends, *, reverse, tt=128, hp=8):
    n = act.shape[0]
    tt = min(tt, n)
    nt = n // tt
    groups = GDN_QK_HEADS // hp
    kw = hp * GDN_DK
    vw = 2 * hp * GDN_DV

    def tmap(t):
        return (nt - 1 - t) if reverse else t

    gate_blocks = (2 * GDN_V_HEADS) // (2 * hp)
    dirn = 1 if reverse else 0
    in_specs = [
        pl.BlockSpec((tt, kw), lambda h, t: (tmap(t), h)),
        pl.BlockSpec((tt, kw), lambda h, t: (tmap(t), groups + h)),
        pl.BlockSpec((tt, vw), lambda h, t: (tmap(t), 2 * GDN_KEY_W // vw + h)),
        pl.BlockSpec((2 * hp, 1, tt), lambda h, t: (dirn * gate_blocks + h, 0, tmap(t))),
        pl.BlockSpec((2 * hp, 1, tt), lambda h, t: (dirn * gate_blocks + groups + h, 0, tmap(t))),
    ]
    args = [act, act, act, g_t, g_t]
    if reverse:
        in_specs += [
            pl.BlockSpec((tt, vw), lambda h, t: (tmap(t), h)),
            pl.BlockSpec((tt, vw), lambda h, t: (tmap(t), P_Z // vw + h)),
            pl.BlockSpec((1, GDN_DV), lambda h, t: (0, 0)),
        ]
        args += [o_fwd, p, norm_w.reshape(1, GDN_DV).astype(F32)]
        out_dtype = BF16
    else:
        out_dtype = F32
    return pl.pallas_call(
        functools.partial(_gdn_scan_body, tt=tt, nt=nt, hp=hp, reverse=reverse,
                          seq_starts=seq_starts, seq_ends=seq_ends),
        out_shape=jax.ShapeDtypeStruct((n, GDN_VAL_W), out_dtype),
        grid=(groups, nt),
        in_specs=in_specs,
        out_specs=pl.BlockSpec((tt, vw), lambda h, t: (tmap(t), h)),
        scratch_shapes=[pltpu.VMEM((2 * hp, GDN_DK, GDN_DV), F32)],
        compiler_params=_params(2),
        name="gdn_scan_bwd" if reverse else "gdn_scan_fwd",
    )(*args)


def _attn_body(slope_ref, lam_ref, sw_ref, q_ref, k_ref, v_ref, o_ref, m_ref, l_ref, acc_ref, sd_ref,
               *, tile, nk, lam_init):
    h = pl.program_id(1)
    qi = pl.program_id(2)
    kj = pl.program_id(3)
    slope = slope_ref[h]

    n_sub = m_ref.shape[1]
    sub = tile // n_sub

    @pl.when(kj == 0)
    def _():
        m_ref[...] = jnp.full_like(m_ref, NEG_BIG)
        l_ref[...] = jnp.zeros_like(l_ref)
        acc_ref[...] = jnp.zeros_like(acc_ref)
        for r in range(n_sub):
            ii = lax.broadcasted_iota(jnp.int32, (sub, tile), 0) + r * sub
            jj = lax.broadcasted_iota(jnp.int32, (sub, tile), 1)
            sd_ref[r] = (ii - jj).astype(F32) * slope

    v = v_ref[...]
    tile_dist = jnp.full((sub, 1), (qi - kj) * tile, jnp.int32).astype(F32) * slope

    def update(add_bias, shift):
        def stages(r, m):
            rows = slice(r * sub, (r + 1) * sub)
            cols = slice(m * DIFF_HD, (m + 1) * DIFF_HD)
            t = add_bias(lax.dot_general(q_ref[rows, cols], k_ref[:, cols], (((1,), (1,)), ((), ())),
                                         preferred_element_type=F32), r)
            yield
            m_old = m_ref[m, r]
            m_new = jnp.maximum(m_old, jnp.max(t, axis=1, keepdims=True) - shift)
            alpha = jnp.exp2(m_old - m_new)
            m_ref[m, r] = m_new
            yield
            p = jnp.exp2(t - (m_new + shift))
            l_ref[m, r] = alpha * l_ref[m, r] + jnp.sum(p, axis=1, keepdims=True)
            p_b = p.astype(BF16)
            yield
            acc_ref[m, r] = alpha * acc_ref[m, r] + jnp.dot(p_b, v, preferred_element_type=F32)
            yield

        waiting = [stages(r, m) for r in range(n_sub) for m in range(2)]
        running = []
        step = 0
        while waiting or running:
            if waiting and step % ATTN_STAGGER == 0:
                running.append(waiting.pop(0))
            running = [g for g in running if next(g, "done") != "done"]
            step += 1

    @pl.when(kj < qi)
    def _():
        update(lambda s, r: s - sd_ref[r], tile_dist)

    @pl.when(kj > qi)
    def _():
        update(lambda s, r: s + sd_ref[r], -tile_dist)

    @pl.when(kj == qi)
    def _():
        update(lambda s, r: s - jnp.abs(sd_ref[r]), jnp.zeros((sub, 1), F32))

    @pl.when(kj == nk - 1)
    def _():
        lam = lam_ref[...]
        lam_full = (jnp.exp(jnp.sum(lam[0:1, :] * lam[1:2, :], axis=-1, keepdims=True))
                    - jnp.exp(jnp.sum(lam[2:3, :] * lam[3:4, :], axis=-1, keepdims=True)) + lam_init)
        for r in range(n_sub):
            o = acc_ref[0, r] / l_ref[0, r] - lam_full * (acc_ref[1, r] / l_ref[1, r])
            ms = jnp.mean(o * o, axis=-1, keepdims=True)
            o_ref[r * sub:(r + 1) * sub, :] = (o * lax.rsqrt(ms + LN_EPS) * sw_ref[...]
                                               * (1.0 - lam_init)).astype(o_ref.dtype)


def _diff_attention(p, lam, subln_w, tok_off, n_seq, seq_len, layer_idx, *, tile=1024, sub=512):
    tile = min(tile, seq_len)
    sub = min(sub, tile)
    n_sub = tile // sub
    nt = seq_len // tile
    lam_init = 0.8 - 0.6 * math.exp(-0.3 * layer_idx)
    slopes = jnp.asarray([LOG2E * 2.0 ** (-8.0 * (i + 1) / DIFF_HEADS) for i in range(DIFF_HEADS)], F32)
    w2 = 2 * DIFF_HD
    off = tok_off // tile
    return pl.pallas_call(
        functools.partial(_attn_body, tile=tile, nk=nt, lam_init=lam_init),
        out_shape=jax.ShapeDtypeStruct((n_seq * seq_len, DIFF_VAL_W), BF16),
        grid=(n_seq, DIFF_HEADS, nt, nt),
        in_specs=[pl.BlockSpec(memory_space=pltpu.SMEM),
                  pl.BlockSpec((4, DIFF_HD), lambda b, h, i, j: (0, 0)),
                  pl.BlockSpec((1, w2), lambda b, h, i, j: (0, 0)),
                  pl.BlockSpec((tile, w2), lambda b, h, i, j: (off + b * nt + i, P_DQ // w2 + h)),
                  pl.BlockSpec((tile, w2), lambda b, h, i, j: (off + b * nt + j, P_DK // w2 + h)),
                  pl.BlockSpec((tile, w2), lambda b, h, i, j: (off + b * nt + j, P_DV // w2 + h))],
        out_specs=pl.BlockSpec((tile, w2), lambda b, h, i, j: (b * nt + i, h)),
        scratch_shapes=[pltpu.VMEM((2, n_sub, sub, 1), F32), pltpu.VMEM((2, n_sub, sub, 1), F32),
                        pltpu.VMEM((2, n_sub, sub, w2), F32), pltpu.VMEM((n_sub, sub, tile), F32)],
        compiler_params=_params(4),
        name="diff_attention",
    )(slopes, lam.astype(F32), subln_w.reshape(1, w2).astype(F32), p, p, p)


def _merge_body(og_ref, od_ref, wg_ref, wd_ref, gg_ref, gd_ref, o_ref):
    a = jnp.dot(og_ref[...], wg_ref[...], preferred_element_type=F32)
    b = jnp.dot(od_ref[...], wd_ref[...], preferred_element_type=F32)
    o_ref[...] = (gg_ref[...].astype(F32) * a + gd_ref[...].astype(F32) * b).astype(o_ref.dtype)


def _merge(o_gdn, o_diff, w_g, w_d, p, *, tm=512, tn=512):
    n = o_gdn.shape[0]
    tm = min(tm, n)
    return pl.pallas_call(
        _merge_body,
        out_shape=jax.ShapeDtypeStruct((n, D_MODEL), BF16),
        grid=(n // tm, D_MODEL // tn),
        in_specs=[pl.BlockSpec((tm, GDN_VAL_W), lambda i, j: (i, 0)),
                  pl.BlockSpec((tm, DIFF_VAL_W), lambda i, j: (i, 0)),
                  pl.BlockSpec((GDN_VAL_W, tn), lambda i, j: (0, j)),
                  pl.BlockSpec((DIFF_VAL_W, tn), lambda i, j: (0, j)),
                  pl.BlockSpec((tm, tn), lambda i, j: (i, P_GATES // tn + j)),
                  pl.BlockSpec((tm, tn), lambda i, j: (i, (P_GATES + D_MODEL) // tn + j))],
        out_specs=pl.BlockSpec((tm, tn), lambda i, j: (i, j)),
        compiler_params=_params(2),
        name="branch_merge",
    )(o_gdn, o_diff, w_g, w_d, p, p)


def _store_slabs(slab_ref, value, n_rows):
    for s in range(SLAB):
        slab_ref[pl.ds(s, n_rows, stride=SLAB), :] = value[:, s * 128:(s + 1) * 128]


def _load_slabs(slab_ref, n_rows, first=0, stride=SLAB):
    return jnp.concatenate([slab_ref[pl.ds(first + s, n_rows, stride=stride), :] for s in range(SLAB)], axis=1)


def _out_ln_body(m_ref, w_ref, x_ref, g_ref, b_ref, o_ref, slab_ref, *, tm):
    mix = jnp.dot(m_ref[...], w_ref[...], preferred_element_type=F32)
    h = _layer_norm(DN_ALPHA * x_ref[...] + mix, g_ref[...], b_ref[...])
    o_ref[...] = h
    _store_slabs(slab_ref, h, tm)


def _out_ln(merged, w_out, x, g, b, *, tm=512):
    n = x.shape[0]
    tm = min(tm, n)
    return pl.pallas_call(
        functools.partial(_out_ln_body, tm=tm),
        out_shape=(jax.ShapeDtypeStruct((n, D_MODEL), F32), jax.ShapeDtypeStruct((n * SLAB, 128), F32)),
        grid=(n // tm,),
        in_specs=[pl.BlockSpec((tm, D_MODEL), lambda i: (i, 0)),
                  pl.BlockSpec((D_MODEL, D_MODEL), lambda i: (0, 0)),
                  pl.BlockSpec((tm, D_MODEL), lambda i: (i, 0)),
                  pl.BlockSpec((1, D_MODEL), lambda i: (0, 0)),
                  pl.BlockSpec((1, D_MODEL), lambda i: (0, 0))],
        out_specs=(pl.BlockSpec((tm, D_MODEL), lambda i: (i, 0)), pl.BlockSpec((tm * SLAB, 128), lambda i: (i, 0))),
        compiler_params=_params(1),
        name="out_proj_ln1",
    )(merged, w_out, x, g.reshape(1, D_MODEL).astype(F32), b.reshape(1, D_MODEL).astype(F32))


def _first_argmax(x, row_ids, n_rows):
    mx = jnp.max(x, axis=0, keepdims=True)
    idx = jnp.min(jnp.where(x == mx, row_ids, n_rows), axis=0, keepdims=True)
    return mx, idx


def _router_body(h_ref, w_ref, bias_ref, e_ref, g_ref, *, n_exp, tm):
    per = n_exp // N_GROUPS
    logits = lax.dot_general(w_ref[...], h_ref[...], (((1,), (1,)), ((), ())),
                             precision=HIGHEST, preferred_element_type=F32)
    scores = _sigmoid(logits)
    choice = scores + bias_ref[...]
    ids_g = lax.broadcasted_iota(jnp.int32, (per, tm), 0)
    group_rows = []
    for g in range(N_GROUPS):
        xg = choice[g * per:(g + 1) * per, :]
        m1, i1 = _first_argmax(xg, ids_g, per)
        m2 = jnp.max(jnp.where(ids_g == i1, -jnp.inf, xg), axis=0, keepdims=True)
        group_rows.append(m1 + m2)
    gs = jnp.concatenate(group_rows, axis=0)
    ids_n = lax.broadcasted_iota(jnp.int32, (N_GROUPS, tm), 0)
    keep = jnp.zeros((N_GROUPS, tm), F32)
    for _ in range(TOPK_GROUPS):
        _, gi = _first_argmax(gs, ids_n, N_GROUPS)
        hit = ids_n == gi
        keep = jnp.where(hit, 1.0, keep)
        gs = jnp.where(hit, -jnp.inf, gs)
    masked = jnp.concatenate(
        [jnp.where(keep[g:g + 1, :] > 0.5, choice[g * per:(g + 1) * per, :], -jnp.inf)
         for g in range(N_GROUPS)], axis=0)
    ids_e = lax.broadcasted_iota(jnp.int32, (n_exp, tm), 0)
    top_idx = []
    top_w = []
    for _ in range(TOP_K):
        _, ei = _first_argmax(masked, ids_e, n_exp)
        hit = ids_e == ei
        top_idx.append(ei)
        top_w.append(jnp.sum(jnp.where(hit, scores, 0.0), axis=0, keepdims=True))
        masked = jnp.where(hit, -jnp.inf, masked)
    gw = jnp.concatenate(top_w, axis=0)
    gw = gw / (jnp.sum(gw, axis=0, keepdims=True) + 1e-20) * ROUTED_SCALE
    e_ref[...] = jnp.concatenate(top_idx, axis=0)
    g_ref[...] = gw


def _router(h, router_w, router_bias, *, tm=256):
    n = h.shape[0]
    n_exp = router_w.shape[1]
    tm = min(tm, n)
    return pl.pallas_call(
        functools.partial(_router_body, n_exp=n_exp, tm=tm),
        out_shape=(jax.ShapeDtypeStruct((TOP_K, n), jnp.int32), jax.ShapeDtypeStruct((TOP_K, n), F32)),
        grid=(n // tm,),
        in_specs=[pl.BlockSpec((tm, D_MODEL), lambda i: (i, 0)),
                  pl.BlockSpec((n_exp, D_MODEL), lambda i: (0, 0)),
                  pl.BlockSpec((n_exp, 1), lambda i: (0, 0))],
        out_specs=(pl.BlockSpec((TOP_K, tm), lambda i: (0, i)), pl.BlockSpec((TOP_K, tm), lambda i: (0, i))),
        compiler_params=_params(1),
        name="moe_router",
    )(h, router_w.T.astype(F32), router_bias.reshape(n_exp, 1).astype(F32))


def _expert_body(be_ref, nu_ref, tok_ref, tok_next_ref, dst_ref, h_ref, wg_ref, wu_ref, wd_ref, y_ref,
                 x_buf, y_buf, wgb_ref, wub_ref, wdb_ref, gather_sem, scatter_sem, *, tb, n_blocks):
    i = pl.program_id(0)
    n_used = nu_ref[0]
    slot = i % 2

    def start_gather(rows_ref, buf):
        def issue(r, carry):
            src = pl.multiple_of(rows_ref[0, 0, r] * SLAB, SLAB)
            pltpu.make_async_copy(h_ref.at[pl.ds(src, SLAB)], x_buf.at[buf, pl.ds(r * SLAB, SLAB)],
                                  gather_sem.at[buf]).start()
            return carry
        lax.fori_loop(0, tb, issue, 0, unroll=8)

    def wait_buffer(buf_ref, sem_ref, buf):
        pltpu.make_async_copy(buf_ref.at[buf], buf_ref.at[buf], sem_ref.at[buf]).wait()

    @pl.when(jnp.logical_and(i == 0, n_used > 0))
    def _():
        start_gather(tok_ref, 0)

    @pl.when(i + 1 < n_used)
    def _():
        start_gather(tok_next_ref, 1 - slot)

    @pl.when(i < n_used)
    def _():
        fresh = jnp.logical_or(i == 0, be_ref[i] != be_ref[jnp.maximum(i - 1, 0)])

        @pl.when(fresh)
        def _():
            wgb_ref[...] = wg_ref[...].astype(BF16)
            wub_ref[...] = wu_ref[...].astype(BF16)
            wdb_ref[...] = wd_ref[...].astype(BF16)

        wait_buffer(x_buf, gather_sem, slot)

        @pl.when(i >= 2)
        def _():
            wait_buffer(y_buf, scatter_sem, slot)

        xb = _load_slabs(x_buf.at[slot], tb).astype(BF16)
        gate = jnp.dot(xb, wgb_ref[...], preferred_element_type=F32)
        up = jnp.dot(xb, wub_ref[...], preferred_element_type=F32)
        act = (_silu(gate) * up).astype(BF16)
        _store_slabs(y_buf.at[slot], jnp.dot(act, wdb_ref[...], preferred_element_type=F32), tb)

        def issue(r, carry):
            dst = pl.multiple_of(dst_ref[0, 0, r] * SLAB, SLAB)
            pltpu.make_async_copy(y_buf.at[slot, pl.ds(r * SLAB, SLAB)], y_ref.at[pl.ds(dst, SLAB)],
                                  scatter_sem.at[slot]).start()
            return carry
        lax.fori_loop(0, tb, issue, 0, unroll=8)

    @pl.when(i == n_blocks - 1)
    def _():
        @pl.when(n_used >= 1)
        def _():
            wait_buffer(y_buf, scatter_sem, (n_used - 1) % 2)

        @pl.when(n_used >= 2)
        def _():
            wait_buffer(y_buf, scatter_sem, n_used % 2)


def _experts(h_slab, slot_tok, slot_dst, block_e, n_used, w_gate, w_up, w_down, *, tb):
    n_slots = slot_tok.shape[0]
    n_blocks = n_slots // tb
    ff = w_gate.shape[2]
    idx_shape = (n_blocks, 1, tb)
    grid_spec = pltpu.PrefetchScalarGridSpec(
        num_scalar_prefetch=2,
        grid=(n_blocks,),
        in_specs=[pl.BlockSpec((1, 1, tb), lambda i, be, nu: (i, 0, 0), memory_space=pltpu.SMEM),
                  pl.BlockSpec((1, 1, tb), lambda i, be, nu: (jnp.minimum(i + 1, n_blocks - 1), 0, 0),
                               memory_space=pltpu.SMEM),
                  pl.BlockSpec((1, 1, tb), lambda i, be, nu: (i, 0, 0), memory_space=pltpu.SMEM),
                  pl.BlockSpec(memory_space=pl.ANY),
                  pl.BlockSpec((None, D_MODEL, ff), lambda i, be, nu: (be[i], 0, 0)),
                  pl.BlockSpec((None, D_MODEL, ff), lambda i, be, nu: (be[i], 0, 0)),
                  pl.BlockSpec((None, ff, D_MODEL), lambda i, be, nu: (be[i], 0, 0))],
        out_specs=pl.BlockSpec(memory_space=pl.ANY),
        scratch_shapes=[pltpu.VMEM((2, tb * SLAB, 128), F32), pltpu.VMEM((2, tb * SLAB, 128), F32),
                        pltpu.VMEM((D_MODEL, ff), BF16), pltpu.VMEM((D_MODEL, ff), BF16),
                        pltpu.VMEM((ff, D_MODEL), BF16),
                        pltpu.SemaphoreType.DMA((2,)), pltpu.SemaphoreType.DMA((2,))],
    )
    tok3 = slot_tok.reshape(idx_shape)
    return pl.pallas_call(
        functools.partial(_expert_body, tb=tb, n_blocks=n_blocks),
        out_shape=jax.ShapeDtypeStruct((n_slots * SLAB, 128), F32),
        grid_spec=grid_spec,
        compiler_params=_params(1),
        name="moe_experts",
    )(block_e, n_used, tok3, tok3, slot_dst.reshape(idx_shape), h_slab, w_gate, w_up, w_down)


def _combine_body(h_ref, y_ref, gw_ref, sgu_ref, sd_ref, g_ref, b_ref, o_ref, *, ff, tm):
    h = h_ref[...]
    hb = h.astype(BF16)
    gu = jnp.dot(hb, sgu_ref[...], preferred_element_type=F32)
    act = (_silu(gu[:, :ff]) * gu[:, ff:]).astype(BF16)
    f = jnp.dot(act, sd_ref[...], preferred_element_type=F32)
    gw = gw_ref[...]
    for k in range(TOP_K):
        f = f + _load_slabs(y_ref, tm, first=k * SLAB, stride=TOP_K * SLAB) * gw[:, k:k + 1]
    o_ref[...] = _layer_norm(DN_ALPHA * h + f, g_ref[...], b_ref[...])


def _combine(h, y_tok, gw_t, sh_gu, sh_down, g, b, tok_off, n_rows, *, tm=128):
    ff = sh_down.shape[0]
    tm = min(tm, n_rows)
    off = tok_off // tm
    return pl.pallas_call(
        functools.partial(_combine_body, ff=ff, tm=tm),
        out_shape=jax.ShapeDtypeStruct((n_rows, D_MODEL), F32),
        grid=(n_rows // tm,),
        in_specs=[pl.BlockSpec((tm, D_MODEL), lambda i: (off + i, 0)),
                  pl.BlockSpec((tm * TOP_K * SLAB, 128), lambda i: (off + i, 0)),
                  pl.BlockSpec((tm, TOP_K), lambda i: (off + i, 0)),
                  pl.BlockSpec((D_MODEL, 2 * ff), lambda i: (0, 0)),
                  pl.BlockSpec((ff, D_MODEL), lambda i: (0, 0)),
                  pl.BlockSpec((1, D_MODEL), lambda i: (0, 0)),
                  pl.BlockSpec((1, D_MODEL), lambda i: (0, 0))],
        out_specs=pl.BlockSpec((tm, D_MODEL), lambda i: (i, 0)),
        compiler_params=_params(1),
        name="moe_combine_ln2",
    )(h, y_tok, gw_t, sh_gu, sh_down, g.reshape(1, D_MODEL).astype(F32), b.reshape(1, D_MODEL).astype(F32))


def _routing_tables(top_e, n_exp, tb):
    k, n = top_e.shape
    a = k * n
    n_blocks = a // tb + n_exp
    a_bits = max(a - 1, 1).bit_length()
    assert n_exp << a_bits < 2 ** 31
    ids = jnp.arange(a, dtype=jnp.int32)
    keys = lax.sort(top_e.reshape(-1) * (1 << a_bits) + ids)
    e_sorted = keys >> a_bits
    a_sorted = keys & ((1 << a_bits) - 1)
    tok_sorted = a_sorted % n
    k_sorted = a_sorted // n
    experts = jnp.arange(n_exp, dtype=jnp.int32)
    start = jnp.sum((e_sorted[:, None] < experts[None, :]).astype(jnp.int32), axis=0)
    counts = jnp.concatenate([start[1:], jnp.full((1,), a, jnp.int32)]) - start
    padded = (counts + tb - 1) // tb * tb
    pad_end = jnp.cumsum(padded)
    shift = pad_end - padded - start
    dest = ids + jnp.sum(jnp.where(e_sorted[:, None] == experts[None, :], shift[None, :], 0), axis=1)
    slot_row = jnp.full((n_blocks * tb,), -1, jnp.int32).at[dest].set(tok_sorted * k + k_sorted)
    is_pad = slot_row < 0
    pad_rank = jnp.cumsum(is_pad.astype(jnp.int32)) - 1
    slot_tok = jnp.where(is_pad, 0, slot_row // k)
    slot_dst = jnp.where(is_pad, a + pad_rank, slot_row)
    block_start = jnp.arange(n_blocks, dtype=jnp.int32) * tb
    block_e = jnp.minimum(jnp.sum(block_start[:, None] >= pad_end[None, :], axis=1), n_exp - 1).astype(jnp.int32)
    n_used = (pad_end[-1] // tb).astype(jnp.int32).reshape(1)
    block_e = jnp.where(block_start < pad_end[-1], block_e, block_e[jnp.maximum(n_used[0] - 1, 0)])
    return slot_tok, slot_dst, block_e, n_used


def _layer(xs, layer_idx, w_in, conv_w, a_log, dt_bias, norm_w, lam, subln_w, w_branch_gdn, w_branch_diff,
           w_out, ln1_g, ln1_b, router_w, router_bias, w_gate, w_up, w_down, sh_gate, sh_up, sh_down,
           ln2_g, ln2_b, *, moe_tb=256):
    seqs = [(x.shape[0], x.shape[1]) for x in xs]
    x = jnp.concatenate([x.reshape(-1, D_MODEL) for x in xs], axis=0)
    n = x.shape[0]
    seq_starts, seq_ends, group_off = [], [], []
    off = 0
    for b, t in seqs:
        group_off.append(off)
        for _ in range(b):
            seq_starts.append(off)
            off += t
            seq_ends.append(off)
    seq_starts, seq_ends = tuple(seq_starts), tuple(seq_ends)

    o_qkv, o_z, o_ab = GDN_CONV_CH, GDN_CONV_CH + GDN_VAL_W, GDN_CONV_CH + GDN_VAL_W + GDN_GATE_W
    w_main = jnp.concatenate([w_in[:, :o_z], w_in[:, o_ab:]], axis=1).astype(BF16)
    w_ab = w_in[:, o_z:o_ab].astype(BF16)

    p = _proj_in(x, w_main)
    gates = _gdn_gates(x, w_ab, a_log, dt_bias)
    g_t = gates.T.reshape(GDN_GATE_W, 1, n)
    act = _gdn_conv(p, conv_w.astype(F32), seq_starts, seq_ends)
    o_fwd = _gdn_scan(act, g_t, p, None, norm_w, seq_starts, seq_ends, reverse=False)
    o_gdn = _gdn_scan(act, g_t, p, o_fwd, norm_w, seq_starts, seq_ends, reverse=True)
    o_diff = jnp.concatenate(
        [_diff_attention(p, lam, subln_w, goff, b, t, layer_idx) for (b, t), goff in zip(seqs, group_off)], axis=0)
    merged = _merge(o_gdn, o_diff, w_branch_gdn.astype(BF16), w_branch_diff.astype(BF16), p)
    h, h_slab = _out_ln(merged, w_out.astype(BF16), x, ln1_g, ln1_b)

    n_exp = router_w.shape[1]
    top_e, gw = _router(h, router_w, router_bias)
    slot_tok, slot_dst, block_e, n_used = _routing_tables(top_e, n_exp, moe_tb)
    y_tok = _experts(h_slab, slot_tok, slot_dst, block_e, n_used, w_gate, w_up, w_down, tb=moe_tb)
    sh_gu = jnp.concatenate([sh_gate, sh_up], axis=1).astype(BF16)
    outs = []
    for (b, t), goff in zip(seqs, group_off):
        y = _combine(h, y_tok, gw.T, sh_gu, sh_down.astype(BF16), ln2_g, ln2_b, goff, b * t)
        outs.append(y.reshape(b, t, D_MODEL))
    return outs


def kernel(x_prompt, x_sample, w_in, gdn_conv_w, gdn_a_log, gdn_dt_bias, gdn_norm_w, diff_lambda, diff_subln_w, w_branch_gdn, w_branch_diff, w_out, ln1_g, ln1_b, router_w, router_bias, exp_w_gate, exp_w_up, exp_w_down, sh_w_gate, sh_w_up, sh_w_down, ln2_g, ln2_b):
    xs = [x_prompt, x_sample]
    for l in range(DEPTH):
        xs = _layer(xs, l, w_in[l], gdn_conv_w[l], gdn_a_log[l], gdn_dt_bias[l], gdn_norm_w[l], diff_lambda[l],
                    diff_subln_w[l], w_branch_gdn[l], w_branch_diff[l], w_out[l], ln1_g[l], ln1_b[l], router_w[l],
                    router_bias[l], exp_w_gate[l], exp_w_up[l], exp_w_down[l], sh_w_gate[l], sh_w_up[l],
                    sh_w_down[l], ln2_g[l], ln2_b[l])
    return (xs[0], xs[1])
```

```python
import functools
import itertools
import math

import jax
import jax.numpy as jnp
from jax import lax
from jax.experimental import pallas as pl
from jax.experimental.pallas import tpu as pltpu

F32 = jnp.float32
BF16 = jnp.bfloat16
HIGHEST = lax.Precision.HIGHEST

D_MODEL = 2048
GDN_QK_HEADS = 16
GDN_V_HEADS = 32
GDN_DK = 128
GDN_DV = 128
GDN_KEY_W = GDN_QK_HEADS * GDN_DK
GDN_VAL_W = GDN_V_HEADS * GDN_DV
GDN_CONV_CH = 2 * GDN_KEY_W + GDN_VAL_W
GDN_GATE_W = 4 * GDN_V_HEADS
GDN_CONV_K = 5
GDN_CHUNK = 64
DIFF_HEADS = 8
DIFF_HD = 128
DIFF_QK_W = 2 * DIFF_HEADS * DIFF_HD
DIFF_VAL_W = DIFF_HEADS * 2 * DIFF_HD
N_GROUPS = 8
TOPK_GROUPS = 4
TOP_K = 8
ROUTED_SCALE = 2.5
DEPTH = 1
DN_ALPHA = (2 * DEPTH) ** 0.25
LN_EPS = 1e-5
RMS_EPS = 1e-6
NEG_BIG = -1e30
LOG2E = math.log2(math.e)
ATTN_STAGGER = 2

P_QKV = 0
P_Z = P_QKV + GDN_CONV_CH
P_DQ = P_Z + GDN_VAL_W
P_DK = P_DQ + DIFF_QK_W
P_DV = P_DK + DIFF_QK_W
P_GATES = P_DV + DIFF_VAL_W
P_COLS = P_GATES + 2 * D_MODEL

VMEM_LIMIT = 56 * 1024 * 1024
SLAB = D_MODEL // 128


def _params(grid_rank, vmem=VMEM_LIMIT):
    return pltpu.CompilerParams(dimension_semantics=("arbitrary",) * grid_rank, vmem_limit_bytes=vmem)


def _sigmoid(x):
    return 1.0 / (1.0 + jnp.exp(-x))


def _silu(x):
    return x * _sigmoid(x)


def _layer_norm(x, g, b):
    mu = jnp.mean(x, axis=-1, keepdims=True)
    xc = x - mu
    var = jnp.mean(xc * xc, axis=-1, keepdims=True)
    return xc * lax.rsqrt(var + LN_EPS) * g + b


def _proj_in_body(x_ref, w_ref, o_ref, xb_ref, *, tn):
    j = pl.program_id(1)

    @pl.when(j == 0)
    def _():
        xb_ref[...] = x_ref[...].astype(BF16)

    acc = jnp.dot(xb_ref[...], w_ref[...], preferred_element_type=F32)
    col = j * tn
    is_gate = col >= P_GATES
    is_dq = jnp.logical_and(col >= P_DQ, col < P_DK)
    scale = jnp.where(is_dq, LOG2E * DIFF_HD ** -0.5, 1.0).astype(F32)

    @pl.when(is_gate)
    def _():
        o_ref[...] = _sigmoid(acc).astype(o_ref.dtype)

    @pl.when(jnp.logical_not(is_gate))
    def _():
        o_ref[...] = (acc * scale).astype(o_ref.dtype)


def _proj_in(x, w, *, tm=1024, tn=1024):
    n, k = x.shape
    m = w.shape[1]
    tm = min(tm, n)
    return pl.pallas_call(
        functools.partial(_proj_in_body, tn=tn),
        out_shape=jax.ShapeDtypeStruct((n, m), BF16),
        grid=(n // tm, m // tn),
        in_specs=[pl.BlockSpec((tm, k), lambda i, j: (i, 0)),
                  pl.BlockSpec((k, tn), lambda i, j: (0, j))],
        out_specs=pl.BlockSpec((tm, tn), lambda i, j: (i, j)),
        scratch_shapes=[pltpu.VMEM((tm, k), BF16)],
        compiler_params=_params(2),
        name="proj_in",
    )(x, w)


def _gates_body(x_ref, w_ref, prm_ref, o_ref, *, tt):
    ab = jnp.dot(x_ref[...].astype(BF16), w_ref[...], preferred_element_type=F32)
    a_log = prm_ref[0:1, :]
    dt_bias = prm_ref[1:2, :]
    y = ab + dt_bias
    softplus = jnp.maximum(y, 0.0) + jnp.log(1.0 + jnp.exp(-jnp.abs(y)))
    log_decay = -jnp.exp(a_log) * softplus
    beta = _sigmoid(ab)
    ri = lax.broadcasted_iota(jnp.int32, (tt, tt), 0)
    ci = lax.broadcasted_iota(jnp.int32, (tt, tt), 1)
    same = (ri // GDN_CHUNK) == (ci // GDN_CHUNK)
    m_lo = jnp.where(same, jnp.where(ci <= ri, 1.0, 0.0), 0.0).astype(F32)
    m_up = jnp.where(same, jnp.where(ci >= ri, 1.0, 0.0), 0.0).astype(F32)
    c_lo = jnp.dot(m_lo, log_decay, precision=HIGHEST, preferred_element_type=F32)
    c_up = jnp.dot(m_up, log_decay, precision=HIGHEST, preferred_element_type=F32)
    lane = lax.broadcasted_iota(jnp.int32, (tt, GDN_GATE_W), 1)
    is_a = (lane % (2 * GDN_V_HEADS)) < GDN_V_HEADS
    is_bwd = lane >= 2 * GDN_V_HEADS
    o_ref[...] = jnp.where(is_a, jnp.where(is_bwd, c_up, c_lo), beta)


def _gdn_gates(x, w_ab, a_log, dt_bias, *, tt=512):
    n, k = x.shape
    tt = min(tt, n)
    zeros = jnp.zeros((GDN_V_HEADS,), F32)
    prm = jnp.zeros((8, GDN_GATE_W), F32)
    prm = prm.at[0].set(jnp.concatenate([a_log[0], zeros, a_log[1], zeros]).astype(F32))
    prm = prm.at[1].set(jnp.concatenate([dt_bias[0], zeros, dt_bias[1], zeros]).astype(F32))
    return pl.pallas_call(
        functools.partial(_gates_body, tt=tt),
        out_shape=jax.ShapeDtypeStruct((n, GDN_GATE_W), F32),
        grid=(n // tt,),
        in_specs=[pl.BlockSpec((tt, k), lambda i: (i, 0)),
                  pl.BlockSpec((k, GDN_GATE_W), lambda i: (0, 0)),
                  pl.BlockSpec((8, GDN_GATE_W), lambda i: (0, 0))],
        out_specs=pl.BlockSpec((tt, GDN_GATE_W), lambda i: (i, 0)),
        compiler_params=_params(1),
        name="gdn_gates",
    )(x, w_ab, prm)


CONV_HALO = 16


def _any_equal(value, constants):
    hit = value == constants[0]
    for c in constants[1:]:
        hit = jnp.logical_or(hit, value == c)
    return hit


def _conv_body(prev_ref, cur_ref, next_ref, w_ref, o_ref, ext_ref, *, tt, tc, seq_starts, seq_ends):
    i = pl.program_id(0)
    j = pl.program_id(1)
    t0 = i * tt
    at_start = _any_equal(t0, seq_starts)
    at_end = _any_equal(t0 + tt, seq_ends)
    prev = prev_ref[...].astype(F32)[CONV_HALO - 8:, :]
    nxt = next_ref[...].astype(F32)[:8, :]
    ext_ref[0:8, :] = jnp.where(at_start, 0.0, prev)
    ext_ref[8:8 + tt, :] = cur_ref[...].astype(F32)
    ext_ref[8 + tt:16 + tt, :] = jnp.where(at_end, 0.0, nxt)
    pad = (GDN_CONV_K - 1) // 2
    acc = jnp.zeros((tt, tc), F32)
    for tap in range(GDN_CONV_K):
        acc = acc + ext_ref[pl.ds(8 - pad + tap, tt), :] * w_ref[tap:tap + 1, :]
    act = _silu(acc)
    col = j * tc
    is_qk = col < 2 * GDN_KEY_W
    q_scale = jnp.where(col < GDN_KEY_W, GDN_DK ** -0.5, 1.0).astype(F32)
    for s in range(tc // GDN_DK):
        a = act[:, s * GDN_DK:(s + 1) * GDN_DK]
        ss = jnp.sum(a * a, axis=-1, keepdims=True)
        normed = a * (lax.rsqrt(ss + RMS_EPS) * q_scale)
        o_ref[:, s * GDN_DK:(s + 1) * GDN_DK] = jnp.where(is_qk, normed, a).astype(o_ref.dtype)


def _gdn_conv(p, conv_w, seq_starts, seq_ends, *, tt=512, tc=512):
    n = p.shape[0]
    tt = min(tt, n)
    hb = tt // CONV_HALO
    n_halo = n // CONV_HALO
    return pl.pallas_call(
        functools.partial(_conv_body, tt=tt, tc=tc, seq_starts=seq_starts, seq_ends=seq_ends),
        out_shape=jax.ShapeDtypeStruct((n, GDN_CONV_CH), BF16),
        grid=(n // tt, GDN_CONV_CH // tc),
        in_specs=[pl.BlockSpec((CONV_HALO, tc), lambda i, j: (jnp.maximum(i * hb - 1, 0), j)),
                  pl.BlockSpec((tt, tc), lambda i, j: (i, j)),
                  pl.BlockSpec((CONV_HALO, tc), lambda i, j: (jnp.minimum((i + 1) * hb, n_halo - 1), j)),
                  pl.BlockSpec((GDN_CONV_K, tc), lambda i, j: (0, j))],
        out_specs=pl.BlockSpec((tt, tc), lambda i, j: (i, j)),
        scratch_shapes=[pltpu.VMEM((tt + 16, tc), F32)],
        compiler_params=_params(2),
        name="gdn_conv",
    )(p, p, p, conv_w)


def _dot_f32(a, b):
    return jnp.dot(a, b, precision=HIGHEST, preferred_element_type=F32)


def _dot_bf16(a, b):
    return jnp.dot(a.astype(BF16), b.astype(BF16), preferred_element_type=F32)


def _gdn_scan_body(*refs, tt, nt, hp, reverse, seq_starts, seq_ends):
    if reverse:
        q_ref, k_ref, v_ref, g_ref, b_ref, ofwd_ref, z_ref, nw_ref, o_ref, s_ref = refs
    else:
        q_ref, k_ref, v_ref, g_ref, b_ref, o_ref, s_ref = refs
    C = GDN_CHUNK
    t = pl.program_id(1)
    tok0 = ((nt - 1 - t) if reverse else t) * tt
    reset = _any_equal(tok0 + tt, seq_ends) if reverse else _any_equal(tok0, seq_starts)

    @pl.when(reset)
    def _():
        s_ref[...] = jnp.zeros_like(s_ref)

    n_chunks = tt // C
    ri = lax.broadcasted_iota(jnp.int32, (tt, tt), 0)
    ci = lax.broadcasted_iota(jnp.int32, (tt, tt), 1)
    same = (ri // C) == (ci // C)
    eye = ri == ci
    causal = jnp.logical_and(same, (ci >= ri) if reverse else (ci <= ri))
    strict = jnp.logical_and(same, (ci > ri) if reverse else (ci < ri))
    eye_f = jnp.where(eye, 1.0, 0.0).astype(F32)
    last = 0 if reverse else C - 1
    order = range(n_chunks - 1, -1, -1) if reverse else range(n_chunks)
    chunk_of_lane = lax.broadcasted_iota(jnp.int32, (1, tt), 1) // C
    chunk_of_lane_k = lax.broadcasted_iota(jnp.int32, (GDN_DK, tt), 1) // C

    shared = []
    for pp in range(hp):
        q = q_ref[:, pp * GDN_DK:(pp + 1) * GDN_DK]
        k = k_ref[:, pp * GDN_DK:(pp + 1) * GDN_DK]
        kf = k.astype(F32)
        kt = kf.T
        ktb = kt.astype(BF16)
        kq = jnp.dot(jnp.concatenate([k, q], axis=0), ktb, preferred_element_type=F32)
        shared.append((kf, q.astype(F32), kt, kq[:tt], kq[tt:]))
    per_head = [None] * (2 * hp)

    def head_stages(hh):
        kf, qf, kt, kk, qk = shared[hh // 2]
        cols = slice(hh * GDN_DV, (hh + 1) * GDN_DV)
        g_row = g_ref[hh]
        b_row = b_ref[hh]
        g_col = jnp.sum(jnp.where(eye, g_row, 0.0), axis=1, keepdims=True)
        b_col = jnp.sum(jnp.where(eye, b_row, 0.0), axis=1, keepdims=True)
        g_last = [g_row[:, c * C + last:c * C + last + 1] for c in range(n_chunks)]
        g_last_row = jnp.zeros((1, tt), F32)
        for c in range(n_chunks):
            g_last_row = jnp.where(chunk_of_lane == c, g_last[c], g_last_row)
        decay = jnp.exp(jnp.where(causal, g_col - g_row, NEG_BIG))
        low = jnp.where(strict, kk * decay, 0.0) * b_col
        low_b = low.astype(BF16)
        pw = jnp.dot(low_b, low_b, preferred_element_type=F32)
        yield
        inv = eye_f - low
        n_factors = int(math.log2(C)) - 1
        for j in range(n_factors):
            pw_b = pw.astype(BF16)
            if j < n_factors - 1:
                both = jnp.dot(jnp.concatenate([inv, pw], axis=0).astype(BF16), pw_b, preferred_element_type=F32)
                inv = inv + both[:tt]
                pw = both[tt:]
            else:
                inv = inv + jnp.dot(inv.astype(BF16), pw_b, preferred_element_type=F32)
            yield
        eg_col = jnp.exp(g_col)
        v = v_ref[:, cols].astype(F32)
        x = jnp.concatenate([v * b_col, kf * (b_col * eg_col)], axis=1)
        uw = jnp.dot(inv.astype(BF16), x.astype(BF16), preferred_element_type=F32).astype(BF16)
        yield
        from_q = jnp.dot((qk * decay).astype(BF16), uw, preferred_element_type=F32)
        yield
        ktd = kt * jnp.exp(g_last_row - g_row)
        lhs_k = jnp.concatenate([jnp.where(chunk_of_lane_k == c, ktd, 0.0) for c in range(n_chunks)],
                                axis=0).astype(BF16)
        from_k = jnp.dot(lhs_k, uw, preferred_element_type=F32)
        per_head[hh] = (qf * eg_col - from_q[:, GDN_DV:], from_q, from_k, g_last)
        yield

    for _ in itertools.zip_longest(*[head_stages(hh) for hh in range(2 * hp)]):
        pass

    for c in order:
        rows = slice(c * C, (c + 1) * C)
        krows = slice(c * GDN_DK, (c + 1) * GDN_DK)
        for pp in range(hp):
            heads = (2 * pp, 2 * pp + 1)
            lhs = jnp.concatenate([part for hh in heads
                                   for part in (per_head[hh][0][rows], per_head[hh][2][krows, GDN_DV:])],
                                  axis=0).astype(BF16)
            states = [s_ref[hh] for hh in heads]
            res = jnp.dot(lhs, jnp.concatenate(states, axis=1).astype(BF16), preferred_element_type=F32)
            for j, hh in enumerate(heads):
                _, from_q, from_k, g_last = per_head[hh]
                cols = slice(hh * GDN_DV, (hh + 1) * GDN_DV)
                blk = res[j * (C + GDN_DK):(j + 1) * (C + GDN_DK), j * GDN_DV:(j + 1) * GDN_DV]
                o = blk[:C] + from_q[rows, :GDN_DV]
                s_ref[hh] = states[j] * jnp.exp(g_last[c]) - blk[C:] + from_k[krows, :GDN_DV]
                if reverse:
                    tot = o + ofwd_ref[rows, cols]
                    ms = jnp.mean(tot * tot, axis=-1, keepdims=True)
                    z = z_ref[rows, cols].astype(F32)
                    o_ref[rows, cols] = (tot * lax.rsqrt(ms + RMS_EPS) * nw_ref[...] * _silu(z)).astype(o_ref.dtype)
                else:
                    o_ref[rows, cols] = o


def _gdn_scan(act, g_t, p, o_fwd, norm_w, seq_starts, seq_ends, *, reverse, tt=128, hp=8):
    n = act.shape[0]
    tt = min(tt, n)
    nt = n // tt
    groups = GDN_QK_HEADS // hp
    kw = hp * GDN_DK
    vw = 2 * hp * GDN_DV

    def tmap(t):
        return (nt - 1 - t) if reverse else t

    gate_blocks = (2 * GDN_V_HEADS) // (2 * hp)
    dirn = 1 if reverse else 0
    in_specs = [
        pl.BlockSpec((tt, kw), lambda h, t: (tmap(t), h)),
        pl.BlockSpec((tt, kw), lambda h, t: (tmap(t), groups + h)),
        pl.BlockSpec((tt, vw), lambda h, t: (tmap(t), 2 * GDN_KEY_W // vw + h)),
        pl.BlockSpec((2 * hp, 1, tt), lambda h, t: (dirn * gate_blocks + h, 0, tmap(t))),
        pl.BlockSpec((2 * hp, 1, tt), lambda h, t: (dirn * gate_blocks + groups + h, 0, tmap(t))),
    ]
    args = [act, act, act, g_t, g_t]
    if reverse:
        in_specs += [
            pl.BlockSpec((tt, vw), lambda h, t: (tmap(t), h)),
            pl.BlockSpec((tt, vw), lambda h, t: (tmap(t), P_Z // vw + h)),
            pl.BlockSpec((1, GDN_DV), lambda h, t: (0, 0)),
        ]
        args += [o_fwd, p, norm_w.reshape(1, GDN_DV).astype(F32)]
        out_dtype = BF16
    else:
        out_dtype = F32
    return pl.pallas_call(
        functools.partial(_gdn_scan_body, tt=tt, nt=nt, hp=hp, reverse=reverse,
                          seq_starts=seq_starts, seq_ends=seq_ends),
        out_shape=jax.ShapeDtypeStruct((n, GDN_VAL_W), out_dtype),
        grid=(groups, nt),
        in_specs=in_specs,
        out_specs=pl.BlockSpec((tt, vw), lambda h, t: (tmap(t), h)),
        scratch_shapes=[pltpu.VMEM((2 * hp, GDN_DK, GDN_DV), F32)],
        compiler_params=_params(2),
        name="gdn_scan_bwd" if reverse else "gdn_scan_fwd",
    )(*args)


def _attn_body(slope_ref, lam_ref, sw_ref, q_ref, k_ref, v_ref, o_ref, m_ref, l_ref, acc_ref, sd_ref,
               *, tile, nk, lam_init):
    h = pl.program_id(1)
    qi = pl.program_id(2)
    kj = pl.program_id(3)
    slope = slope_ref[h]

    n_sub = m_ref.shape[1]
    sub = tile // n_sub

    @pl.when(kj == 0)
    def _():
        m_ref[...] = jnp.full_like(m_ref, NEG_BIG)
        l_ref[...] = jnp.zeros_like(l_ref)
        acc_ref[...] = jnp.zeros_like(acc_ref)
        for r in range(n_sub):
            ii = lax.broadcasted_iota(jnp.int32, (sub, tile), 0) + r * sub
            jj = lax.broadcasted_iota(jnp.int32, (sub, tile), 1)
            sd_ref[r] = (ii - jj).astype(F32) * slope

    v = v_ref[...]
    tile_dist = jnp.full((sub, 1), (qi - kj) * tile, jnp.int32).astype(F32) * slope

    def update(add_bias, shift):
        def stages(r, m):
            rows = slice(r * sub, (r + 1) * sub)
            cols = slice(m * DIFF_HD, (m + 1) * DIFF_HD)
            t = add_bias(lax.dot_general(q_ref[rows, cols], k_ref[:, cols], (((1,), (1,)), ((), ())),
                                         preferred_element_type=F32), r)
            yield
            m_old = m_ref[m, r]
            m_new = jnp.maximum(m_old, jnp.max(t, axis=1, keepdims=True) - shift)
            alpha = jnp.exp2(m_old - m_new)
            m_ref[m, r] = m_new
            yield
            p = jnp.exp2(t - (m_new + shift))
            l_ref[m, r] = alpha * l_ref[m, r] + jnp.sum(p, axis=1, keepdims=True)
            p_b = p.astype(BF16)
            yield
            acc_ref[m, r] = alpha * acc_ref[m, r] + jnp.dot(p_b, v, preferred_element_type=F32)
            yield

        waiting = [stages(r, m) for r in range(n_sub) for m in range(2)]
        running = []
        step = 0
        while waiting or running:
            if waiting and step % ATTN_STAGGER == 0:
                running.append(waiting.pop(0))
            running = [g for g in running if next(g, "done") != "done"]
            step += 1

    @pl.when(kj < qi)
    def _():
        update(lambda s, r: s - sd_ref[r], tile_dist)

    @pl.when(kj > qi)
    def _():
        update(lambda s, r: s + sd_ref[r], -tile_dist)

    @pl.when(kj == qi)
    def _():
        update(lambda s, r: s - jnp.abs(sd_ref[r]), jnp.zeros((sub, 1), F32))

    @pl.when(kj == nk - 1)
    def _():
        lam = lam_ref[...]
        lam_full = (jnp.exp(jnp.sum(lam[0:1, :] * lam[1:2, :], axis=-1, keepdims=True))
                    - jnp.exp(jnp.sum(lam[2:3, :] * lam[3:4, :], axis=-1, keepdims=True)) + lam_init)
        for r in range(n_sub):
            o = acc_ref[0, r] / l_ref[0, r] - lam_full * (acc_ref[1, r] / l_ref[1, r])
            ms = jnp.mean(o * o, axis=-1, keepdims=True)
            o_ref[r * sub:(r + 1) * sub, :] = (o * lax.rsqrt(ms + LN_EPS) * sw_ref[...]
                                               * (1.0 - lam_init)).astype(o_ref.dtype)


def _diff_attention(p, lam, subln_w, tok_off, n_seq, seq_len, layer_idx, *, tile=1024, sub=512):
    tile = min(tile, seq_len)
    sub = min(sub, tile)
    n_sub = tile // sub
    nt = seq_len // tile
    lam_init = 0.8 - 0.6 * math.exp(-0.3 * layer_idx)
    slopes = jnp.asarray([LOG2E * 2.0 ** (-8.0 * (i + 1) / DIFF_HEADS) for i in range(DIFF_HEADS)], F32)
    w2 = 2 * DIFF_HD
    off = tok_off // tile
    return pl.pallas_call(
        functools.partial(_attn_body, tile=tile, nk=nt, lam_init=lam_init),
        out_shape=jax.ShapeDtypeStruct((n_seq * seq_len, DIFF_VAL_W), BF16),
        grid=(n_seq, DIFF_HEADS, nt, nt),
        in_specs=[pl.BlockSpec(memory_space=pltpu.SMEM),
                  pl.BlockSpec((4, DIFF_HD), lambda b, h, i, j: (0, 0)),
                  pl.BlockSpec((1, w2), lambda b, h, i, j: (0, 0)),
                  pl.BlockSpec((tile, w2), lambda b, h, i, j: (off + b * nt + i, P_DQ // w2 + h)),
                  pl.BlockSpec((tile, w2), lambda b, h, i, j: (off + b * nt + j, P_DK // w2 + h)),
                  pl.BlockSpec((tile, w2), lambda b, h, i, j: (off + b * nt + j, P_DV // w2 + h))],
        out_specs=pl.BlockSpec((tile, w2), lambda b, h, i, j: (b * nt + i, h)),
        scratch_shapes=[pltpu.VMEM((2, n_sub, sub, 1), F32), pltpu.VMEM((2, n_sub, sub, 1), F32),
                        pltpu.VMEM((2, n_sub, sub, w2), F32), pltpu.VMEM((n_sub, sub, tile), F32)],
        compiler_params=_params(4),
        name="diff_attention",
    )(slopes, lam.astype(F32), subln_w.reshape(1, w2).astype(F32), p, p, p)


def _merge_body(og_ref, od_ref, wg_ref, wd_ref, gg_ref, gd_ref, o_ref):
    a = jnp.dot(og_ref[...], wg_ref[...], preferred_element_type=F32)
    b = jnp.dot(od_ref[...], wd_ref[...], preferred_element_type=F32)
    o_ref[...] = (gg_ref[...].astype(F32) * a + gd_ref[...].astype(F32) * b).astype(o_ref.dtype)


def _merge(o_gdn, o_diff, w_g, w_d, p, *, tm=512, tn=512):
    n = o_gdn.shape[0]
    tm = min(tm, n)
    return pl.pallas_call(
        _merge_body,
        out_shape=jax.ShapeDtypeStruct((n, D_MODEL), BF16),
        grid=(n // tm, D_MODEL // tn),
        in_specs=[pl.BlockSpec((tm, GDN_VAL_W), lambda i, j: (i, 0)),
                  pl.BlockSpec((tm, DIFF_VAL_W), lambda i, j: (i, 0)),
                  pl.BlockSpec((GDN_VAL_W, tn), lambda i, j: (0, j)),
                  pl.BlockSpec((DIFF_VAL_W, tn), lambda i, j: (0, j)),
                  pl.BlockSpec((tm, tn), lambda i, j: (i, P_GATES // tn + j)),
                  pl.BlockSpec((tm, tn), lambda i, j: (i, (P_GATES + D_MODEL) // tn + j))],
        out_specs=pl.BlockSpec((tm, tn), lambda i, j: (i, j)),
        compiler_params=_params(2),
        name="branch_merge",
    )(o_gdn, o_diff, w_g, w_d, p, p)


def _store_slabs(slab_ref, value, n_rows):
    for s in range(SLAB):
        slab_ref[pl.ds(s, n_rows, stride=SLAB), :] = value[:, s * 128:(s + 1) * 128]


def _load_slabs(slab_ref, n_rows, first=0, stride=SLAB):
    return jnp.concatenate([slab_ref[pl.ds(first + s, n_rows, stride=stride), :] for s in range(SLAB)], axis=1)


def _out_ln_body(m_ref, w_ref, x_ref, g_ref, b_ref, o_ref, slab_ref, *, tm):
    mix = jnp.dot(m_ref[...], w_ref[...], preferred_element_type=F32)
    h = _layer_norm(DN_ALPHA * x_ref[...] + mix, g_ref[...], b_ref[...])
    o_ref[...] = h
    _store_slabs(slab_ref, h, tm)


def _out_ln(merged, w_out, x, g, b, *, tm=512):
    n = x.shape[0]
    tm = min(tm, n)
    return pl.pallas_call(
        functools.partial(_out_ln_body, tm=tm),
        out_shape=(jax.ShapeDtypeStruct((n, D_MODEL), F32), jax.ShapeDtypeStruct((n * SLAB, 128), F32)),
        grid=(n // tm,),
        in_specs=[pl.BlockSpec((tm, D_MODEL), lambda i: (i, 0)),
                  pl.BlockSpec((D_MODEL, D_MODEL), lambda i: (0, 0)),
                  pl.BlockSpec((tm, D_MODEL), lambda i: (i, 0)),
                  pl.BlockSpec((1, D_MODEL), lambda i: (0, 0)),
                  pl.BlockSpec((1, D_MODEL), lambda i: (0, 0))],
        out_specs=(pl.BlockSpec((tm, D_MODEL), lambda i: (i, 0)), pl.BlockSpec((tm * SLAB, 128), lambda i: (i, 0))),
        compiler_params=_params(1),
        name="out_proj_ln1",
    )(merged, w_out, x, g.reshape(1, D_MODEL).astype(F32), b.reshape(1, D_MODEL).astype(F32))


def _first_argmax(x, row_ids, n_rows):
    mx = jnp.max(x, axis=0, keepdims=True)
    idx = jnp.min(jnp.where(x == mx, row_ids, n_rows), axis=0, keepdims=True)
    return mx, idx


def _router_body(h_ref, w_ref, bias_ref, e_ref, g_ref, *, n_exp, tm):
    per = n_exp // N_GROUPS
    logits = lax.dot_general(w_ref[...], h_ref[...], (((1,), (1,)), ((), ())),
                             precision=HIGHEST, preferred_element_type=F32)
    scores = _sigmoid(logits)
    choice = scores + bias_ref[...]
    ids_g = lax.broadcasted_iota(jnp.int32, (per, tm), 0)
    group_rows = []
    for g in range(N_GROUPS):
        xg = choice[g * per:(g + 1) * per, :]
        m1, i1 = _first_argmax(xg, ids_g, per)
        m2 = jnp.max(jnp.where(ids_g == i1, -jnp.inf, xg), axis=0, keepdims=True)
        group_rows.append(m1 + m2)
    gs = jnp.concatenate(group_rows, axis=0)
    ids_n = lax.broadcasted_iota(jnp.int32, (N_GROUPS, tm), 0)
    keep = jnp.zeros((N_GROUPS, tm), F32)
    for _ in range(TOPK_GROUPS):
        _, gi = _first_argmax(gs, ids_n, N_GROUPS)
        hit = ids_n == gi
        keep = jnp.where(hit, 1.0, keep)
        gs = jnp.where(hit, -jnp.inf, gs)
    masked = jnp.concatenate(
        [jnp.where(keep[g:g + 1, :] > 0.5, choice[g * per:(g + 1) * per, :], -jnp.inf)
         for g in range(N_GROUPS)], axis=0)
    ids_e = lax.broadcasted_iota(jnp.int32, (n_exp, tm), 0)
    top_idx = []
    top_w = []
    for _ in range(TOP_K):
        _, ei = _first_argmax(masked, ids_e, n_exp)
        hit = ids_e == ei
        top_idx.append(ei)
        top_w.append(jnp.sum(jnp.where(hit, scores, 0.0), axis=0, keepdims=True))
        masked = jnp.where(hit, -jnp.inf, masked)
    gw = jnp.concatenate(top_w, axis=0)
    gw = gw / (jnp.sum(gw, axis=0, keepdims=True) + 1e-20) * ROUTED_SCALE
    e_ref[...] = jnp.concatenate(top_idx, axis=0)
    g_ref[...] = gw


def _router(h, router_w, router_bias, *, tm=256):
    n = h.shape[0]
    n_exp = router_w.shape[1]
    tm = min(tm, n)
    return pl.pallas_call(
        functools.partial(_router_body, n_exp=n_exp, tm=tm),
        out_shape=(jax.ShapeDtypeStruct((TOP_K, n), jnp.int32), jax.ShapeDtypeStruct((TOP_K, n), F32)),
        grid=(n // tm,),
        in_specs=[pl.BlockSpec((tm, D_MODEL), lambda i: (i, 0)),
                  pl.BlockSpec((n_exp, D_MODEL), lambda i: (0, 0)),
                  pl.BlockSpec((n_exp, 1), lambda i: (0, 0))],
        out_specs=(pl.BlockSpec((TOP_K, tm), lambda i: (0, i)), pl.BlockSpec((TOP_K, tm), lambda i: (0, i))),
        compiler_params=_params(1),
        name="moe_router",
    )(h, router_w.T.astype(F32), router_bias.reshape(n_exp, 1).astype(F32))


def _expert_body(be_ref, nu_ref, tok_ref, tok_next_ref, dst_ref, h_ref, wg_ref, wu_ref, wd_ref, y_ref,
                 x_buf, y_buf, wgb_ref, wub_ref, wdb_ref, gather_sem, scatter_sem, *, tb, n_blocks):
    i = pl.program_id(0)
    n_used = nu_ref[0]
    slot = i % 2

    def start_gather(rows_ref, buf):
        def issue(r, carry):
            src = pl.multiple_of(rows_ref[0, 0, r] * SLAB, SLAB)
            pltpu.make_async_copy(h_ref.at[pl.ds(src, SLAB)], x_buf.at[buf, pl.ds(r * SLAB, SLAB)],
                                  gather_sem.at[buf]).start()
            return carry
        lax.fori_loop(0, tb, issue, 0, unroll=8)

    def wait_buffer(buf_ref, sem_ref, buf):
        pltpu.make_async_copy(buf_ref.at[buf], buf_ref.at[buf], sem_ref.at[buf]).wait()

    @pl.when(jnp.logical_and(i == 0, n_used > 0))
    def _():
        start_gather(tok_ref, 0)

    @pl.when(i + 1 < n_used)
    def _():
        start_gather(tok_next_ref, 1 - slot)

    @pl.when(i < n_used)
    def _():
        fresh = jnp.logical_or(i == 0, be_ref[i] != be_ref[jnp.maximum(i - 1, 0)])

        @pl.when(fresh)
        def _():
            wgb_ref[...] = wg_ref[...].astype(BF16)
            wub_ref[...] = wu_ref[...].astype(BF16)
            wdb_ref[...] = wd_ref[...].astype(BF16)

        wait_buffer(x_buf, gather_sem, slot)

        @pl.when(i >= 2)
        def _():
            wait_buffer(y_buf, scatter_sem, slot)

        xb = _load_slabs(x_buf.at[slot], tb).astype(BF16)
        gate = jnp.dot(xb, wgb_ref[...], preferred_element_type=F32)
        up = jnp.dot(xb, wub_ref[...], preferred_element_type=F32)
        act = (_silu(gate) * up).astype(BF16)
        _store_slabs(y_buf.at[slot], jnp.dot(act, wdb_ref[...], preferred_element_type=F32), tb)

        def issue(r, carry):
            dst = pl.multiple_of(dst_ref[0, 0, r] * SLAB, SLAB)
            pltpu.make_async_copy(y_buf.at[slot, pl.ds(r * SLAB, SLAB)], y_ref.at[pl.ds(dst, SLAB)],
                                  scatter_sem.at[slot]).start()
            return carry
        lax.fori_loop(0, tb, issue, 0, unroll=8)

    @pl.when(i == n_blocks - 1)
    def _():
        @pl.when(n_used >= 1)
        def _():
            wait_buffer(y_buf, scatter_sem, (n_used - 1) % 2)

        @pl.when(n_used >= 2)
        def _():
            wait_buffer(y_buf, scatter_sem, n_used % 2)


def _experts(h_slab, slot_tok, slot_dst, block_e, n_used, w_gate, w_up, w_down, *, tb):
    n_slots = slot_tok.shape[0]
    n_blocks = n_slots // tb
    ff = w_gate.shape[2]
    idx_shape = (n_blocks, 1, tb)
    grid_spec = pltpu.PrefetchScalarGridSpec(
        num_scalar_prefetch=2,
        grid=(n_blocks,),
        in_specs=[pl.BlockSpec((1, 1, tb), lambda i, be, nu: (i, 0, 0), memory_space=pltpu.SMEM),
                  pl.BlockSpec((1, 1, tb), lambda i, be, nu: (jnp.minimum(i + 1, n_blocks - 1), 0, 0),
                               memory_space=pltpu.SMEM),
                  pl.BlockSpec((1, 1, tb), lambda i, be, nu: (i, 0, 0), memory_space=pltpu.SMEM),
                  pl.BlockSpec(memory_space=pl.ANY),
                  pl.BlockSpec((None, D_MODEL, ff), lambda i, be, nu: (be[i], 0, 0)),
                  pl.BlockSpec((None, D_MODEL, ff), lambda i, be, nu: (be[i], 0, 0)),
                  pl.BlockSpec((None, ff, D_MODEL), lambda i, be, nu: (be[i], 0, 0))],
        out_specs=pl.BlockSpec(memory_space=pl.ANY),
        scratch_shapes=[pltpu.VMEM((2, tb * SLAB, 128), F32), pltpu.VMEM((2, tb * SLAB, 128), F32),
                        pltpu.VMEM((D_MODEL, ff), BF16), pltpu.VMEM((D_MODEL, ff), BF16),
                        pltpu.VMEM((ff, D_MODEL), BF16),
                        pltpu.SemaphoreType.DMA((2,)), pltpu.SemaphoreType.DMA((2,))],
    )
    tok3 = slot_tok.reshape(idx_shape)
    return pl.pallas_call(
        functools.partial(_expert_body, tb=tb, n_blocks=n_blocks),
        out_shape=jax.ShapeDtypeStruct((n_slots * SLAB, 128), F32),
        grid_spec=grid_spec,
        compiler_params=_params(1),
        name="moe_experts",
    )(block_e, n_used, tok3, tok3, slot_dst.reshape(idx_shape), h_slab, w_gate, w_up, w_down)


def _combine_body(h_ref, y_ref, gw_ref, sgu_ref, sd_ref, g_ref, b_ref, o_ref, *, ff, tm):
    h = h_ref[...]
    hb = h.astype(BF16)
    gu = jnp.dot(hb, sgu_ref[...], preferred_element_type=F32)
    act = (_silu(gu[:, :ff]) * gu[:, ff:]).astype(BF16)
    f = jnp.dot(act, sd_ref[...], preferred_element_type=F32)
    gw = gw_ref[...]
    for k in range(TOP_K):
        f = f + _load_slabs(y_ref, tm, first=k * SLAB, stride=TOP_K * SLAB) * gw[:, k:k + 1]
    o_ref[...] = _layer_norm(DN_ALPHA * h + f, g_ref[...], b_ref[...])


def _combine(h, y_tok, gw_t, sh_gu, sh_down, g, b, tok_off, n_rows, *, tm=128):
    ff = sh_down.shape[0]
    tm = min(tm, n_rows)
    off = tok_off // tm
    return pl.pallas_call(
        functools.partial(_combine_body, ff=ff, tm=tm),
        out_shape=jax.ShapeDtypeStruct((n_rows, D_MODEL), F32),
        grid=(n_rows // tm,),
        in_specs=[pl.BlockSpec((tm, D_MODEL), lambda i: (off + i, 0)),
                  pl.BlockSpec((tm * TOP_K * SLAB, 128), lambda i: (off + i, 0)),
                  pl.BlockSpec((tm, TOP_K), lambda i: (off + i, 0)),
                  pl.BlockSpec((D_MODEL, 2 * ff), lambda i: (0, 0)),
                  pl.BlockSpec((ff, D_MODEL), lambda i: (0, 0)),
                  pl.BlockSpec((1, D_MODEL), lambda i: (0, 0)),
                  pl.BlockSpec((1, D_MODEL), lambda i: (0, 0))],
        out_specs=pl.BlockSpec((tm, D_MODEL), lambda i: (i, 0)),
        compiler_params=_params(1),
        name="moe_combine_ln2",
    )(h, y_tok, gw_t, sh_gu, sh_down, g.reshape(1, D_MODEL).astype(F32), b.reshape(1, D_MODEL).astype(F32))


def _routing_tables(top_e, n_exp, tb):
    k, n = top_e.shape
    a = k * n
    n_blocks = a // tb + n_exp
    a_bits = max(a - 1, 1).bit_length()
    assert n_exp << a_bits < 2 ** 31
    ids = jnp.arange(a, dtype=jnp.int32)
    keys = lax.sort(top_e.reshape(-1) * (1 << a_bits) + ids)
    e_sorted = keys >> a_bits
    a_sorted = keys & ((1 << a_bits) - 1)
    tok_sorted = a_sorted % n
    k_sorted = a_sorted // n
    experts = jnp.arange(n_exp, dtype=jnp.int32)
    start = jnp.sum((e_sorted[:, None] < experts[None, :]).astype(jnp.int32), axis=0)
    counts = jnp.concatenate([start[1:], jnp.full((1,), a, jnp.int32)]) - start
    padded = (counts + tb - 1) // tb * tb
    pad_end = jnp.cumsum(padded)
    pad_start = pad_end - padded
    block_start = jnp.arange(n_blocks, dtype=jnp.int32) * tb
    block_e = jnp.minimum(jnp.sum(block_start[:, None] >= pad_end[None, :], axis=1), n_exp - 1).astype(jnp.int32)
    used = block_start < pad_end[-1]
    n_used = (pad_end[-1] // tb).astype(jnp.int32).reshape(1)
    first = start[block_e] + block_start - pad_start[block_e]
    n_real = jnp.where(used, jnp.clip(start[block_e] + counts[block_e] - first, 0, tb), 0)
    row_sorted = jnp.concatenate([tok_sorted * k + k_sorted, jnp.zeros((tb,), jnp.int32)])
    rows = jax.vmap(lambda f: lax.dynamic_slice(row_sorted, (f,), (tb,)))(jnp.where(used, first, a))
    is_pad = (jnp.arange(tb, dtype=jnp.int32)[None, :] >= n_real[:, None]).reshape(-1)
    slot_row = rows.reshape(-1)
    pad_rank = jnp.cumsum(is_pad.astype(jnp.int32)) - 1
    slot_tok = jnp.where(is_pad, 0, slot_row // k)
    slot_dst = jnp.where(is_pad, a + pad_rank, slot_row)
    block_e = jnp.where(used, block_e, block_e[jnp.maximum(n_used[0] - 1, 0)])
    return slot_tok, slot_dst, block_e, n_used


def _layer(xs, layer_idx, w_in, conv_w, a_log, dt_bias, norm_w, lam, subln_w, w_branch_gdn, w_branch_diff,
           w_out, ln1_g, ln1_b, router_w, router_bias, w_gate, w_up, w_down, sh_gate, sh_up, sh_down,
           ln2_g, ln2_b, *, moe_tb=256):
    seqs = [(x.shape[0], x.shape[1]) for x in xs]
    x = jnp.concatenate([x.reshape(-1, D_MODEL) for x in xs], axis=0)
    n = x.shape[0]
    seq_starts, seq_ends, group_off = [], [], []
    off = 0
    for b, t in seqs:
        group_off.append(off)
        for _ in range(b):
            seq_starts.append(off)
            off += t
            seq_ends.append(off)
    seq_starts, seq_ends = tuple(seq_starts), tuple(seq_ends)

    o_qkv, o_z, o_ab = GDN_CONV_CH, GDN_CONV_CH + GDN_VAL_W, GDN_CONV_CH + GDN_VAL_W + GDN_GATE_W
    w_main = jnp.concatenate([w_in[:, :o_z], w_in[:, o_ab:]], axis=1).astype(BF16)
    w_ab = w_in[:, o_z:o_ab].astype(BF16)

    p = _proj_in(x, w_main)
    gates = _gdn_gates(x, w_ab, a_log, dt_bias)
    g_t = gates.T.reshape(GDN_GATE_W, 1, n)
    act = _gdn_conv(p, conv_w.astype(F32), seq_starts, seq_ends)
    o_fwd = _gdn_scan(act, g_t, p, None, norm_w, seq_starts, seq_ends, reverse=False)
    o_gdn = _gdn_scan(act, g_t, p, o_fwd, norm_w, seq_starts, seq_ends, reverse=True)
    o_diff = jnp.concatenate(
        [_diff_attention(p, lam, subln_w, goff, b, t, layer_idx) for (b, t), goff in zip(seqs, group_off)], axis=0)
    merged = _merge(o_gdn, o_diff, w_branch_gdn.astype(BF16), w_branch_diff.astype(BF16), p)
    h, h_slab = _out_ln(merged, w_out.astype(BF16), x, ln1_g, ln1_b)

    n_exp = router_w.shape[1]
    top_e, gw = _router(h, router_w, router_bias)
    slot_tok, slot_dst, block_e, n_used = _routing_tables(top_e, n_exp, moe_tb)
    y_tok = _experts(h_slab, slot_tok, slot_dst, block_e, n_used, w_gate, w_up, w_down, tb=moe_tb)
    sh_gu = jnp.concatenate([sh_gate, sh_up], axis=1).astype(BF16)
    outs = []
    for (b, t), goff in zip(seqs, group_off):
        y = _combine(h, y_tok, gw.T, sh_gu, sh_down.astype(BF16), ln2_g, ln2_b, goff, b * t)
        outs.append(y.reshape(b, t, D_MODEL))
    return outs


def kernel(x_prompt, x_sample, w_in, gdn_conv_w, gdn_a_log, gdn_dt_bias, gdn_norm_w, diff_lambda, diff_subln_w, w_branch_gdn, w_branch_diff, w_out, ln1_g, ln1_b, router_w, router_bias, exp_w_gate, exp_w_up, exp_w_down, sh_w_gate, sh_w_up, sh_w_down, ln2_g, ln2_b):
    xs = [x_prompt, x_sample]
    for l in range(DEPTH):
        xs = _layer(xs, l, w_in[l], gdn_conv_w[l], gdn_a_log[l], gdn_dt_bias[l], gdn_norm_w[l], diff_lambda[l],
                    diff_subln_w[l], w_branch_gdn[l], w_branch_diff[l], w_out[l], ln1_g[l], ln1_b[l], router_w[l],
                    router_bias[l], exp_w_gate[l], exp_w_up[l], exp_w_down[l], sh_w_gate[l], sh_w_up[l],
                    sh_w_down[l], ln2_g[l], ln2_b[l])
    return (xs[0], xs[1])
```

```python
import functools
import itertools
import math

import jax
import jax.numpy as jnp
from jax import lax
from jax.experimental import pallas as pl
from jax.experimental.pallas import tpu as pltpu

F32 = jnp.float32
BF16 = jnp.bfloat16
HIGHEST = lax.Precision.HIGHEST

D_MODEL = 2048
GDN_QK_HEADS = 16
GDN_V_HEADS = 32
GDN_DK = 128
GDN_DV = 128
GDN_KEY_W = GDN_QK_HEADS * GDN_DK
GDN_VAL_W = GDN_V_HEADS * GDN_DV
GDN_CONV_CH = 2 * GDN_KEY_W + GDN_VAL_W
GDN_GATE_W = 4 * GDN_V_HEADS
GDN_CONV_K = 5
GDN_CHUNK = 64
DIFF_HEADS = 8
DIFF_HD = 128
DIFF_QK_W = 2 * DIFF_HEADS * DIFF_HD
DIFF_VAL_W = DIFF_HEADS * 2 * DIFF_HD
N_GROUPS = 8
TOPK_GROUPS = 4
TOP_K = 8
ROUTED_SCALE = 2.5
DEPTH = 1
DN_ALPHA = (2 * DEPTH) ** 0.25
LN_EPS = 1e-5
RMS_EPS = 1e-6
NEG_BIG = -1e30
LOG2E = math.log2(math.e)
ATTN_STAGGER = 2

P_QKV = 0
P_Z = P_QKV + GDN_CONV_CH
P_DQ = P_Z + GDN_VAL_W
P_DK = P_DQ + DIFF_QK_W
P_DV = P_DK + DIFF_QK_W
P_GATES = P_DV + DIFF_VAL_W
P_COLS = P_GATES + 2 * D_MODEL

VMEM_LIMIT = 56 * 1024 * 1024
SLAB = D_MODEL // 128


def _params(grid_rank, vmem=VMEM_LIMIT):
    return pltpu.CompilerParams(dimension_semantics=("arbitrary",) * grid_rank, vmem_limit_bytes=vmem)


def _sigmoid(x):
    return 1.0 / (1.0 + jnp.exp(-x))


def _silu(x):
    return x * _sigmoid(x)


def _layer_norm(x, g, b):
    mu = jnp.mean(x, axis=-1, keepdims=True)
    xc = x - mu
    var = jnp.mean(xc * xc, axis=-1, keepdims=True)
    return xc * lax.rsqrt(var + LN_EPS) * g + b


def _proj_in_body(x_ref, w_ref, o_ref, xb_ref, *, tn):
    j = pl.program_id(1)

    @pl.when(j == 0)
    def _():
        xb_ref[...] = x_ref[...].astype(BF16)

    acc = jnp.dot(xb_ref[...], w_ref[...], preferred_element_type=F32)
    col = j * tn
    is_gate = col >= P_GATES
    is_dq = jnp.logical_and(col >= P_DQ, col < P_DK)
    scale = jnp.where(is_dq, LOG2E * DIFF_HD ** -0.5, 1.0).astype(F32)

    @pl.when(is_gate)
    def _():
        o_ref[...] = _sigmoid(acc).astype(o_ref.dtype)

    @pl.when(jnp.logical_not(is_gate))
    def _():
        o_ref[...] = (acc * scale).astype(o_ref.dtype)


def _proj_in(x, w, *, tm=1024, tn=1024):
    n, k = x.shape
    m = w.shape[1]
    tm = min(tm, n)
    return pl.pallas_call(
        functools.partial(_proj_in_body, tn=tn),
        out_shape=jax.ShapeDtypeStruct((n, m), BF16),
        grid=(n // tm, m // tn),
        in_specs=[pl.BlockSpec((tm, k), lambda i, j: (i, 0)),
                  pl.BlockSpec((k, tn), lambda i, j: (0, j))],
        out_specs=pl.BlockSpec((tm, tn), lambda i, j: (i, j)),
        scratch_shapes=[pltpu.VMEM((tm, k), BF16)],
        compiler_params=_params(2),
        name="proj_in",
    )(x, w)


def _gates_body(x_ref, w_ref, prm_ref, o_ref, *, tt):
    ab = jnp.dot(x_ref[...].astype(BF16), w_ref[...], preferred_element_type=F32)
    a_log = prm_ref[0:1, :]
    dt_bias = prm_ref[1:2, :]
    y = ab + dt_bias
    softplus = jnp.maximum(y, 0.0) + jnp.log(1.0 + jnp.exp(-jnp.abs(y)))
    log_decay = -jnp.exp(a_log) * softplus
    beta = _sigmoid(ab)
    ri = lax.broadcasted_iota(jnp.int32, (tt, tt), 0)
    ci = lax.broadcasted_iota(jnp.int32, (tt, tt), 1)
    same = (ri // GDN_CHUNK) == (ci // GDN_CHUNK)
    m_lo = jnp.where(same, jnp.where(ci <= ri, 1.0, 0.0), 0.0).astype(F32)
    m_up = jnp.where(same, jnp.where(ci >= ri, 1.0, 0.0), 0.0).astype(F32)
    c_lo = jnp.dot(m_lo, log_decay, precision=HIGHEST, preferred_element_type=F32)
    c_up = jnp.dot(m_up, log_decay, precision=HIGHEST, preferred_element_type=F32)
    lane = lax.broadcasted_iota(jnp.int32, (tt, GDN_GATE_W), 1)
    is_a = (lane % (2 * GDN_V_HEADS)) < GDN_V_HEADS
    is_bwd = lane >= 2 * GDN_V_HEADS
    o_ref[...] = jnp.where(is_a, jnp.where(is_bwd, c_up, c_lo), beta)


def _gdn_gates(x, w_ab, a_log, dt_bias, *, tt=512):
    n, k = x.shape
    tt = min(tt, n)
    zeros = jnp.zeros((GDN_V_HEADS,), F32)
    prm = jnp.zeros((8, GDN_GATE_W), F32)
    prm = prm.at[0].set(jnp.concatenate([a_log[0], zeros, a_log[1], zeros]).astype(F32))
    prm = prm.at[1].set(jnp.concatenate([dt_bias[0], zeros, dt_bias[1], zeros]).astype(F32))
    return pl.pallas_call(
        functools.partial(_gates_body, tt=tt),
        out_shape=jax.ShapeDtypeStruct((n, GDN_GATE_W), F32),
        grid=(n // tt,),
        in_specs=[pl.BlockSpec((tt, k), lambda i: (i, 0)),
                  pl.BlockSpec((k, GDN_GATE_W), lambda i: (0, 0)),
                  pl.BlockSpec((8, GDN_GATE_W), lambda i: (0, 0))],
        out_specs=pl.BlockSpec((tt, GDN_GATE_W), lambda i: (i, 0)),
        compiler_params=_params(1),
        name="gdn_gates",
    )(x, w_ab, prm)


CONV_HALO = 16


def _any_equal(value, constants):
    hit = value == constants[0]
    for c in constants[1:]:
        hit = jnp.logical_or(hit, value == c)
    return hit


def _conv_body(prev_ref, cur_ref, next_ref, w_ref, o_ref, ext_ref, *, tt, tc, seq_starts, seq_ends):
    i = pl.program_id(0)
    j = pl.program_id(1)
    t0 = i * tt
    at_start = _any_equal(t0, seq_starts)
    at_end = _any_equal(t0 + tt, seq_ends)
    prev = prev_ref[...].astype(F32)[CONV_HALO - 8:, :]
    nxt = next_ref[...].astype(F32)[:8, :]
    ext_ref[0:8, :] = jnp.where(at_start, 0.0, prev)
    ext_ref[8:8 + tt, :] = cur_ref[...].astype(F32)
    ext_ref[8 + tt:16 + tt, :] = jnp.where(at_end, 0.0, nxt)
    pad = (GDN_CONV_K - 1) // 2
    acc = jnp.zeros((tt, tc), F32)
    for tap in range(GDN_CONV_K):
        acc = acc + ext_ref[pl.ds(8 - pad + tap, tt), :] * w_ref[tap:tap + 1, :]
    act = _silu(acc)
    col = j * tc
    is_qk = col < 2 * GDN_KEY_W
    q_scale = jnp.where(col < GDN_KEY_W, GDN_DK ** -0.5, 1.0).astype(F32)
    for s in range(tc // GDN_DK):
        a = act[:, s * GDN_DK:(s + 1) * GDN_DK]
        ss = jnp.sum(a * a, axis=-1, keepdims=True)
        normed = a * (lax.rsqrt(ss + RMS_EPS) * q_scale)
        o_ref[:, s * GDN_DK:(s + 1) * GDN_DK] = jnp.where(is_qk, normed, a).astype(o_ref.dtype)


def _gdn_conv(p, conv_w, seq_starts, seq_ends, *, tt=512, tc=512):
    n = p.shape[0]
    tt = min(tt, n)
    hb = tt // CONV_HALO
    n_halo = n // CONV_HALO
    return pl.pallas_call(
        functools.partial(_conv_body, tt=tt, tc=tc, seq_starts=seq_starts, seq_ends=seq_ends),
        out_shape=jax.ShapeDtypeStruct((n, GDN_CONV_CH), BF16),
        grid=(n // tt, GDN_CONV_CH // tc),
        in_specs=[pl.BlockSpec((CONV_HALO, tc), lambda i, j: (jnp.maximum(i * hb - 1, 0), j)),
                  pl.BlockSpec((tt, tc), lambda i, j: (i, j)),
                  pl.BlockSpec((CONV_HALO, tc), lambda i, j: (jnp.minimum((i + 1) * hb, n_halo - 1), j)),
                  pl.BlockSpec((GDN_CONV_K, tc), lambda i, j: (0, j))],
        out_specs=pl.BlockSpec((tt, tc), lambda i, j: (i, j)),
        scratch_shapes=[pltpu.VMEM((tt + 16, tc), F32)],
        compiler_params=_params(2),
        name="gdn_conv",
    )(p, p, p, conv_w)


def _dot_f32(a, b):
    return jnp.dot(a, b, precision=HIGHEST, preferred_element_type=F32)


def _dot_bf16(a, b):
    return jnp.dot(a.astype(BF16), b.astype(BF16), preferred_element_type=F32)


def _gdn_scan_body(*refs, tt, nt, hp, reverse, seq_starts, seq_ends):
    if reverse:
        q_ref, k_ref, v_ref, g_ref, b_ref, ofwd_ref, z_ref, nw_ref, o_ref, s_ref = refs
    else:
        q_ref, k_ref, v_ref, g_ref, b_ref, o_ref, s_ref = refs
    C = GDN_CHUNK
    t = pl.program_id(1)
    tok0 = ((nt - 1 - t) if reverse else t) * tt
    reset = _any_equal(tok0 + tt, seq_ends) if reverse else _any_equal(tok0, seq_starts)

    @pl.when(reset)
    def _():
        s_ref[...] = jnp.zeros_like(s_ref)

    n_chunks = tt // C
    ri = lax.broadcasted_iota(jnp.int32, (tt, tt), 0)
    ci = lax.broadcasted_iota(jnp.int32, (tt, tt), 1)
    same = (ri // C) == (ci // C)
    eye = ri == ci
    causal = jnp.logical_and(same, (ci >= ri) if reverse else (ci <= ri))
    strict = jnp.logical_and(same, (ci > ri) if reverse else (ci < ri))
    eye_f = jnp.where(eye, 1.0, 0.0).astype(F32)
    last = 0 if reverse else C - 1
    order = range(n_chunks - 1, -1, -1) if reverse else range(n_chunks)
    chunk_of_lane = lax.broadcasted_iota(jnp.int32, (1, tt), 1) // C
    chunk_of_lane_k = lax.broadcasted_iota(jnp.int32, (GDN_DK, tt), 1) // C

    shared = []
    for pp in range(hp):
        q = q_ref[:, pp * GDN_DK:(pp + 1) * GDN_DK]
        k = k_ref[:, pp * GDN_DK:(pp + 1) * GDN_DK]
        kf = k.astype(F32)
        kt = kf.T
        ktb = kt.astype(BF16)
        kq = jnp.dot(jnp.concatenate([k, q], axis=0), ktb, preferred_element_type=F32)
        shared.append((kf, q.astype(F32), kt, kq[:tt], kq[tt:]))
    per_head = [None] * (2 * hp)

    def head_stages(hh):
        kf, qf, kt, kk, qk = shared[hh // 2]
        cols = slice(hh * GDN_DV, (hh + 1) * GDN_DV)
        g_row = g_ref[hh]
        b_row = b_ref[hh]
        g_col = jnp.sum(jnp.where(eye, g_row, 0.0), axis=1, keepdims=True)
        b_col = jnp.sum(jnp.where(eye, b_row, 0.0), axis=1, keepdims=True)
        g_last = [g_row[:, c * C + last:c * C + last + 1] for c in range(n_chunks)]
        g_last_row = jnp.zeros((1, tt), F32)
        for c in range(n_chunks):
            g_last_row = jnp.where(chunk_of_lane == c, g_last[c], g_last_row)
        decay = jnp.exp(jnp.where(causal, g_col - g_row, NEG_BIG))
        low = jnp.where(strict, kk * decay, 0.0) * b_col
        low_b = low.astype(BF16)
        pw = jnp.dot(low_b, low_b, preferred_element_type=F32)
        yield
        inv = eye_f - low
        n_factors = int(math.log2(C)) - 1
        for j in range(n_factors):
            pw_b = pw.astype(BF16)
            if j < n_factors - 1:
                both = jnp.dot(jnp.concatenate([inv, pw], axis=0).astype(BF16), pw_b, preferred_element_type=F32)
                inv = inv + both[:tt]
                pw = both[tt:]
            else:
                inv = inv + jnp.dot(inv.astype(BF16), pw_b, preferred_element_type=F32)
            yield
        eg_col = jnp.exp(g_col)
        v = v_ref[:, cols].astype(F32)
        x = jnp.concatenate([v * b_col, kf * (b_col * eg_col)], axis=1)
        uw = jnp.dot(inv.astype(BF16), x.astype(BF16), preferred_element_type=F32).astype(BF16)
        yield
        from_q = jnp.dot((qk * decay).astype(BF16), uw, preferred_element_type=F32)
        yield
        ktd = kt * jnp.exp(g_last_row - g_row)
        lhs_k = jnp.concatenate([jnp.where(chunk_of_lane_k == c, ktd, 0.0) for c in range(n_chunks)],
                                axis=0).astype(BF16)
        from_k = jnp.dot(lhs_k, uw, preferred_element_type=F32)
        per_head[hh] = (qf * eg_col - from_q[:, GDN_DV:], from_q, from_k, g_last)
        yield

    for _ in itertools.zip_longest(*[head_stages(hh) for hh in range(2 * hp)]):
        pass

    for c in order:
        rows = slice(c * C, (c + 1) * C)
        krows = slice(c * GDN_DK, (c + 1) * GDN_DK)
        for pp in range(hp):
            heads = (2 * pp, 2 * pp + 1)
            lhs = jnp.concatenate([part for hh in heads
                                   for part in (per_head[hh][0][rows], per_head[hh][2][krows, GDN_DV:])],
                                  axis=0).astype(BF16)
            states = [s_ref[hh] for hh in heads]
            res = jnp.dot(lhs, jnp.concatenate(states, axis=1).astype(BF16), preferred_element_type=F32)
            for j, hh in enumerate(heads):
                _, from_q, from_k, g_last = per_head[hh]
                cols = slice(hh * GDN_DV, (hh + 1) * GDN_DV)
                blk = res[j * (C + GDN_DK):(j + 1) * (C + GDN_DK), j * GDN_DV:(j + 1) * GDN_DV]
                o = blk[:C] + from_q[rows, :GDN_DV]
                s_ref[hh] = states[j] * jnp.exp(g_last[c]) - blk[C:] + from_k[krows, :GDN_DV]
                if reverse:
                    tot = o + ofwd_ref[rows, cols]
                    ms = jnp.mean(tot * tot, axis=-1, keepdims=True)
                    z = z_ref[rows, cols].astype(F32)
                    o_ref[rows, cols] = (tot * lax.rsqrt(ms + RMS_EPS) * nw_ref[...] * _silu(z)).astype(o_ref.dtype)
                else:
                    o_ref[rows, cols] = o


def _gdn_scan(act, g_t, p, o_fwd, norm_w, seq_starts, seq_ends, *, reverse, tt=128, hp=8):
    n = act.shape[0]
    tt = min(tt, n)
    nt = n // tt
    groups = GDN_QK_HEADS // hp
    kw = hp * GDN_DK
    vw = 2 * hp * GDN_DV

    def tmap(t):
        return (nt - 1 - t) if reverse else t

    gate_blocks = (2 * GDN_V_HEADS) // (2 * hp)
    dirn = 1 if reverse else 0
    in_specs = [
        pl.BlockSpec((tt, kw), lambda h, t: (tmap(t), h)),
        pl.BlockSpec((tt, kw), lambda h, t: (tmap(t), groups + h)),
        pl.BlockSpec((tt, vw), lambda h, t: (tmap(t), 2 * GDN_KEY_W // vw + h)),
        pl.BlockSpec((2 * hp, 1, tt), lambda h, t: (dirn * gate_blocks + h, 0, tmap(t))),
        pl.BlockSpec((2 * hp, 1, tt), lambda h, t: (dirn * gate_blocks + groups + h, 0, tmap(t))),
    ]
    args = [act, act, act, g_t, g_t]
    if reverse:
        in_specs += [
            pl.BlockSpec((tt, vw), lambda h, t: (tmap(t), h)),
            pl.BlockSpec((tt, vw), lambda h, t: (tmap(t), P_Z // vw + h)),
            pl.BlockSpec((1, GDN_DV), lambda h, t: (0, 0)),
        ]
        args += [o_fwd, p, norm_w.reshape(1, GDN_DV).astype(F32)]
        out_dtype = BF16
    else:
        out_dtype = F32
    return pl.pallas_call(
        functools.partial(_gdn_scan_body, tt=tt, nt=nt, hp=hp, reverse=reverse,
                          seq_starts=seq_starts, seq_ends=seq_ends),
        out_shape=jax.ShapeDtypeStruct((n, GDN_VAL_W), out_dtype),
        grid=(groups, nt),
        in_specs=in_specs,
        out_specs=pl.BlockSpec((tt, vw), lambda h, t: (tmap(t), h)),
        scratch_shapes=[pltpu.VMEM((2 * hp, GDN_DK, GDN_DV), F32)],
        compiler_params=_params(2),
        name="gdn_scan_bwd" if reverse else "gdn_scan_fwd",
    )(*args)


def _attn_body(slope_ref, lam_ref, sw_ref, q_ref, k_ref, v_ref, o_ref, m_ref, l_ref, acc_ref, sd_ref,
               *, tile, nk, lam_init):
    h = pl.program_id(1)
    qi = pl.program_id(2)
    kj = pl.program_id(3)
    slope = slope_ref[h]

    n_sub = m_ref.shape[1]
    sub = tile // n_sub

    @pl.when(kj == 0)
    def _():
        m_ref[...] = jnp.full_like(m_ref, NEG_BIG)
        l_ref[...] = jnp.zeros_like(l_ref)
        acc_ref[...] = jnp.zeros_like(acc_ref)
        for r in range(n_sub):
            ii = lax.broadcasted_iota(jnp.int32, (sub, tile), 0) + r * sub
            jj = lax.broadcasted_iota(jnp.int32, (sub, tile), 1)
            sd_ref[r] = (ii - jj).astype(F32) * slope

    v = v_ref[...]
    tile_dist = jnp.full((sub, 1), (qi - kj) * tile, jnp.int32).astype(F32) * slope

    def update(add_bias, shift):
        def stages(r, m):
            rows = slice(r * sub, (r + 1) * sub)
            cols = slice(m * DIFF_HD, (m + 1) * DIFF_HD)
            t = add_bias(lax.dot_general(q_ref[rows, cols], k_ref[:, cols], (((1,), (1,)), ((), ())),
                                         preferred_element_type=F32), r)
            yield
            m_old = m_ref[m, r]
            m_new = jnp.maximum(m_old, jnp.max(t, axis=1, keepdims=True) - shift)
            alpha = jnp.exp2(m_old - m_new)
            m_ref[m, r] = m_new
            yield
            p = jnp.exp2(t - (m_new + shift))
            l_ref[m, r] = alpha * l_ref[m, r] + jnp.sum(p, axis=1, keepdims=True)
            p_b = p.astype(BF16)
            yield
            acc_ref[m, r] = alpha * acc_ref[m, r] + jnp.dot(p_b, v, preferred_element_type=F32)
            yield

        waiting = [stages(r, m) for r in range(n_sub) for m in range(2)]
        running = []
        step = 0
        while waiting or running:
            if waiting and step % ATTN_STAGGER == 0:
                running.append(waiting.pop(0))
            running = [g for g in running if next(g, "done") != "done"]
            step += 1

    @pl.when(kj < qi)
    def _():
        update(lambda s, r: s - sd_ref[r], tile_dist)

    @pl.when(kj > qi)
    def _():
        update(lambda s, r: s + sd_ref[r], -tile_dist)

    @pl.when(kj == qi)
    def _():
        update(lambda s, r: s - jnp.abs(sd_ref[r]), jnp.zeros((sub, 1), F32))

    @pl.when(kj == nk - 1)
    def _():
        lam = lam_ref[...]
        lam_full = (jnp.exp(jnp.sum(lam[0:1, :] * lam[1:2, :], axis=-1, keepdims=True))
                    - jnp.exp(jnp.sum(lam[2:3, :] * lam[3:4, :], axis=-1, keepdims=True)) + lam_init)
        for r in range(n_sub):
            o = acc_ref[0, r] / l_ref[0, r] - lam_full * (acc_ref[1, r] / l_ref[1, r])
            ms = jnp.mean(o * o, axis=-1, keepdims=True)
            o_ref[r * sub:(r + 1) * sub, :] = (o * lax.rsqrt(ms + LN_EPS) * sw_ref[...]
                                               * (1.0 - lam_init)).astype(o_ref.dtype)


def _diff_attention(p, lam, subln_w, tok_off, n_seq, seq_len, layer_idx, *, tile=1024, sub=512):
    tile = min(tile, seq_len)
    sub = min(sub, tile)
    n_sub = tile // sub
    nt = seq_len // tile
    lam_init = 0.8 - 0.6 * math.exp(-0.3 * layer_idx)
    slopes = jnp.asarray([LOG2E * 2.0 ** (-8.0 * (i + 1) / DIFF_HEADS) for i in range(DIFF_HEADS)], F32)
    w2 = 2 * DIFF_HD
    off = tok_off // tile
    return pl.pallas_call(
        functools.partial(_attn_body, tile=tile, nk=nt, lam_init=lam_init),
        out_shape=jax.ShapeDtypeStruct((n_seq * seq_len, DIFF_VAL_W), BF16),
        grid=(n_seq, DIFF_HEADS, nt, nt),
        in_specs=[pl.BlockSpec(memory_space=pltpu.SMEM),
                  pl.BlockSpec((4, DIFF_HD), lambda b, h, i, j: (0, 0)),
                  pl.BlockSpec((1, w2), lambda b, h, i, j: (0, 0)),
                  pl.BlockSpec((tile, w2), lambda b, h, i, j: (off + b * nt + i, P_DQ // w2 + h)),
                  pl.BlockSpec((tile, w2), lambda b, h, i, j: (off + b * nt + j, P_DK // w2 + h)),
                  pl.BlockSpec((tile, w2), lambda b, h, i, j: (off + b * nt + j, P_DV // w2 + h))],
        out_specs=pl.BlockSpec((tile, w2), lambda b, h, i, j: (b * nt + i, h)),
        scratch_shapes=[pltpu.VMEM((2, n_sub, sub, 1), F32), pltpu.VMEM((2, n_sub, sub, 1), F32),
                        pltpu.VMEM((2, n_sub, sub, w2), F32), pltpu.VMEM((n_sub, sub, tile), F32)],
        compiler_params=_params(4),
        name="diff_attention",
    )(slopes, lam.astype(F32), subln_w.reshape(1, w2).astype(F32), p, p, p)


def _merge_body(og_ref, od_ref, wg_ref, wd_ref, gg_ref, gd_ref, o_ref):
    a = jnp.dot(og_ref[...], wg_ref[...], preferred_element_type=F32)
    b = jnp.dot(od_ref[...], wd_ref[...], preferred_element_type=F32)
    o_ref[...] = (gg_ref[...].astype(F32) * a + gd_ref[...].astype(F32) * b).astype(o_ref.dtype)


def _merge(o_gdn, o_diff, w_g, w_d, p, *, tm=512, tn=512):
    n = o_gdn.shape[0]
    tm = min(tm, n)
    return pl.pallas_call(
        _merge_body,
        out_shape=jax.ShapeDtypeStruct((n, D_MODEL), BF16),
        grid=(n // tm, D_MODEL // tn),
        in_specs=[pl.BlockSpec((tm, GDN_VAL_W), lambda i, j: (i, 0)),
                  pl.BlockSpec((tm, DIFF_VAL_W), lambda i, j: (i, 0)),
                  pl.BlockSpec((GDN_VAL_W, tn), lambda i, j: (0, j)),
                  pl.BlockSpec((DIFF_VAL_W, tn), lambda i, j: (0, j)),
                  pl.BlockSpec((tm, tn), lambda i, j: (i, P_GATES // tn + j)),
                  pl.BlockSpec((tm, tn), lambda i, j: (i, (P_GATES + D_MODEL) // tn + j))],
        out_specs=pl.BlockSpec((tm, tn), lambda i, j: (i, j)),
        compiler_params=_params(2),
        name="branch_merge",
    )(o_gdn, o_diff, w_g, w_d, p, p)


def _store_slabs(slab_ref, value, n_rows):
    for s in range(SLAB):
        slab_ref[pl.ds(s, n_rows, stride=SLAB), :] = value[:, s * 128:(s + 1) * 128]


def _load_slabs(slab_ref, n_rows, first=0, stride=SLAB):
    return jnp.concatenate([slab_ref[pl.ds(first + s, n_rows, stride=stride), :] for s in range(SLAB)], axis=1)


def _out_ln_body(m_ref, w_ref, x_ref, g_ref, b_ref, o_ref, slab_ref, *, tm):
    mix = jnp.dot(m_ref[...], w_ref[...], preferred_element_type=F32)
    h = _layer_norm(DN_ALPHA * x_ref[...] + mix, g_ref[...], b_ref[...])
    o_ref[...] = h
    _store_slabs(slab_ref, h, tm)


def _out_ln(merged, w_out, x, g, b, *, tm=512):
    n = x.shape[0]
    tm = min(tm, n)
    return pl.pallas_call(
        functools.partial(_out_ln_body, tm=tm),
        out_shape=(jax.ShapeDtypeStruct((n, D_MODEL), F32), jax.ShapeDtypeStruct((n * SLAB, 128), F32)),
        grid=(n // tm,),
        in_specs=[pl.BlockSpec((tm, D_MODEL), lambda i: (i, 0)),
                  pl.BlockSpec((D_MODEL, D_MODEL), lambda i: (0, 0)),
                  pl.BlockSpec((tm, D_MODEL), lambda i: (i, 0)),
                  pl.BlockSpec((1, D_MODEL), lambda i: (0, 0)),
                  pl.BlockSpec((1, D_MODEL), lambda i: (0, 0))],
        out_specs=(pl.BlockSpec((tm, D_MODEL), lambda i: (i, 0)), pl.BlockSpec((tm * SLAB, 128), lambda i: (i, 0))),
        compiler_params=_params(1),
        name="out_proj_ln1",
    )(merged, w_out, x, g.reshape(1, D_MODEL).astype(F32), b.reshape(1, D_MODEL).astype(F32))


def _first_argmax(x, row_ids, n_rows):
    mx = jnp.max(x, axis=0, keepdims=True)
    idx = jnp.min(jnp.where(x == mx, row_ids, n_rows), axis=0, keepdims=True)
    return mx, idx


def _router_body(h_ref, w_ref, bias_ref, e_ref, g_ref, *, n_exp, tm):
    per = n_exp // N_GROUPS
    logits = lax.dot_general(w_ref[...], h_ref[...], (((1,), (1,)), ((), ())),
                             precision=HIGHEST, preferred_element_type=F32)
    scores = _sigmoid(logits)
    choice = scores + bias_ref[...]
    ids_g = lax.broadcasted_iota(jnp.int32, (per, tm), 0)
    group_rows = []
    for g in range(N_GROUPS):
        xg = choice[g * per:(g + 1) * per, :]
        m1, i1 = _first_argmax(xg, ids_g, per)
        m2 = jnp.max(jnp.where(ids_g == i1, -jnp.inf, xg), axis=0, keepdims=True)
        group_rows.append(m1 + m2)
    gs = jnp.concatenate(group_rows, axis=0)
    ids_n = lax.broadcasted_iota(jnp.int32, (N_GROUPS, tm), 0)
    keep = jnp.zeros((N_GROUPS, tm), F32)
    for _ in range(TOPK_GROUPS):
        _, gi = _first_argmax(gs, ids_n, N_GROUPS)
        hit = ids_n == gi
        keep = jnp.where(hit, 1.0, keep)
        gs = jnp.where(hit, -jnp.inf, gs)
    masked = jnp.concatenate(
        [jnp.where(keep[g:g + 1, :] > 0.5, choice[g * per:(g + 1) * per, :], -jnp.inf)
         for g in range(N_GROUPS)], axis=0)
    ids_e = lax.broadcasted_iota(jnp.int32, (n_exp, tm), 0)
    top_idx = []
    top_w = []
    for _ in range(TOP_K):
        _, ei = _first_argmax(masked, ids_e, n_exp)
        hit = ids_e == ei
        top_idx.append(ei)
        top_w.append(jnp.sum(jnp.where(hit, scores, 0.0), axis=0, keepdims=True))
        masked = jnp.where(hit, -jnp.inf, masked)
    gw = jnp.concatenate(top_w, axis=0)
    gw = gw / (jnp.sum(gw, axis=0, keepdims=True) + 1e-20) * ROUTED_SCALE
    e_ref[...] = jnp.concatenate(top_idx, axis=0)
    g_ref[...] = gw


def _router(h, router_w, router_bias, *, tm=256):
    n = h.shape[0]
    n_exp = router_w.shape[1]
    tm = min(tm, n)
    return pl.pallas_call(
        functools.partial(_router_body, n_exp=n_exp, tm=tm),
        out_shape=(jax.ShapeDtypeStruct((TOP_K, n), jnp.int32), jax.ShapeDtypeStruct((TOP_K, n), F32)),
        grid=(n // tm,),
        in_specs=[pl.BlockSpec((tm, D_MODEL), lambda i: (i, 0)),
                  pl.BlockSpec((n_exp, D_MODEL), lambda i: (0, 0)),
                  pl.BlockSpec((n_exp, 1), lambda i: (0, 0))],
        out_specs=(pl.BlockSpec((TOP_K, tm), lambda i: (0, i)), pl.BlockSpec((TOP_K, tm), lambda i: (0, i))),
        compiler_params=_params(1),
        name="moe_router",
    )(h, router_w.T.astype(F32), router_bias.reshape(n_exp, 1).astype(F32))


def _expert_body(be_ref, nu_ref, tok_ref, tok_next_ref, dst_ref, h_ref, wg_ref, wu_ref, wd_ref, y_ref,
                 x_buf, y_buf, wgb_ref, wub_ref, wdb_ref, gather_sem, scatter_sem, *, tb, n_blocks):
    i = pl.program_id(0)
    n_used = nu_ref[0]
    slot = i % 2

    def start_gather(rows_ref, buf):
        def issue(r, carry):
            src = pl.multiple_of(rows_ref[0, 0, r] * SLAB, SLAB)
            pltpu.make_async_copy(h_ref.at[pl.ds(src, SLAB)], x_buf.at[buf, pl.ds(r * SLAB, SLAB)],
                                  gather_sem.at[buf]).start()
            return carry
        lax.fori_loop(0, tb, issue, 0, unroll=8)

    def wait_buffer(buf_ref, sem_ref, buf):
        pltpu.make_async_copy(buf_ref.at[buf], buf_ref.at[buf], sem_ref.at[buf]).wait()

    @pl.when(jnp.logical_and(i == 0, n_used > 0))
    def _():
        start_gather(tok_ref, 0)

    @pl.when(i + 1 < n_used)
    def _():
        start_gather(tok_next_ref, 1 - slot)

    @pl.when(i < n_used)
    def _():
        fresh = jnp.logical_or(i == 0, be_ref[i] != be_ref[jnp.maximum(i - 1, 0)])

        @pl.when(fresh)
        def _():
            wgb_ref[...] = wg_ref[...].astype(BF16)
            wub_ref[...] = wu_ref[...].astype(BF16)
            wdb_ref[...] = wd_ref[...].astype(BF16)

        wait_buffer(x_buf, gather_sem, slot)

        @pl.when(i >= 2)
        def _():
            wait_buffer(y_buf, scatter_sem, slot)

        xb = _load_slabs(x_buf.at[slot], tb).astype(BF16)
        gate = jnp.dot(xb, wgb_ref[...], preferred_element_type=F32)
        up = jnp.dot(xb, wub_ref[...], preferred_element_type=F32)
        act = (_silu(gate) * up).astype(BF16)
        _store_slabs(y_buf.at[slot], jnp.dot(act, wdb_ref[...], preferred_element_type=F32), tb)

        def issue(r, carry):
            dst = pl.multiple_of(dst_ref[0, 0, r] * SLAB, SLAB)
            pltpu.make_async_copy(y_buf.at[slot, pl.ds(r * SLAB, SLAB)], y_ref.at[pl.ds(dst, SLAB)],
                                  scatter_sem.at[slot]).start()
            return carry
        lax.fori_loop(0, tb, issue, 0, unroll=8)

    @pl.when(i == n_blocks - 1)
    def _():
        @pl.when(n_used >= 1)
        def _():
            wait_buffer(y_buf, scatter_sem, (n_used - 1) % 2)

        @pl.when(n_used >= 2)
        def _():
            wait_buffer(y_buf, scatter_sem, n_used % 2)


def _experts(h_slab, slot_tok, slot_dst, block_e, n_used, w_gate, w_up, w_down, *, tb):
    n_slots = slot_tok.shape[0]
    n_blocks = n_slots // tb
    ff = w_gate.shape[2]
    idx_shape = (n_blocks, 1, tb)
    grid_spec = pltpu.PrefetchScalarGridSpec(
        num_scalar_prefetch=2,
        grid=(n_blocks,),
        in_specs=[pl.BlockSpec((1, 1, tb), lambda i, be, nu: (i, 0, 0), memory_space=pltpu.SMEM),
                  pl.BlockSpec((1, 1, tb), lambda i, be, nu: (jnp.minimum(i + 1, n_blocks - 1), 0, 0),
                               memory_space=pltpu.SMEM),
                  pl.BlockSpec((1, 1, tb), lambda i, be, nu: (i, 0, 0), memory_space=pltpu.SMEM),
                  pl.BlockSpec(memory_space=pl.ANY),
                  pl.BlockSpec((None, D_MODEL, ff), lambda i, be, nu: (be[i], 0, 0)),
                  pl.BlockSpec((None, D_MODEL, ff), lambda i, be, nu: (be[i], 0, 0)),
                  pl.BlockSpec((None, ff, D_MODEL), lambda i, be, nu: (be[i], 0, 0))],
        out_specs=pl.BlockSpec(memory_space=pl.ANY),
        scratch_shapes=[pltpu.VMEM((2, tb * SLAB, 128), F32), pltpu.VMEM((2, tb * SLAB, 128), F32),
                        pltpu.VMEM((D_MODEL, ff), BF16), pltpu.VMEM((D_MODEL, ff), BF16),
                        pltpu.VMEM((ff, D_MODEL), BF16),
                        pltpu.SemaphoreType.DMA((2,)), pltpu.SemaphoreType.DMA((2,))],
    )
    tok3 = slot_tok.reshape(idx_shape)
    return pl.pallas_call(
        functools.partial(_expert_body, tb=tb, n_blocks=n_blocks),
        out_shape=jax.ShapeDtypeStruct((n_slots * SLAB, 128), F32),
        grid_spec=grid_spec,
        compiler_params=_params(1),
        name="moe_experts",
    )(block_e, n_used, tok3, tok3, slot_dst.reshape(idx_shape), h_slab, w_gate, w_up, w_down)


def _combine_body(h_ref, y_ref, gw_ref, sgu_ref, sd_ref, g_ref, b_ref, o_ref, *, ff, tm):
    h = h_ref[...]
    hb = h.astype(BF16)
    gu = jnp.dot(hb, sgu_ref[...], preferred_element_type=F32)
    act = (_silu(gu[:, :ff]) * gu[:, ff:]).astype(BF16)
    f = jnp.dot(act, sd_ref[...], preferred_element_type=F32)
    gw = gw_ref[...]
    for k in range(TOP_K):
        f = f + _load_slabs(y_ref, tm, first=k * SLAB, stride=TOP_K * SLAB) * gw[:, k:k + 1]
    o_ref[...] = _layer_norm(DN_ALPHA * h + f, g_ref[...], b_ref[...])


def _combine(h, y_tok, gw_t, sh_gu, sh_down, g, b, tok_off, n_rows, *, tm=128):
    ff = sh_down.shape[0]
    tm = min(tm, n_rows)
    off = tok_off // tm
    return pl.pallas_call(
        functools.partial(_combine_body, ff=ff, tm=tm),
        out_shape=jax.ShapeDtypeStruct((n_rows, D_MODEL), F32),
        grid=(n_rows // tm,),
        in_specs=[pl.BlockSpec((tm, D_MODEL), lambda i: (off + i, 0)),
                  pl.BlockSpec((tm * TOP_K * SLAB, 128), lambda i: (off + i, 0)),
                  pl.BlockSpec((tm, TOP_K), lambda i: (off + i, 0)),
                  pl.BlockSpec((D_MODEL, 2 * ff), lambda i: (0, 0)),
                  pl.BlockSpec((ff, D_MODEL), lambda i: (0, 0)),
                  pl.BlockSpec((1, D_MODEL), lambda i: (0, 0)),
                  pl.BlockSpec((1, D_MODEL), lambda i: (0, 0))],
        out_specs=pl.BlockSpec((tm, D_MODEL), lambda i: (i, 0)),
        compiler_params=_params(1),
        name="moe_combine_ln2",
    )(h, y_tok, gw_t, sh_gu, sh_down, g.reshape(1, D_MODEL).astype(F32), b.reshape(1, D_MODEL).astype(F32))


def _routing_tables(top_e, n_exp, tb):
    k, n = top_e.shape
    a = k * n
    n_blocks = a // tb + n_exp
    a_bits = max(a - 1, 1).bit_length()
    assert n_exp << a_bits < 2 ** 31
    ids = jnp.arange(a, dtype=jnp.int32)
    keys = lax.sort(top_e.reshape(-1) * (1 << a_bits) + ids)
    e_sorted = keys >> a_bits
    a_sorted = keys & ((1 << a_bits) - 1)
    tok_sorted = a_sorted % n
    k_sorted = a_sorted // n
    experts = jnp.arange(n_exp, dtype=jnp.int32)
    start = jnp.sum((e_sorted[:, None] < experts[None, :]).astype(jnp.int32), axis=0)
    counts = jnp.concatenate([start[1:], jnp.full((1,), a, jnp.int32)]) - start
    padded = (counts + tb - 1) // tb * tb
    pad_end = jnp.cumsum(padded)
    shift = pad_end - padded - start
    dest = ids + jnp.sum(jnp.where(e_sorted[:, None] == experts[None, :], shift[None, :], 0), axis=1)
    slot_row = jnp.full((n_blocks * tb,), -1, jnp.int32).at[dest].set(tok_sorted * k + k_sorted)
    is_pad = slot_row < 0
    pad_rank = jnp.cumsum(is_pad.astype(jnp.int32)) - 1
    slot_tok = jnp.where(is_pad, 0, slot_row // k)
    slot_dst = jnp.where(is_pad, a + pad_rank, slot_row)
    block_start = jnp.arange(n_blocks, dtype=jnp.int32) * tb
    block_e = jnp.minimum(jnp.sum(block_start[:, None] >= pad_end[None, :], axis=1), n_exp - 1).astype(jnp.int32)
    n_used = (pad_end[-1] // tb).astype(jnp.int32).reshape(1)
    block_e = jnp.where(block_start < pad_end[-1], block_e, block_e[jnp.maximum(n_used[0] - 1, 0)])
    return slot_tok, slot_dst, block_e, n_used


def _layer(xs, layer_idx, w_in, conv_w, a_log, dt_bias, norm_w, lam, subln_w, w_branch_gdn, w_branch_diff,
           w_out, ln1_g, ln1_b, router_w, router_bias, w_gate, w_up, w_down, sh_gate, sh_up, sh_down,
           ln2_g, ln2_b, *, moe_tb=256):
    seqs = [(x.shape[0], x.shape[1]) for x in xs]
    x = jnp.concatenate([x.reshape(-1, D_MODEL) for x in xs], axis=0)
    n = x.shape[0]
    seq_starts, seq_ends, group_off = [], [], []
    off = 0
    for b, t in seqs:
        group_off.append(off)
        for _ in range(b):
            seq_starts.append(off)
            off += t
            seq_ends.append(off)
    seq_starts, seq_ends = tuple(seq_starts), tuple(seq_ends)

    o_qkv, o_z, o_ab = GDN_CONV_CH, GDN_CONV_CH + GDN_VAL_W, GDN_CONV_CH + GDN_VAL_W + GDN_GATE_W
    w_main = jnp.concatenate([w_in[:, :o_z], w_in[:, o_ab:]], axis=1).astype(BF16)
    w_ab = w_in[:, o_z:o_ab].astype(BF16)

    p = _proj_in(x, w_main)
    gates = _gdn_gates(x, w_ab, a_log, dt_bias)
    g_t = gates.T.reshape(GDN_GATE_W, 1, n)
    act = _gdn_conv(p, conv_w.astype(F32), seq_starts, seq_ends)
    o_fwd = _gdn_scan(act, g_t, p, None, norm_w, seq_starts, seq_ends, reverse=False)
    o_gdn = _gdn_scan(act, g_t, p, o_fwd, norm_w, seq_starts, seq_ends, reverse=True)
    o_diff = jnp.concatenate(
        [_diff_attention(p, lam, subln_w, goff, b, t, layer_idx) for (b, t), goff in zip(seqs, group_off)], axis=0)
    merged = _merge(o_gdn, o_diff, w_branch_gdn.astype(BF16), w_branch_diff.astype(BF16), p)
    h, h_slab = _out_ln(merged, w_out.astype(BF16), x, ln1_g, ln1_b)

    n_exp = router_w.shape[1]
    top_e, gw = _router(h, router_w, router_bias)
    slot_tok, slot_dst, block_e, n_used = _routing_tables(top_e, n_exp, moe_tb)
    y_tok = _experts(h_slab, slot_tok, slot_dst, block_e, n_used, w_gate, w_up, w_down, tb=moe_tb)
    sh_gu = jnp.concatenate([sh_gate, sh_up], axis=1).astype(BF16)
    outs = []
    for (b, t), goff in zip(seqs, group_off):
        y = _combine(h, y_tok, gw.T, sh_gu, sh_down.astype(BF16), ln2_g, ln2_b, goff, b * t)
        outs.append(y.reshape(b, t, D_MODEL))
    return outs


def kernel(x_prompt, x_sample, w_in, gdn_conv_w, gdn_a_log, gdn_dt_bias, gdn_norm_w, diff_lambda, diff_subln_w, w_branch_gdn, w_branch_diff, w_out, ln1_g, ln1_b, router_w, router_bias, exp_w_gate, exp_w_up, exp_w_down, sh_w_gate, sh_w_up, sh_w_down, ln2_g, ln2_b):
    xs = [x_prompt, x_sample]
    for l in range(DEPTH):
        xs = _layer(xs, l, w_in[l], gdn_conv_w[l], gdn_a_log[l], gdn_dt_bias[l], gdn_norm_w[l], diff_lambda[l],
                    diff_subln_w[l], w_branch_gdn[l], w_branch_diff[l], w_out[l], ln1_g[l], ln1_b[l], router_w[l],
                    router_bias[l], exp_w_gate[l], exp_w_up[l], exp_w_down[l], sh_w_gate[l], sh_w_up[l],
                    sh_w_down[l], ln2_g[l], ln2_b[l])
    return (xs[0], xs[1])
```

```python
import functools
import itertools
import math

import jax
import jax.numpy as jnp
from jax import lax
from jax.experimental import pallas as pl
from jax.experimental.pallas import tpu as pltpu

F32 = jnp.float32
BF16 = jnp.bfloat16
HIGHEST = lax.Precision.HIGHEST

D_MODEL = 2048
GDN_QK_HEADS = 16
GDN_V_HEADS = 32
GDN_DK = 128
GDN_DV = 128
GDN_KEY_W = GDN_QK_HEADS * GDN_DK
GDN_VAL_W = GDN_V_HEADS * GDN_DV
GDN_CONV_CH = 2 * GDN_KEY_W + GDN_VAL_W
GDN_GATE_W = 4 * GDN_V_HEADS
GDN_CONV_K = 5
GDN_CHUNK = 64
DIFF_HEADS = 8
DIFF_HD = 128
DIFF_QK_W = 2 * DIFF_HEADS * DIFF_HD
DIFF_VAL_W = DIFF_HEADS * 2 * DIFF_HD
N_GROUPS = 8
TOPK_GROUPS = 4
TOP_K = 8
ROUTED_SCALE = 2.5
DEPTH = 1
DN_ALPHA = (2 * DEPTH) ** 0.25
LN_EPS = 1e-5
RMS_EPS = 1e-6
NEG_BIG = -1e30
LOG2E = math.log2(math.e)
ATTN_STAGGER = 2

P_QKV = 0
P_Z = P_QKV + GDN_CONV_CH
P_DQ = P_Z + GDN_VAL_W
P_DK = P_DQ + DIFF_QK_W
P_DV = P_DK + DIFF_QK_W
P_GATES = P_DV + DIFF_VAL_W
P_COLS = P_GATES + 2 * D_MODEL

VMEM_LIMIT = 56 * 1024 * 1024
SLAB = D_MODEL // 128


def _params(grid_rank, vmem=VMEM_LIMIT):
    return pltpu.CompilerParams(dimension_semantics=("arbitrary",) * grid_rank, vmem_limit_bytes=vmem)


def _sigmoid(x):
    return 1.0 / (1.0 + jnp.exp(-x))


def _silu(x):
    return x * _sigmoid(x)


def _layer_norm(x, g, b):
    mu = jnp.mean(x, axis=-1, keepdims=True)
    xc = x - mu
    var = jnp.mean(xc * xc, axis=-1, keepdims=True)
    return xc * lax.rsqrt(var + LN_EPS) * g + b


def _proj_in_body(x_ref, w_ref, o_ref, xb_ref, *, tn):
    j = pl.program_id(1)

    @pl.when(j == 0)
    def _():
        xb_ref[...] = x_ref[...].astype(BF16)

    acc = jnp.dot(xb_ref[...], w_ref[...], preferred_element_type=F32)
    col = j * tn
    is_gate = col >= P_GATES
    is_dq = jnp.logical_and(col >= P_DQ, col < P_DK)
    scale = jnp.where(is_dq, LOG2E * DIFF_HD ** -0.5, 1.0).astype(F32)

    @pl.when(is_gate)
    def _():
        o_ref[...] = _sigmoid(acc).astype(o_ref.dtype)

    @pl.when(jnp.logical_not(is_gate))
    def _():
        o_ref[...] = (acc * scale).astype(o_ref.dtype)


def _proj_in(x, w, *, tm=1024, tn=1024):
    n, k = x.shape
    m = w.shape[1]
    tm = min(tm, n)
    return pl.pallas_call(
        functools.partial(_proj_in_body, tn=tn),
        out_shape=jax.ShapeDtypeStruct((n, m), BF16),
        grid=(n // tm, m // tn),
        in_specs=[pl.BlockSpec((tm, k), lambda i, j: (i, 0)),
                  pl.BlockSpec((k, tn), lambda i, j: (0, j))],
        out_specs=pl.BlockSpec((tm, tn), lambda i, j: (i, j)),
        scratch_shapes=[pltpu.VMEM((tm, k), BF16)],
        compiler_params=_params(2),
        name="proj_in",
    )(x, w)


def _gates_body(x_ref, w_ref, prm_ref, o_ref, *, tt):
    ab = jnp.dot(x_ref[...].astype(BF16), w_ref[...], preferred_element_type=F32)
    a_log = prm_ref[0:1, :]
    dt_bias = prm_ref[1:2, :]
    y = ab + dt_bias
    softplus = jnp.maximum(y, 0.0) + jnp.log(1.0 + jnp.exp(-jnp.abs(y)))
    log_decay = -jnp.exp(a_log) * softplus
    beta = _sigmoid(ab)
    ri = lax.broadcasted_iota(jnp.int32, (tt, tt), 0)
    ci = lax.broadcasted_iota(jnp.int32, (tt, tt), 1)
    same = (ri // GDN_CHUNK) == (ci // GDN_CHUNK)
    m_lo = jnp.where(same, jnp.where(ci <= ri, 1.0, 0.0), 0.0).astype(F32)
    m_up = jnp.where(same, jnp.where(ci >= ri, 1.0, 0.0), 0.0).astype(F32)
    c_lo = jnp.dot(m_lo, log_decay, precision=HIGHEST, preferred_element_type=F32)
    c_up = jnp.dot(m_up, log_decay, precision=HIGHEST, preferred_element_type=F32)
    lane = lax.broadcasted_iota(jnp.int32, (tt, GDN_GATE_W), 1)
    is_a = (lane % (2 * GDN_V_HEADS)) < GDN_V_HEADS
    is_bwd = lane >= 2 * GDN_V_HEADS
    o_ref[...] = jnp.where(is_a, jnp.where(is_bwd, c_up, c_lo), beta)


def _gdn_gates(x, w_ab, a_log, dt_bias, *, tt=512):
    n, k = x.shape
    tt = min(tt, n)
    zeros = jnp.zeros((GDN_V_HEADS,), F32)
    prm = jnp.zeros((8, GDN_GATE_W), F32)
    prm = prm.at[0].set(jnp.concatenate([a_log[0], zeros, a_log[1], zeros]).astype(F32))
    prm = prm.at[1].set(jnp.concatenate([dt_bias[0], zeros, dt_bias[1], zeros]).astype(F32))
    return pl.pallas_call(
        functools.partial(_gates_body, tt=tt),
        out_shape=jax.ShapeDtypeStruct((n, GDN_GATE_W), F32),
        grid=(n // tt,),
        in_specs=[pl.BlockSpec((tt, k), lambda i: (i, 0)),
                  pl.BlockSpec((k, GDN_GATE_W), lambda i: (0, 0)),
                  pl.BlockSpec((8, GDN_GATE_W), lambda i: (0, 0))],
        out_specs=pl.BlockSpec((tt, GDN_GATE_W), lambda i: (i, 0)),
        compiler_params=_params(1),
        name="gdn_gates",
    )(x, w_ab, prm)


CONV_HALO = 16


def _any_equal(value, constants):
    hit = value == constants[0]
    for c in constants[1:]:
        hit = jnp.logical_or(hit, value == c)
    return hit


def _conv_body(prev_ref, cur_ref, next_ref, w_ref, o_ref, ext_ref, *, tt, tc, seq_starts, seq_ends):
    i = pl.program_id(0)
    j = pl.program_id(1)
    t0 = i * tt
    at_start = _any_equal(t0, seq_starts)
    at_end = _any_equal(t0 + tt, seq_ends)
    prev = prev_ref[...].astype(F32)[CONV_HALO - 8:, :]
    nxt = next_ref[...].astype(F32)[:8, :]
    ext_ref[0:8, :] = jnp.where(at_start, 0.0, prev)
    ext_ref[8:8 + tt, :] = cur_ref[...].astype(F32)
    ext_ref[8 + tt:16 + tt, :] = jnp.where(at_end, 0.0, nxt)
    pad = (GDN_CONV_K - 1) // 2
    acc = jnp.zeros((tt, tc), F32)
    for tap in range(GDN_CONV_K):
        acc = acc + ext_ref[pl.ds(8 - pad + tap, tt), :] * w_ref[tap:tap + 1, :]
    act = _silu(acc)
    col = j * tc
    is_qk = col < 2 * GDN_KEY_W
    q_scale = jnp.where(col < GDN_KEY_W, GDN_DK ** -0.5, 1.0).astype(F32)
    for s in range(tc // GDN_DK):
        a = act[:, s * GDN_DK:(s + 1) * GDN_DK]
        ss = jnp.sum(a * a, axis=-1, keepdims=True)
        normed = a * (lax.rsqrt(ss + RMS_EPS) * q_scale)
        o_ref[:, s * GDN_DK:(s + 1) * GDN_DK] = jnp.where(is_qk, normed, a).astype(o_ref.dtype)


def _gdn_conv(p, conv_w, seq_starts, seq_ends, *, tt=512, tc=512):
    n = p.shape[0]
    tt = min(tt, n)
    hb = tt // CONV_HALO
    n_halo = n // CONV_HALO
    return pl.pallas_call(
        functools.partial(_conv_body, tt=tt, tc=tc, seq_starts=seq_starts, seq_ends=seq_ends),
        out_shape=jax.ShapeDtypeStruct((n, GDN_CONV_CH), BF16),
        grid=(n // tt, GDN_CONV_CH // tc),
        in_specs=[pl.BlockSpec((CONV_HALO, tc), lambda i, j: (jnp.maximum(i * hb - 1, 0), j)),
                  pl.BlockSpec((tt, tc), lambda i, j: (i, j)),
                  pl.BlockSpec((CONV_HALO, tc), lambda i, j: (jnp.minimum((i + 1) * hb, n_halo - 1), j)),
                  pl.BlockSpec((GDN_CONV_K, tc), lambda i, j: (0, j))],
        out_specs=pl.BlockSpec((tt, tc), lambda i, j: (i, j)),
        scratch_shapes=[pltpu.VMEM((tt + 16, tc), F32)],
        compiler_params=_params(2),
        name="gdn_conv",
    )(p, p, p, conv_w)


def _dot_f32(a, b):
    return jnp.dot(a, b, precision=HIGHEST, preferred_element_type=F32)


def _dot_bf16(a, b):
    return jnp.dot(a.astype(BF16), b.astype(BF16), preferred_element_type=F32)


def _gdn_scan_body(*refs, tt, nt, hp, reverse, seq_starts, seq_ends):
    if reverse:
        q_ref, k_ref, v_ref, g_ref, b_ref, ofwd_ref, z_ref, nw_ref, o_ref, s_ref = refs
    else:
        q_ref, k_ref, v_ref, g_ref, b_ref, o_ref, s_ref = refs
    C = GDN_CHUNK
    t = pl.program_id(1)
    tok0 = ((nt - 1 - t) if reverse else t) * tt
    reset = _any_equal(tok0 + tt, seq_ends) if reverse else _any_equal(tok0, seq_starts)

    @pl.when(reset)
    def _():
        s_ref[...] = jnp.zeros_like(s_ref)

    n_chunks = tt // C
    ri = lax.broadcasted_iota(jnp.int32, (tt, tt), 0)
    ci = lax.broadcasted_iota(jnp.int32, (tt, tt), 1)
    same = (ri // C) == (ci // C)
    eye = ri == ci
    causal = jnp.logical_and(same, (ci >= ri) if reverse else (ci <= ri))
    strict = jnp.logical_and(same, (ci > ri) if reverse else (ci < ri))
    eye_f = jnp.where(eye, 1.0, 0.0).astype(F32)
    last = 0 if reverse else C - 1
    order = range(n_chunks - 1, -1, -1) if reverse else range(n_chunks)
    chunk_of_lane = lax.broadcasted_iota(jnp.int32, (1, tt), 1) // C
    chunk_of_lane_k = lax.broadcasted_iota(jnp.int32, (GDN_DK, tt), 1) // C

    shared = []
    for pp in range(hp):
        q = q_ref[:, pp * GDN_DK:(pp + 1) * GDN_DK]
        k = k_ref[:, pp * GDN_DK:(pp + 1) * GDN_DK]
        kf = k.astype(F32)
        kt = kf.T
        ktb = kt.astype(BF16)
        kq = jnp.dot(jnp.concatenate([k, q], axis=0), ktb, preferred_element_type=F32)
        shared.append((kf, q.astype(F32), kt, kq[:tt], kq[tt:]))
    per_head = [None] * (2 * hp)

    def head_stages(hh):
        kf, qf, kt, kk, qk = shared[hh // 2]
        cols = slice(hh * GDN_DV, (hh + 1) * GDN_DV)
        g_row = g_ref[hh]
        b_row = b_ref[hh]
        g_col = jnp.sum(jnp.where(eye, g_row, 0.0), axis=1, keepdims=True)
        b_col = jnp.sum(jnp.where(eye, b_row, 0.0), axis=1, keepdims=True)
        g_last = [g_row[:, c * C + last:c * C + last + 1] for c in range(n_chunks)]
        g_last_row = jnp.zeros((1, tt), F32)
        for c in range(n_chunks):
            g_last_row = jnp.where(chunk_of_lane == c, g_last[c], g_last_row)
        decay = jnp.exp(jnp.where(causal, g_col - g_row, NEG_BIG))
        low = jnp.where(strict, kk * decay, 0.0) * b_col
        low_b = low.astype(BF16)
        pw = jnp.dot(low_b, low_b, preferred_element_type=F32)
        yield
        inv = eye_f - low
        n_factors = int(math.log2(C)) - 1
        for j in range(n_factors):
            pw_b = pw.astype(BF16)
            if j < n_factors - 1:
                both = jnp.dot(jnp.concatenate([inv, pw], axis=0).astype(BF16), pw_b, preferred_element_type=F32)
                inv = inv + both[:tt]
                pw = both[tt:]
            else:
                inv = inv + jnp.dot(inv.astype(BF16), pw_b, preferred_element_type=F32)
            yield
        eg_col = jnp.exp(g_col)
        v = v_ref[:, cols].astype(F32)
        x = jnp.concatenate([v * b_col, kf * (b_col * eg_col)], axis=1)
        uw = jnp.dot(inv.astype(BF16), x.astype(BF16), preferred_element_type=F32).astype(BF16)
        yield
        from_q = jnp.dot((qk * decay).astype(BF16), uw, preferred_element_type=F32)
        yield
        ktd = kt * jnp.exp(g_last_row - g_row)
        lhs_k = jnp.concatenate([jnp.where(chunk_of_lane_k == c, ktd, 0.0) for c in range(n_chunks)],
                                axis=0).astype(BF16)
        from_k = jnp.dot(lhs_k, uw, preferred_element_type=F32)
        per_head[hh] = (qf * eg_col - from_q[:, GDN_DV:], from_q, from_k, g_last)
        yield

    for _ in itertools.zip_longest(*[head_stages(hh) for hh in range(2 * hp)]):
        pass

    for c in order:
        rows = slice(c * C, (c + 1) * C)
        krows = slice(c * GDN_DK, (c + 1) * GDN_DK)
        for pp in range(hp):
            heads = (2 * pp, 2 * pp + 1)
            lhs = jnp.concatenate([part for hh in heads
                                   for part in (per_head[hh][0][rows], per_head[hh][2][krows, GDN_DV:])],
                                  axis=0).astype(BF16)
            states = [s_ref[hh] for hh in heads]
            res = jnp.dot(lhs, jnp.concatenate(states, axis=1).astype(BF16), preferred_element_type=F32)
            for j, hh in enumerate(heads):
                _, from_q, from_k, g_last = per_head[hh]
                cols = slice(hh * GDN_DV, (hh + 1) * GDN_DV)
                blk = res[j * (C + GDN_DK):(j + 1) * (C + GDN_DK), j * GDN_DV:(j + 1) * GDN_DV]
                o = blk[:C] + from_q[rows, :GDN_DV]
                s_ref[hh] = states[j] * jnp.exp(g_last[c]) - blk[C:] + from_k[krows, :GDN_DV]
                if reverse:
                    tot = o + ofwd_ref[rows, cols]
                    ms = jnp.mean(tot * tot, axis=-1, keepdims=True)
                    z = z_ref[rows, cols].astype(F32)
                    o_ref[rows, cols] = (tot * lax.rsqrt(ms + RMS_EPS) * nw_ref[...] * _silu(z)).astype(o_ref.dtype)
                else:
                    o_ref[rows, cols] = o


def _gdn_scan(act, g_t, p, o_fwd, norm_w, seq_starts, seq_ends, *, reverse, tt=128, hp=8):
    n = act.shape[0]
    tt = min(tt, n)
    nt = n // tt
    groups = GDN_QK_HEADS // hp
    kw = hp * GDN_DK
    vw = 2 * hp * GDN_DV

    def tmap(t):
        return (nt - 1 - t) if reverse else t

    gate_blocks = (2 * GDN_V_HEADS) // (2 * hp)
    dirn = 1 if reverse else 0
    in_specs = [
        pl.BlockSpec((tt, kw), lambda h, t: (tmap(t), h)),
        pl.BlockSpec((tt, kw), lambda h, t: (tmap(t), groups + h)),
        pl.BlockSpec((tt, vw), lambda h, t: (tmap(t), 2 * GDN_KEY_W // vw + h)),
        pl.BlockSpec((2 * hp, 1, tt), lambda h, t: (dirn * gate_blocks + h, 0, tmap(t))),
        pl.BlockSpec((2 * hp, 1, tt), lambda h, t: (dirn * gate_blocks + groups + h, 0, tmap(t))),
    ]
    args = [act, act, act, g_t, g_t]
    if reverse:
        in_specs += [
            pl.BlockSpec((tt, vw), lambda h, t: (tmap(t), h)),
            pl.BlockSpec((tt, vw), lambda h, t: (tmap(t), P_Z // vw + h)),
            pl.BlockSpec((1, GDN_DV), lambda h, t: (0, 0)),
        ]
        args += [o_fwd, p, norm_w.reshape(1, GDN_DV).astype(F32)]
        out_dtype = BF16
    else:
        out_dtype = F32
    return pl.pallas_call(
        functools.partial(_gdn_scan_body, tt=tt, nt=nt, hp=hp, reverse=reverse,
                          seq_starts=seq_starts, seq_ends=seq_ends),
        out_shape=jax.ShapeDtypeStruct((n, GDN_VAL_W), out_dtype),
        grid=(groups, nt),
        in_specs=in_specs,
        out_specs=pl.BlockSpec((tt, vw), lambda h, t: (tmap(t), h)),
        scratch_shapes=[pltpu.VMEM((2 * hp, GDN_DK, GDN_DV), F32)],
        compiler_params=_params(2),
        name="gdn_scan_bwd" if reverse else "gdn_scan_fwd",
    )(*args)


def _attn_body(slope_ref, lam_ref, sw_ref, q_ref, k_ref, v_ref, o_ref, m_ref, l_ref, acc_ref, sd_ref,
               *, tile, nk, lam_init):
    h = pl.program_id(1)
    qi = pl.program_id(2)
    kj = pl.program_id(3)
    slope = slope_ref[h]

    n_sub = m_ref.shape[1]
    sub = tile // n_sub

    @pl.when(kj == 0)
    def _():
        m_ref[...] = jnp.full_like(m_ref, NEG_BIG)
        l_ref[...] = jnp.zeros_like(l_ref)
        acc_ref[...] = jnp.zeros_like(acc_ref)
        for r in range(n_sub):
            ii = lax.broadcasted_iota(jnp.int32, (sub, tile), 0) + r * sub
            jj = lax.broadcasted_iota(jnp.int32, (sub, tile), 1)
            sd_ref[r] = (ii - jj).astype(F32) * slope

    v = v_ref[...]
    tile_dist = jnp.full((sub, 1), (qi - kj) * tile, jnp.int32).astype(F32) * slope

    def update(add_bias, shift):
        def stages(r, m):
            rows = slice(r * sub, (r + 1) * sub)
            cols = slice(m * DIFF_HD, (m + 1) * DIFF_HD)
            t = add_bias(lax.dot_general(q_ref[rows, cols], k_ref[:, cols], (((1,), (1,)), ((), ())),
                                         preferred_element_type=F32), r)
            yield
            m_old = m_ref[m, r]
            m_new = jnp.maximum(m_old, jnp.max(t, axis=1, keepdims=True) - shift)
            alpha = jnp.exp2(m_old - m_new)
            m_ref[m, r] = m_new
            yield
            p = jnp.exp2(t - (m_new + shift))
            l_ref[m, r] = alpha * l_ref[m, r] + jnp.sum(p, axis=1, keepdims=True)
            p_b = p.astype(BF16)
            yield
            acc_ref[m, r] = alpha * acc_ref[m, r] + jnp.dot(p_b, v, preferred_element_type=F32)
            yield

        waiting = [stages(r, m) for r in range(n_sub) for m in range(2)]
        running = []
        step = 0
        while waiting or running:
            if waiting and step % ATTN_STAGGER == 0:
                running.append(waiting.pop(0))
            running = [g for g in running if next(g, "done") != "done"]
            step += 1

    @pl.when(kj < qi)
    def _():
        update(lambda s, r: s - sd_ref[r], tile_dist)

    @pl.when(kj > qi)
    def _():
        update(lambda s, r: s + sd_ref[r], -tile_dist)

    @pl.when(kj == qi)
    def _():
        update(lambda s, r: s - jnp.abs(sd_ref[r]), jnp.zeros((sub, 1), F32))

    @pl.when(kj == nk - 1)
    def _():
        lam = lam_ref[...]
        lam_full = (jnp.exp(jnp.sum(lam[0:1, :] * lam[1:2, :], axis=-1, keepdims=True))
                    - jnp.exp(jnp.sum(lam[2:3, :] * lam[3:4, :], axis=-1, keepdims=True)) + lam_init)
        for r in range(n_sub):
            o = acc_ref[0, r] / l_ref[0, r] - lam_full * (acc_ref[1, r] / l_ref[1, r])
            ms = jnp.mean(o * o, axis=-1, keepdims=True)
            o_ref[r * sub:(r + 1) * sub, :] = (o * lax.rsqrt(ms + LN_EPS) * sw_ref[...]
                                               * (1.0 - lam_init)).astype(o_ref.dtype)


def _diff_attention(p, lam, subln_w, tok_off, n_seq, seq_len, layer_idx, *, tile=1024, sub=512):
    tile = min(tile, seq_len)
    sub = min(sub, tile)
    n_sub = tile // sub
    nt = seq_len // tile
    lam_init = 0.8 - 0.6 * math.exp(-0.3 * layer_idx)
    slopes = jnp.asarray([LOG2E * 2.0 ** (-8.0 * (i + 1) / DIFF_HEADS) for i in range(DIFF_HEADS)], F32)
    w2 = 2 * DIFF_HD
    off = tok_off // tile
    return pl.pallas_call(
        functools.partial(_attn_body, tile=tile, nk=nt, lam_init=lam_init),
        out_shape=jax.ShapeDtypeStruct((n_seq * seq_len, DIFF_VAL_W), BF16),
        grid=(n_seq, DIFF_HEADS, nt, nt),
        in_specs=[pl.BlockSpec(memory_space=pltpu.SMEM),
                  pl.BlockSpec((4, DIFF_HD), lambda b, h, i, j: (0, 0)),
                  pl.BlockSpec((1, w2), lambda b, h, i, j: (0, 0)),
                  pl.BlockSpec((tile, w2), lambda b, h, i, j: (off + b * nt + i, P_DQ // w2 + h)),
                  pl.BlockSpec((tile, w2), lambda b, h, i, j: (off + b * nt + j, P_DK // w2 + h)),
                  pl.BlockSpec((tile, w2), lambda b, h, i, j: (off + b * nt + j, P_DV // w2 + h))],
        out_specs=pl.BlockSpec((tile, w2), lambda b, h, i, j: (b * nt + i, h)),
        scratch_shapes=[pltpu.VMEM((2, n_sub, sub, 1), F32), pltpu.VMEM((2, n_sub, sub, 1), F32),
                        pltpu.VMEM((2, n_sub, sub, w2), F32), pltpu.VMEM((n_sub, sub, tile), F32)],
        compiler_params=_params(4),
        name="diff_attention",
    )(slopes, lam.astype(F32), subln_w.reshape(1, w2).astype(F32), p, p, p)


def _merge_body(og_ref, od_ref, wg_ref, wd_ref, gg_ref, gd_ref, o_ref):
    a = jnp.dot(og_ref[...], wg_ref[...], preferred_element_type=F32)
    b = jnp.dot(od_ref[...], wd_ref[...], preferred_element_type=F32)
    o_ref[...] = (gg_ref[...].astype(F32) * a + gd_ref[...].astype(F32) * b).astype(o_ref.dtype)


def _merge(o_gdn, o_diff, w_g, w_d, p, *, tm=512, tn=512):
    n = o_gdn.shape[0]
    tm = min(tm, n)
    return pl.pallas_call(
        _merge_body,
        out_shape=jax.ShapeDtypeStruct((n, D_MODEL), BF16),
        grid=(n // tm, D_MODEL // tn),
        in_specs=[pl.BlockSpec((tm, GDN_VAL_W), lambda i, j: (i, 0)),
                  pl.BlockSpec((tm, DIFF_VAL_W), lambda i, j: (i, 0)),
                  pl.BlockSpec((GDN_VAL_W, tn), lambda i, j: (0, j)),
                  pl.BlockSpec((DIFF_VAL_W, tn), lambda i, j: (0, j)),
                  pl.BlockSpec((tm, tn), lambda i, j: (i, P_GATES // tn + j)),
                  pl.BlockSpec((tm, tn), lambda i, j: (i, (P_GATES + D_MODEL) // tn + j))],
        out_specs=pl.BlockSpec((tm, tn), lambda i, j: (i, j)),
        compiler_params=_params(2),
        name="branch_merge",
    )(o_gdn, o_diff, w_g, w_d, p, p)


def _store_slabs(slab_ref, value, n_rows):
    for s in range(SLAB):
        slab_ref[pl.ds(s, n_rows, stride=SLAB), :] = value[:, s * 128:(s + 1) * 128]


def _load_slabs(slab_ref, n_rows, first=0, stride=SLAB):
    return jnp.concatenate([slab_ref[pl.ds(first + s, n_rows, stride=stride), :] for s in range(SLAB)], axis=1)


def _out_ln_body(m_ref, w_ref, x_ref, g_ref, b_ref, o_ref, slab_ref, *, tm):
    mix = jnp.dot(m_ref[...], w_ref[...], preferred_element_type=F32)
    h = _layer_norm(DN_ALPHA * x_ref[...] + mix, g_ref[...], b_ref[...])
    o_ref[...] = h
    _store_slabs(slab_ref, h, tm)


def _out_ln(merged, w_out, x, g, b, *, tm=512):
    n = x.shape[0]
    tm = min(tm, n)
    return pl.pallas_call(
        functools.partial(_out_ln_body, tm=tm),
        out_shape=(jax.ShapeDtypeStruct((n, D_MODEL), F32), jax.ShapeDtypeStruct((n * SLAB, 128), F32)),
        grid=(n // tm,),
        in_specs=[pl.BlockSpec((tm, D_MODEL), lambda i: (i, 0)),
                  pl.BlockSpec((D_MODEL, D_MODEL), lambda i: (0, 0)),
                  pl.BlockSpec((tm, D_MODEL), lambda i: (i, 0)),
                  pl.BlockSpec((1, D_MODEL), lambda i: (0, 0)),
                  pl.BlockSpec((1, D_MODEL), lambda i: (0, 0))],
        out_specs=(pl.BlockSpec((tm, D_MODEL), lambda i: (i, 0)), pl.BlockSpec((tm * SLAB, 128), lambda i: (i, 0))),
        compiler_params=_params(1),
        name="out_proj_ln1",
    )(merged, w_out, x, g.reshape(1, D_MODEL).astype(F32), b.reshape(1, D_MODEL).astype(F32))


def _first_argmax(x, row_ids, n_rows):
    mx = jnp.max(x, axis=0, keepdims=True)
    idx = jnp.min(jnp.where(x == mx, row_ids, n_rows), axis=0, keepdims=True)
    return mx, idx


def _router_body(h_ref, w_ref, bias_ref, e_ref, g_ref, *, n_exp, tm):
    per = n_exp // N_GROUPS
    logits = lax.dot_general(w_ref[...], h_ref[...], (((1,), (1,)), ((), ())),
                             precision=HIGHEST, preferred_element_type=F32)
    scores = _sigmoid(logits)
    choice = scores + bias_ref[...]
    ids_g = lax.broadcasted_iota(jnp.int32, (per, tm), 0)
    group_rows = []
    for g in range(N_GROUPS):
        xg = choice[g * per:(g + 1) * per, :]
        m1, i1 = _first_argmax(xg, ids_g, per)
        m2 = jnp.max(jnp.where(ids_g == i1, -jnp.inf, xg), axis=0, keepdims=True)
        group_rows.append(m1 + m2)
    gs = jnp.concatenate(group_rows, axis=0)
    ids_n = lax.broadcasted_iota(jnp.int32, (N_GROUPS, tm), 0)
    keep = jnp.zeros((N_GROUPS, tm), F32)
    for _ in range(TOPK_GROUPS):
        _, gi = _first_argmax(gs, ids_n, N_GROUPS)
        hit = ids_n == gi
        keep = jnp.where(hit, 1.0, keep)
        gs = jnp.where(hit, -jnp.inf, gs)
    masked = jnp.concatenate(
        [jnp.where(keep[g:g + 1, :] > 0.5, choice[g * per:(g + 1) * per, :], -jnp.inf)
         for g in range(N_GROUPS)], axis=0)
    ids_e = lax.broadcasted_iota(jnp.int32, (n_exp, tm), 0)
    top_idx = []
    top_w = []
    for _ in range(TOP_K):
        _, ei = _first_argmax(masked, ids_e, n_exp)
        hit = ids_e == ei
        top_idx.append(ei)
        top_w.append(jnp.sum(jnp.where(hit, scores, 0.0), axis=0, keepdims=True))
        masked = jnp.where(hit, -jnp.inf, masked)
    gw = jnp.concatenate(top_w, axis=0)
    gw = gw / (jnp.sum(gw, axis=0, keepdims=True) + 1e-20) * ROUTED_SCALE
    e_ref[...] = jnp.concatenate(top_idx, axis=0)
    g_ref[...] = gw


def _router(h, router_w, router_bias, *, tm=256):
    n = h.shape[0]
    n_exp = router_w.shape[1]
    tm = min(tm, n)
    return pl.pallas_call(
        functools.partial(_router_body, n_exp=n_exp, tm=tm),
        out_shape=(jax.ShapeDtypeStruct((TOP_K, n), jnp.int32), jax.ShapeDtypeStruct((TOP_K, n), F32)),
        grid=(n // tm,),
        in_specs=[pl.BlockSpec((tm, D_MODEL), lambda i: (i, 0)),
                  pl.BlockSpec((n_exp, D_MODEL), lambda i: (0, 0)),
                  pl.BlockSpec((n_exp, 1), lambda i: (0, 0))],
        out_specs=(pl.BlockSpec((TOP_K, tm), lambda i: (0, i)), pl.BlockSpec((TOP_K, tm), lambda i: (0, i))),
        compiler_params=_params(1),
        name="moe_router",
    )(h, router_w.T.astype(F32), router_bias.reshape(n_exp, 1).astype(F32))


def _expert_body(be_ref, nu_ref, tok_ref, tok_next_ref, dst_ref, h_ref, wg_ref, wu_ref, wd_ref, y_ref,
                 x_buf, y_buf, wgb_ref, wub_ref, wdb_ref, gather_sem, scatter_sem, *, tb, n_blocks):
    i = pl.program_id(0)
    n_used = nu_ref[0]
    slot = i % 2

    def start_gather(rows_ref, buf):
        def issue(r, carry):
            src = pl.multiple_of(rows_ref[0, 0, r] * SLAB, SLAB)
            pltpu.make_async_copy(h_ref.at[pl.ds(src, SLAB)], x_buf.at[buf, pl.ds(r * SLAB, SLAB)],
                                  gather_sem.at[buf]).start()
            return carry
        lax.fori_loop(0, tb, issue, 0, unroll=8)

    def wait_buffer(buf_ref, sem_ref, buf):
        pltpu.make_async_copy(buf_ref.at[buf], buf_ref.at[buf], sem_ref.at[buf]).wait()

    @pl.when(jnp.logical_and(i == 0, n_used > 0))
    def _():
        start_gather(tok_ref, 0)

    @pl.when(i + 1 < n_used)
    def _():
        start_gather(tok_next_ref, 1 - slot)

    @pl.when(i < n_used)
    def _():
        fresh = jnp.logical_or(i == 0, be_ref[i] != be_ref[jnp.maximum(i - 1, 0)])

        @pl.when(fresh)
        def _():
            wgb_ref[...] = wg_ref[...].astype(BF16)
            wub_ref[...] = wu_ref[...].astype(BF16)
            wdb_ref[...] = wd_ref[...].astype(BF16)

        wait_buffer(x_buf, gather_sem, slot)

        @pl.when(i >= 2)
        def _():
            wait_buffer(y_buf, scatter_sem, slot)

        xb = _load_slabs(x_buf.at[slot], tb).astype(BF16)
        gate = jnp.dot(xb, wgb_ref[...], preferred_element_type=F32)
        up = jnp.dot(xb, wub_ref[...], preferred_element_type=F32)
        act = (_silu(gate) * up).astype(BF16)
        _store_slabs(y_buf.at[slot], jnp.dot(act, wdb_ref[...], preferred_element_type=F32), tb)

        def issue(r, carry):
            dst = pl.multiple_of(dst_ref[0, 0, r] * SLAB, SLAB)
            pltpu.make_async_copy(y_buf.at[slot, pl.ds(r * SLAB, SLAB)], y_ref.at[pl.ds(dst, SLAB)],
                                  scatter_sem.at[slot]).start(priority=1)
            return carry
        lax.fori_loop(0, tb, issue, 0, unroll=8)

    @pl.when(i == n_blocks - 1)
    def _():
        @pl.when(n_used >= 1)
        def _():
            wait_buffer(y_buf, scatter_sem, (n_used - 1) % 2)

        @pl.when(n_used >= 2)
        def _():
            wait_buffer(y_buf, scatter_sem, n_used % 2)


def _experts(h_slab, slot_tok, slot_dst, block_e, n_used, w_gate, w_up, w_down, *, tb):
    n_slots = slot_tok.shape[0]
    n_blocks = n_slots // tb
    ff = w_gate.shape[2]
    idx_shape = (n_blocks, 1, tb)
    grid_spec = pltpu.PrefetchScalarGridSpec(
        num_scalar_prefetch=2,
        grid=(n_blocks,),
        in_specs=[pl.BlockSpec((1, 1, tb), lambda i, be, nu: (i, 0, 0), memory_space=pltpu.SMEM),
                  pl.BlockSpec((1, 1, tb), lambda i, be, nu: (jnp.minimum(i + 1, n_blocks - 1), 0, 0),
                               memory_space=pltpu.SMEM),
                  pl.BlockSpec((1, 1, tb), lambda i, be, nu: (i, 0, 0), memory_space=pltpu.SMEM),
                  pl.BlockSpec(memory_space=pl.ANY),
                  pl.BlockSpec((None, D_MODEL, ff), lambda i, be, nu: (be[i], 0, 0)),
                  pl.BlockSpec((None, D_MODEL, ff), lambda i, be, nu: (be[i], 0, 0)),
                  pl.BlockSpec((None, ff, D_MODEL), lambda i, be, nu: (be[i], 0, 0))],
        out_specs=pl.BlockSpec(memory_space=pl.ANY),
        scratch_shapes=[pltpu.VMEM((2, tb * SLAB, 128), F32), pltpu.VMEM((2, tb * SLAB, 128), F32),
                        pltpu.VMEM((D_MODEL, ff), BF16), pltpu.VMEM((D_MODEL, ff), BF16),
                        pltpu.VMEM((ff, D_MODEL), BF16),
                        pltpu.SemaphoreType.DMA((2,)), pltpu.SemaphoreType.DMA((2,))],
    )
    tok3 = slot_tok.reshape(idx_shape)
    return pl.pallas_call(
        functools.partial(_expert_body, tb=tb, n_blocks=n_blocks),
        out_shape=jax.ShapeDtypeStruct((n_slots * SLAB, 128), F32),
        grid_spec=grid_spec,
        compiler_params=_params(1),
        name="moe_experts",
    )(block_e, n_used, tok3, tok3, slot_dst.reshape(idx_shape), h_slab, w_gate, w_up, w_down)


def _combine_body(h_ref, y_ref, gw_ref, sgu_ref, sd_ref, g_ref, b_ref, o_ref, *, ff, tm):
    h = h_ref[...]
    hb = h.astype(BF16)
    gu = jnp.dot(hb, sgu_ref[...], preferred_element_type=F32)
    act = (_silu(gu[:, :ff]) * gu[:, ff:]).astype(BF16)
    f = jnp.dot(act, sd_ref[...], preferred_element_type=F32)
    gw = gw_ref[...]
    for k in range(TOP_K):
        f = f + _load_slabs(y_ref, tm, first=k * SLAB, stride=TOP_K * SLAB) * gw[:, k:k + 1]
    o_ref[...] = _layer_norm(DN_ALPHA * h + f, g_ref[...], b_ref[...])


def _combine(h, y_tok, gw_t, sh_gu, sh_down, g, b, tok_off, n_rows, *, tm=128):
    ff = sh_down.shape[0]
    tm = min(tm, n_rows)
    off = tok_off // tm
    return pl.pallas_call(
        functools.partial(_combine_body, ff=ff, tm=tm),
        out_shape=jax.ShapeDtypeStruct((n_rows, D_MODEL), F32),
        grid=(n_rows // tm,),
        in_specs=[pl.BlockSpec((tm, D_MODEL), lambda i: (off + i, 0)),
                  pl.BlockSpec((tm * TOP_K * SLAB, 128), lambda i: (off + i, 0)),
                  pl.BlockSpec((tm, TOP_K), lambda i: (off + i, 0)),
                  pl.BlockSpec((D_MODEL, 2 * ff), lambda i: (0, 0)),
                  pl.BlockSpec((ff, D_MODEL), lambda i: (0, 0)),
                  pl.BlockSpec((1, D_MODEL), lambda i: (0, 0)),
                  pl.BlockSpec((1, D_MODEL), lambda i: (0, 0))],
        out_specs=pl.BlockSpec((tm, D_MODEL), lambda i: (i, 0)),
        compiler_params=_params(1),
        name="moe_combine_ln2",
    )(h, y_tok, gw_t, sh_gu, sh_down, g.reshape(1, D_MODEL).astype(F32), b.reshape(1, D_MODEL).astype(F32))


def _routing_tables(top_e, n_exp, tb):
    k, n = top_e.shape
    a = k * n
    n_blocks = a // tb + n_exp
    a_bits = max(a - 1, 1).bit_length()
    assert n_exp << a_bits < 2 ** 31
    ids = jnp.arange(a, dtype=jnp.int32)
    keys = lax.sort(top_e.reshape(-1) * (1 << a_bits) + ids)
    e_sorted = keys >> a_bits
    a_sorted = keys & ((1 << a_bits) - 1)
    tok_sorted = a_sorted % n
    k_sorted = a_sorted // n
    experts = jnp.arange(n_exp, dtype=jnp.int32)
    start = jnp.sum((e_sorted[:, None] < experts[None, :]).astype(jnp.int32), axis=0)
    counts = jnp.concatenate([start[1:], jnp.full((1,), a, jnp.int32)]) - start
    padded = (counts + tb - 1) // tb * tb
    pad_end = jnp.cumsum(padded)
    shift = pad_end - padded - start
    dest = ids + jnp.sum(jnp.where(e_sorted[:, None] == experts[None, :], shift[None, :], 0), axis=1)
    slot_row = jnp.full((n_blocks * tb,), -1, jnp.int32).at[dest].set(tok_sorted * k + k_sorted)
    is_pad = slot_row < 0
    pad_rank = jnp.cumsum(is_pad.astype(jnp.int32)) - 1
    slot_tok = jnp.where(is_pad, 0, slot_row // k)
    slot_dst = jnp.where(is_pad, a + pad_rank, slot_row)
    block_start = jnp.arange(n_blocks, dtype=jnp.int32) * tb
    block_e = jnp.minimum(jnp.sum(block_start[:, None] >= pad_end[None, :], axis=1), n_exp - 1).astype(jnp.int32)
    n_used = (pad_end[-1] // tb).astype(jnp.int32).reshape(1)
    block_e = jnp.where(block_start < pad_end[-1], block_e, block_e[jnp.maximum(n_used[0] - 1, 0)])
    return slot_tok, slot_dst, block_e, n_used


def _layer(xs, layer_idx, w_in, conv_w, a_log, dt_bias, norm_w, lam, subln_w, w_branch_gdn, w_branch_diff,
           w_out, ln1_g, ln1_b, router_w, router_bias, w_gate, w_up, w_down, sh_gate, sh_up, sh_down,
           ln2_g, ln2_b, *, moe_tb=256):
    seqs = [(x.shape[0], x.shape[1]) for x in xs]
    x = jnp.concatenate([x.reshape(-1, D_MODEL) for x in xs], axis=0)
    n = x.shape[0]
    seq_starts, seq_ends, group_off = [], [], []
    off = 0
    for b, t in seqs:
        group_off.append(off)
        for _ in range(b):
            seq_starts.append(off)
            off += t
            seq_ends.append(off)
    seq_starts, seq_ends = tuple(seq_starts), tuple(seq_ends)

    o_qkv, o_z, o_ab = GDN_CONV_CH, GDN_CONV_CH + GDN_VAL_W, GDN_CONV_CH + GDN_VAL_W + GDN_GATE_W
    w_main = jnp.concatenate([w_in[:, :o_z], w_in[:, o_ab:]], axis=1).astype(BF16)
    w_ab = w_in[:, o_z:o_ab].astype(BF16)

    p = _proj_in(x, w_main)
    gates = _gdn_gates(x, w_ab, a_log, dt_bias)
    g_t = gates.T.reshape(GDN_GATE_W, 1, n)
    act = _gdn_conv(p, conv_w.astype(F32), seq_starts, seq_ends)
    o_fwd = _gdn_scan(act, g_t, p, None, norm_w, seq_starts, seq_ends, reverse=False)
    o_gdn = _gdn_scan(act, g_t, p, o_fwd, norm_w, seq_starts, seq_ends, reverse=True)
    o_diff = jnp.concatenate(
        [_diff_attention(p, lam, subln_w, goff, b, t, layer_idx) for (b, t), goff in zip(seqs, group_off)], axis=0)
    merged = _merge(o_gdn, o_diff, w_branch_gdn.astype(BF16), w_branch_diff.astype(BF16), p)
    h, h_slab = _out_ln(merged, w_out.astype(BF16), x, ln1_g, ln1_b)

    n_exp = router_w.shape[1]
    top_e, gw = _router(h, router_w, router_bias)
    slot_tok, slot_dst, block_e, n_used = _routing_tables(top_e, n_exp, moe_tb)
    y_tok = _experts(h_slab, slot_tok, slot_dst, block_e, n_used, w_gate, w_up, w_down, tb=moe_tb)
    sh_gu = jnp.concatenate([sh_gate, sh_up], axis=1).astype(BF16)
    outs = []
    for (b, t), goff in zip(seqs, group_off):
        y = _combine(h, y_tok, gw.T, sh_gu, sh_down.astype(BF16), ln2_g, ln2_b, goff, b * t)
        outs.append(y.reshape(b, t, D_MODEL))
    return outs


def kernel(x_prompt, x_sample, w_in, gdn_conv_w, gdn_a_log, gdn_dt_bias, gdn_norm_w, diff_lambda, diff_subln_w, w_branch_gdn, w_branch_diff, w_out, ln1_g, ln1_b, router_w, router_bias, exp_w_gate, exp_w_up, exp_w_down, sh_w_gate, sh_w_up, sh_w_down, ln2_g, ln2_b):
    xs = [x_prompt, x_sample]
    for l in range(DEPTH):
        xs = _layer(xs, l, w_in[l], gdn_conv_w[l], gdn_a_log[l], gdn_dt_bias[l], gdn_norm_w[l], diff_lambda[l],
                    diff_subln_w[l], w_branch_gdn[l], w_branch_diff[l], w_out[l], ln1_g[l], ln1_b[l], router_w[l],
                    router_bias[l], exp_w_gate[l], exp_w_up[l], exp_w_down[l], sh_w_gate[l], sh_w_up[l],
                    sh_w_down[l], ln2_g[l], ln2_b[l])
    return (xs[0], xs[1])
```

```python
import functools
import itertools
import math

import jax
import jax.numpy as jnp
from jax import lax
from jax.experimental import pallas as pl
from jax.experimental.pallas import tpu as pltpu

F32 = jnp.float32
BF16 = jnp.bfloat16
HIGHEST = lax.Precision.HIGHEST

D_MODEL = 2048
GDN_QK_HEADS = 16
GDN_V_HEADS = 32
GDN_DK = 128
GDN_DV = 128
GDN_KEY_W = GDN_QK_HEADS * GDN_DK
GDN_VAL_W = GDN_V_HEADS * GDN_DV
GDN_CONV_CH = 2 * GDN_KEY_W + GDN_VAL_W
GDN_GATE_W = 4 * GDN_V_HEADS
GDN_CONV_K = 5
GDN_CHUNK = 64
DIFF_HEADS = 8
DIFF_HD = 128
DIFF_QK_W = 2 * DIFF_HEADS * DIFF_HD
DIFF_VAL_W = DIFF_HEADS * 2 * DIFF_HD
N_GROUPS = 8
TOPK_GROUPS = 4
TOP_K = 8
ROUTED_SCALE = 2.5
DEPTH = 1
DN_ALPHA = (2 * DEPTH) ** 0.25
LN_EPS = 1e-5
RMS_EPS = 1e-6
NEG_BIG = -1e30
LOG2E = math.log2(math.e)
ATTN_STAGGER = 2

P_QKV = 0
P_Z = P_QKV + GDN_CONV_CH
P_DQ = P_Z + GDN_VAL_W
P_DK = P_DQ + DIFF_QK_W
P_DV = P_DK + DIFF_QK_W
P_GATES = P_DV + DIFF_VAL_W
P_COLS = P_GATES + 2 * D_MODEL

VMEM_LIMIT = 56 * 1024 * 1024
SLAB = D_MODEL // 128


def _params(grid_rank, vmem=VMEM_LIMIT):
    return pltpu.CompilerParams(dimension_semantics=("arbitrary",) * grid_rank, vmem_limit_bytes=vmem)


def _sigmoid(x):
    return 1.0 / (1.0 + jnp.exp(-x))


def _silu(x):
    return x * _sigmoid(x)


def _layer_norm(x, g, b):
    mu = jnp.mean(x, axis=-1, keepdims=True)
    xc = x - mu
    var = jnp.mean(xc * xc, axis=-1, keepdims=True)
    return xc * lax.rsqrt(var + LN_EPS) * g + b


def _proj_in_body(x_ref, w_ref, o_ref, xb_ref, *, tn):
    j = pl.program_id(1)

    @pl.when(j == 0)
    def _():
        xb_ref[...] = x_ref[...].astype(BF16)

    acc = jnp.dot(xb_ref[...], w_ref[...], preferred_element_type=F32)
    col = j * tn
    is_gate = col >= P_GATES
    is_dq = jnp.logical_and(col >= P_DQ, col < P_DK)
    scale = jnp.where(is_dq, LOG2E * DIFF_HD ** -0.5, 1.0).astype(F32)

    @pl.when(is_gate)
    def _():
        o_ref[...] = _sigmoid(acc).astype(o_ref.dtype)

    @pl.when(jnp.logical_not(is_gate))
    def _():
        o_ref[...] = (acc * scale).astype(o_ref.dtype)


def _proj_in(x, w, *, tm=1024, tn=1024):
    n, k = x.shape
    m = w.shape[1]
    tm = min(tm, n)
    return pl.pallas_call(
        functools.partial(_proj_in_body, tn=tn),
        out_shape=jax.ShapeDtypeStruct((n, m), BF16),
        grid=(n // tm, m // tn),
        in_specs=[pl.BlockSpec((tm, k), lambda i, j: (i, 0)),
                  pl.BlockSpec((k, tn), lambda i, j: (0, j))],
        out_specs=pl.BlockSpec((tm, tn), lambda i, j: (i, j)),
        scratch_shapes=[pltpu.VMEM((tm, k), BF16)],
        compiler_params=_params(2),
        name="proj_in",
    )(x, w)


def _gates_body(x_ref, w_ref, prm_ref, o_ref, *, tt):
    ab = jnp.dot(x_ref[...].astype(BF16), w_ref[...], preferred_element_type=F32)
    a_log = prm_ref[0:1, :]
    dt_bias = prm_ref[1:2, :]
    y = ab + dt_bias
    softplus = jnp.maximum(y, 0.0) + jnp.log(1.0 + jnp.exp(-jnp.abs(y)))
    log_decay = -jnp.exp(a_log) * softplus
    beta = _sigmoid(ab)
    ri = lax.broadcasted_iota(jnp.int32, (tt, tt), 0)
    ci = lax.broadcasted_iota(jnp.int32, (tt, tt), 1)
    same = (ri // GDN_CHUNK) == (ci // GDN_CHUNK)
    m_lo = jnp.where(same, jnp.where(ci <= ri, 1.0, 0.0), 0.0).astype(F32)
    m_up = jnp.where(same, jnp.where(ci >= ri, 1.0, 0.0), 0.0).astype(F32)
    c_lo = jnp.dot(m_lo, log_decay, precision=HIGHEST, preferred_element_type=F32)
    c_up = jnp.dot(m_up, log_decay, precision=HIGHEST, preferred_element_type=F32)
    lane = lax.broadcasted_iota(jnp.int32, (tt, GDN_GATE_W), 1)
    is_a = (lane % (2 * GDN_V_HEADS)) < GDN_V_HEADS
    is_bwd = lane >= 2 * GDN_V_HEADS
    o_ref[...] = jnp.where(is_a, jnp.where(is_bwd, c_up, c_lo), beta)


def _gdn_gates(x, w_ab, a_log, dt_bias, *, tt=512):
    n, k = x.shape
    tt = min(tt, n)
    zeros = jnp.zeros((GDN_V_HEADS,), F32)
    prm = jnp.zeros((8, GDN_GATE_W), F32)
    prm = prm.at[0].set(jnp.concatenate([a_log[0], zeros, a_log[1], zeros]).astype(F32))
    prm = prm.at[1].set(jnp.concatenate([dt_bias[0], zeros, dt_bias[1], zeros]).astype(F32))
    return pl.pallas_call(
        functools.partial(_gates_body, tt=tt),
        out_shape=jax.ShapeDtypeStruct((n, GDN_GATE_W), F32),
        grid=(n // tt,),
        in_specs=[pl.BlockSpec((tt, k), lambda i: (i, 0)),
                  pl.BlockSpec((k, GDN_GATE_W), lambda i: (0, 0)),
                  pl.BlockSpec((8, GDN_GATE_W), lambda i: (0, 0))],
        out_specs=pl.BlockSpec((tt, GDN_GATE_W), lambda i: (i, 0)),
        compiler_params=_params(1),
        name="gdn_gates",
    )(x, w_ab, prm)


CONV_HALO = 16


def _any_equal(value, constants):
    hit = value == constants[0]
    for c in constants[1:]:
        hit = jnp.logical_or(hit, value == c)
    return hit


def _conv_body(prev_ref, cur_ref, next_ref, w_ref, o_ref, ext_ref, *, tt, tc, seq_starts, seq_ends):
    i = pl.program_id(0)
    j = pl.program_id(1)
    t0 = i * tt
    at_start = _any_equal(t0, seq_starts)
    at_end = _any_equal(t0 + tt, seq_ends)
    prev = prev_ref[...].astype(F32)[CONV_HALO - 8:, :]
    nxt = next_ref[...].astype(F32)[:8, :]
    ext_ref[0:8, :] = jnp.where(at_start, 0.0, prev)
    ext_ref[8:8 + tt, :] = cur_ref[...].astype(F32)
    ext_ref[8 + tt:16 + tt, :] = jnp.where(at_end, 0.0, nxt)
    pad = (GDN_CONV_K - 1) // 2
    acc = jnp.zeros((tt, tc), F32)
    for tap in range(GDN_CONV_K):
        acc = acc + ext_ref[pl.ds(8 - pad + tap, tt), :] * w_ref[tap:tap + 1, :]
    act = _silu(acc)
    col = j * tc
    is_qk = col < 2 * GDN_KEY_W
    q_scale = jnp.where(col < GDN_KEY_W, GDN_DK ** -0.5, 1.0).astype(F32)
    for s in range(tc // GDN_DK):
        a = act[:, s * GDN_DK:(s + 1) * GDN_DK]
        ss = jnp.sum(a * a, axis=-1, keepdims=True)
        normed = a * (lax.rsqrt(ss + RMS_EPS) * q_scale)
        o_ref[:, s * GDN_DK:(s + 1) * GDN_DK] = jnp.where(is_qk, normed, a).astype(o_ref.dtype)


def _gdn_conv(p, conv_w, seq_starts, seq_ends, *, tt=512, tc=512):
    n = p.shape[0]
    tt = min(tt, n)
    hb = tt // CONV_HALO
    n_halo = n // CONV_HALO
    return pl.pallas_call(
        functools.partial(_conv_body, tt=tt, tc=tc, seq_starts=seq_starts, seq_ends=seq_ends),
        out_shape=jax.ShapeDtypeStruct((n, GDN_CONV_CH), BF16),
        grid=(n // tt, GDN_CONV_CH // tc),
        in_specs=[pl.BlockSpec((CONV_HALO, tc), lambda i, j: (jnp.maximum(i * hb - 1, 0), j)),
                  pl.BlockSpec((tt, tc), lambda i, j: (i, j)),
                  pl.BlockSpec((CONV_HALO, tc), lambda i, j: (jnp.minimum((i + 1) * hb, n_halo - 1), j)),
                  pl.BlockSpec((GDN_CONV_K, tc), lambda i, j: (0, j))],
        out_specs=pl.BlockSpec((tt, tc), lambda i, j: (i, j)),
        scratch_shapes=[pltpu.VMEM((tt + 16, tc), F32)],
        compiler_params=_params(2),
        name="gdn_conv",
    )(p, p, p, conv_w)


def _dot_f32(a, b):
    return jnp.dot(a, b, precision=HIGHEST, preferred_element_type=F32)


def _dot_bf16(a, b):
    return jnp.dot(a.astype(BF16), b.astype(BF16), preferred_element_type=F32)


def _gdn_scan_body(*refs, tt, nt, hp, reverse, seq_starts, seq_ends):
    if reverse:
        q_ref, k_ref, v_ref, g_ref, b_ref, ofwd_ref, z_ref, nw_ref, o_ref, s_ref = refs
    else:
        q_ref, k_ref, v_ref, g_ref, b_ref, o_ref, s_ref = refs
    C = GDN_CHUNK
    t = pl.program_id(1)
    tok0 = ((nt - 1 - t) if reverse else t) * tt
    reset = _any_equal(tok0 + tt, seq_ends) if reverse else _any_equal(tok0, seq_starts)

    @pl.when(reset)
    def _():
        s_ref[...] = jnp.zeros_like(s_ref)

    n_chunks = tt // C
    ri = lax.broadcasted_iota(jnp.int32, (tt, tt), 0)
    ci = lax.broadcasted_iota(jnp.int32, (tt, tt), 1)
    same = (ri // C) == (ci // C)
    eye = ri == ci
    causal = jnp.logical_and(same, (ci >= ri) if reverse else (ci <= ri))
    strict = jnp.logical_and(same, (ci > ri) if reverse else (ci < ri))
    eye_f = jnp.where(eye, 1.0, 0.0).astype(F32)
    last = 0 if reverse else C - 1
    order = range(n_chunks - 1, -1, -1) if reverse else range(n_chunks)
    chunk_of_lane = lax.broadcasted_iota(jnp.int32, (1, tt), 1) // C
    chunk_of_lane_k = lax.broadcasted_iota(jnp.int32, (GDN_DK, tt), 1) // C

    shared = []
    for pp in range(hp):
        q = q_ref[:, pp * GDN_DK:(pp + 1) * GDN_DK]
        k = k_ref[:, pp * GDN_DK:(pp + 1) * GDN_DK]
        kf = k.astype(F32)
        kt = kf.T
        ktb = kt.astype(BF16)
        kq = jnp.dot(jnp.concatenate([k, q], axis=0), ktb, preferred_element_type=F32)
        shared.append((kf, q.astype(F32), kt, kq[:tt], kq[tt:]))
    per_head = [None] * (2 * hp)

    def head_stages(hh):
        kf, qf, kt, kk, qk = shared[hh // 2]
        cols = slice(hh * GDN_DV, (hh + 1) * GDN_DV)
        g_row = g_ref[hh]
        b_row = b_ref[hh]
        g_col = jnp.sum(jnp.where(eye, g_row, 0.0), axis=1, keepdims=True)
        b_col = jnp.sum(jnp.where(eye, b_row, 0.0), axis=1, keepdims=True)
        g_last = [g_row[:, c * C + last:c * C + last + 1] for c in range(n_chunks)]
        g_last_row = jnp.zeros((1, tt), F32)
        for c in range(n_chunks):
            g_last_row = jnp.where(chunk_of_lane == c, g_last[c], g_last_row)
        decay = jnp.exp(jnp.where(causal, g_col - g_row, NEG_BIG))
        low = jnp.where(strict, kk * decay, 0.0) * b_col
        low_b = low.astype(BF16)
        pw = jnp.dot(low_b, low_b, preferred_element_type=F32)
        yield
        inv = eye_f - low
        n_factors = int(math.log2(C)) - 1
        for j in range(n_factors):
            pw_b = pw.astype(BF16)
            if j < n_factors - 1:
                both = jnp.dot(jnp.concatenate([inv, pw], axis=0).astype(BF16), pw_b, preferred_element_type=F32)
                inv = inv + both[:tt]
                pw = both[tt:]
            else:
                inv = inv + jnp.dot(inv.astype(BF16), pw_b, preferred_element_type=F32)
            yield
        eg_col = jnp.exp(g_col)
        v = v_ref[:, cols].astype(F32)
        x = jnp.concatenate([v * b_col, kf * (b_col * eg_col)], axis=1)
        uw = jnp.dot(inv.astype(BF16), x.astype(BF16), preferred_element_type=F32).astype(BF16)
        yield
        from_q = jnp.dot((qk * decay).astype(BF16), uw, preferred_element_type=F32)
        yield
        ktd = kt * jnp.exp(g_last_row - g_row)
        lhs_k = jnp.concatenate([jnp.where(chunk_of_lane_k == c, ktd, 0.0) for c in range(n_chunks)],
                                axis=0).astype(BF16)
        from_k = jnp.dot(lhs_k, uw, preferred_element_type=F32)
        per_head[hh] = (qf * eg_col - from_q[:, GDN_DV:], from_q, from_k, g_last)
        yield

    for _ in itertools.zip_longest(*[head_stages(hh) for hh in range(2 * hp)]):
        pass

    for c in order:
        rows = slice(c * C, (c + 1) * C)
        krows = slice(c * GDN_DK, (c + 1) * GDN_DK)
        for pp in range(hp):
            heads = (2 * pp, 2 * pp + 1)
            lhs = jnp.concatenate([part for hh in heads
                                   for part in (per_head[hh][0][rows], per_head[hh][2][krows, GDN_DV:])],
                                  axis=0).astype(BF16)
            states = [s_ref[hh] for hh in heads]
            res = jnp.dot(lhs, jnp.concatenate(states, axis=1).astype(BF16), preferred_element_type=F32)
            for j, hh in enumerate(heads):
                _, from_q, from_k, g_last = per_head[hh]
                cols = slice(hh * GDN_DV, (hh + 1) * GDN_DV)
                blk = res[j * (C + GDN_DK):(j + 1) * (C + GDN_DK), j * GDN_DV:(j + 1) * GDN_DV]
                o = blk[:C] + from_q[rows, :GDN_DV]
                s_ref[hh] = states[j] * jnp.exp(g_last[c]) - blk[C:] + from_k[krows, :GDN_DV]
                if reverse:
                    tot = o + ofwd_ref[rows, cols]
                    ms = jnp.mean(tot * tot, axis=-1, keepdims=True)
                    z = z_ref[rows, cols].astype(F32)
                    o_ref[rows, cols] = (tot * lax.rsqrt(ms + RMS_EPS) * nw_ref[...] * _silu(z)).astype(o_ref.dtype)
                else:
                    o_ref[rows, cols] = o


def _gdn_scan(act, g_t, p, o_fwd, norm_w, seq_starts, seq_ends, *, reverse, tt=128, hp=8):
    n = act.shape[0]
    tt = min(tt, n)
    nt = n // tt
    groups = GDN_QK_HEADS // hp
    kw = hp * GDN_DK
    vw = 2 * hp * GDN_DV

    def tmap(t):
        return (nt - 1 - t) if reverse else t

    gate_blocks = (2 * GDN_V_HEADS) // (2 * hp)
    dirn = 1 if reverse else 0
    in_specs = [
        pl.BlockSpec((tt, kw), lambda h, t: (tmap(t), h)),
        pl.BlockSpec((tt, kw), lambda h, t: (tmap(t), groups + h)),
        pl.BlockSpec((tt, vw), lambda h, t: (tmap(t), 2 * GDN_KEY_W // vw + h)),
        pl.BlockSpec((2 * hp, 1, tt), lambda h, t: (dirn * gate_blocks + h, 0, tmap(t))),
        pl.BlockSpec((2 * hp, 1, tt), lambda h, t: (dirn * gate_blocks + groups + h, 0, tmap(t))),
    ]
    args = [act, act, act, g_t, g_t]
    if reverse:
        in_specs += [
            pl.BlockSpec((tt, vw), lambda h, t: (tmap(t), h)),
            pl.BlockSpec((tt, vw), lambda h, t: (tmap(t), P_Z // vw + h)),
            pl.BlockSpec((1, GDN_DV), lambda h, t: (0, 0)),
        ]
        args += [o_fwd, p, norm_w.reshape(1, GDN_DV).astype(F32)]
        out_dtype = BF16
    else:
        out_dtype = F32
    return pl.pallas_call(
        functools.partial(_gdn_scan_body, tt=tt, nt=nt, hp=hp, reverse=reverse,
                          seq_starts=seq_starts, seq_ends=seq_ends),
        out_shape=jax.ShapeDtypeStruct((n, GDN_VAL_W), out_dtype),
        grid=(groups, nt),
        in_specs=in_specs,
        out_specs=pl.BlockSpec((tt, vw), lambda h, t: (tmap(t), h)),
        scratch_shapes=[pltpu.VMEM((2 * hp, GDN_DK, GDN_DV), F32)],
        compiler_params=_params(2),
        name="gdn_scan_bwd" if reverse else "gdn_scan_fwd",
    )(*args)


def _attn_body(slope_ref, lam_ref, sw_ref, q_ref, k_ref, v_ref, o_ref, m_ref, l_ref, acc_ref, sd_ref,
               *, tile, nk, lam_init):
    h = pl.program_id(1)
    qi = pl.program_id(2)
    kj = pl.program_id(3)
    slope = slope_ref[h]

    n_sub = m_ref.shape[1]
    sub = tile // n_sub

    @pl.when(kj == 0)
    def _():
        m_ref[...] = jnp.full_like(m_ref, NEG_BIG)
        l_ref[...] = jnp.zeros_like(l_ref)
        acc_ref[...] = jnp.zeros_like(acc_ref)
        for r in range(n_sub):
            ii = lax.broadcasted_iota(jnp.int32, (sub, tile), 0) + r * sub
            jj = lax.broadcasted_iota(jnp.int32, (sub, tile), 1)
            sd_ref[r] = (ii - jj).astype(F32) * slope

    v = v_ref[...]
    tile_dist = jnp.full((sub, 1), (qi - kj) * tile, jnp.int32).astype(F32) * slope

    def update(add_bias, shift):
        def stages(r, m):
            rows = slice(r * sub, (r + 1) * sub)
            cols = slice(m * DIFF_HD, (m + 1) * DIFF_HD)
            t = add_bias(lax.dot_general(q_ref[rows, cols], k_ref[:, cols], (((1,), (1,)), ((), ())),
                                         preferred_element_type=F32), r)
            yield
            m_old = m_ref[m, r]
            m_new = jnp.maximum(m_old, jnp.max(t, axis=1, keepdims=True) - shift)
            alpha = jnp.exp2(m_old - m_new)
            m_ref[m, r] = m_new
            yield
            p = jnp.exp2(t - (m_new + shift))
            l_ref[m, r] = alpha * l_ref[m, r] + jnp.sum(p, axis=1, keepdims=True)
            p_b = p.astype(BF16)
            yield
            acc_ref[m, r] = alpha * acc_ref[m, r] + jnp.dot(p_b, v, preferred_element_type=F32)
            yield

        waiting = [stages(r, m) for r in range(n_sub) for m in range(2)]
        running = []
        step = 0
        while waiting or running:
            if waiting and step % ATTN_STAGGER == 0:
                running.append(waiting.pop(0))
            running = [g for g in running if next(g, "done") != "done"]
            step += 1

    @pl.when(kj < qi)
    def _():
        update(lambda s, r: s - sd_ref[r], tile_dist)

    @pl.when(kj > qi)
    def _():
        update(lambda s, r: s + sd_ref[r], -tile_dist)

    @pl.when(kj == qi)
    def _():
        update(lambda s, r: s - jnp.abs(sd_ref[r]), jnp.zeros((sub, 1), F32))

    @pl.when(kj == nk - 1)
    def _():
        lam = lam_ref[...]
        lam_full = (jnp.exp(jnp.sum(lam[0:1, :] * lam[1:2, :], axis=-1, keepdims=True))
                    - jnp.exp(jnp.sum(lam[2:3, :] * lam[3:4, :], axis=-1, keepdims=True)) + lam_init)
        for r in range(n_sub):
            o = acc_ref[0, r] / l_ref[0, r] - lam_full * (acc_ref[1, r] / l_ref[1, r])
            ms = jnp.mean(o * o, axis=-1, keepdims=True)
            o_ref[r * sub:(r + 1) * sub, :] = (o * lax.rsqrt(ms + LN_EPS) * sw_ref[...]
                                               * (1.0 - lam_init)).astype(o_ref.dtype)


def _diff_attention(p, lam, subln_w, tok_off, n_seq, seq_len, layer_idx, *, tile=1024, sub=512):
    tile = min(tile, seq_len)
    sub = min(sub, tile)
    n_sub = tile // sub
    nt = seq_len // tile
    lam_init = 0.8 - 0.6 * math.exp(-0.3 * layer_idx)
    slopes = jnp.asarray([LOG2E * 2.0 ** (-8.0 * (i + 1) / DIFF_HEADS) for i in range(DIFF_HEADS)], F32)
    w2 = 2 * DIFF_HD
    off = tok_off // tile
    return pl.pallas_call(
        functools.partial(_attn_body, tile=tile, nk=nt, lam_init=lam_init),
        out_shape=jax.ShapeDtypeStruct((n_seq * seq_len, DIFF_VAL_W), BF16),
        grid=(n_seq, DIFF_HEADS, nt, nt),
        in_specs=[pl.BlockSpec(memory_space=pltpu.SMEM),
                  pl.BlockSpec((4, DIFF_HD), lambda b, h, i, j: (0, 0)),
                  pl.BlockSpec((1, w2), lambda b, h, i, j: (0, 0)),
                  pl.BlockSpec((tile, w2), lambda b, h, i, j: (off + b * nt + i, P_DQ // w2 + h)),
                  pl.BlockSpec((tile, w2), lambda b, h, i, j: (off + b * nt + j, P_DK // w2 + h)),
                  pl.BlockSpec((tile, w2), lambda b, h, i, j: (off + b * nt + j, P_DV // w2 + h))],
        out_specs=pl.BlockSpec((tile, w2), lambda b, h, i, j: (b * nt + i, h)),
        scratch_shapes=[pltpu.VMEM((2, n_sub, sub, 1), F32), pltpu.VMEM((2, n_sub, sub, 1), F32),
                        pltpu.VMEM((2, n_sub, sub, w2), F32), pltpu.VMEM((n_sub, sub, tile), F32)],
        compiler_params=_params(4),
        name="diff_attention",
    )(slopes, lam.astype(F32), subln_w.reshape(1, w2).astype(F32), p, p, p)


def _merge_body(og_ref, od_ref, wg_ref, wd_ref, gg_ref, gd_ref, o_ref):
    a = jnp.dot(og_ref[...], wg_ref[...], preferred_element_type=F32)
    b = jnp.dot(od_ref[...], wd_ref[...], preferred_element_type=F32)
    o_ref[...] = (gg_ref[...].astype(F32) * a + gd_ref[...].astype(F32) * b).astype(o_ref.dtype)


def _merge(o_gdn, o_diff, w_g, w_d, p, *, tm=512, tn=512):
    n = o_gdn.shape[0]
    tm = min(tm, n)
    return pl.pallas_call(
        _merge_body,
        out_shape=jax.ShapeDtypeStruct((n, D_MODEL), BF16),
        grid=(n // tm, D_MODEL // tn),
        in_specs=[pl.BlockSpec((tm, GDN_VAL_W), lambda i, j: (i, 0)),
                  pl.BlockSpec((tm, DIFF_VAL_W), lambda i, j: (i, 0)),
                  pl.BlockSpec((GDN_VAL_W, tn), lambda i, j: (0, j)),
                  pl.BlockSpec((DIFF_VAL_W, tn), lambda i, j: (0, j)),
                  pl.BlockSpec((tm, tn), lambda i, j: (i, P_GATES // tn + j)),
                  pl.BlockSpec((tm, tn), lambda i, j: (i, (P_GATES + D_MODEL) // tn + j))],
        out_specs=pl.BlockSpec((tm, tn), lambda i, j: (i, j)),
        compiler_params=_params(2),
        name="branch_merge",
    )(o_gdn, o_diff, w_g, w_d, p, p)


def _store_slabs(slab_ref, value, n_rows):
    for s in range(SLAB):
        slab_ref[pl.ds(s, n_rows, stride=SLAB), :] = value[:, s * 128:(s + 1) * 128]


def _load_slabs(slab_ref, n_rows, first=0, stride=SLAB):
    return jnp.concatenate([slab_ref[pl.ds(first + s, n_rows, stride=stride), :] for s in range(SLAB)], axis=1)


def _out_ln_body(m_ref, w_ref, x_ref, g_ref, b_ref, o_ref, slab_ref, *, tm):
    mix = jnp.dot(m_ref[...], w_ref[...], preferred_element_type=F32)
    h = _layer_norm(DN_ALPHA * x_ref[...] + mix, g_ref[...], b_ref[...])
    o_ref[...] = h
    _store_slabs(slab_ref, h, tm)


def _out_ln(merged, w_out, x, g, b, *, tm=512):
    n = x.shape[0]
    tm = min(tm, n)
    return pl.pallas_call(
        functools.partial(_out_ln_body, tm=tm),
        out_shape=(jax.ShapeDtypeStruct((n, D_MODEL), F32), jax.ShapeDtypeStruct((n * SLAB, 128), F32)),
        grid=(n // tm,),
        in_specs=[pl.BlockSpec((tm, D_MODEL), lambda i: (i, 0)),
                  pl.BlockSpec((D_MODEL, D_MODEL), lambda i: (0, 0)),
                  pl.BlockSpec((tm, D_MODEL), lambda i: (i, 0)),
                  pl.BlockSpec((1, D_MODEL), lambda i: (0, 0)),
                  pl.BlockSpec((1, D_MODEL), lambda i: (0, 0))],
        out_specs=(pl.BlockSpec((tm, D_MODEL), lambda i: (i, 0)), pl.BlockSpec((tm * SLAB, 128), lambda i: (i, 0))),
        compiler_params=_params(1),
        name="out_proj_ln1",
    )(merged, w_out, x, g.reshape(1, D_MODEL).astype(F32), b.reshape(1, D_MODEL).astype(F32))


def _first_argmax(x, row_ids, n_rows):
    mx = jnp.max(x, axis=0, keepdims=True)
    idx = jnp.min(jnp.where(x == mx, row_ids, n_rows), axis=0, keepdims=True)
    return mx, idx


def _router_body(h_ref, w_ref, bias_ref, e_ref, g_ref, *, n_exp, tm):
    per = n_exp // N_GROUPS
    logits = lax.dot_general(w_ref[...], h_ref[...], (((1,), (1,)), ((), ())),
                             precision=HIGHEST, preferred_element_type=F32)
    scores = _sigmoid(logits)
    choice = scores + bias_ref[...]
    ids_g = lax.broadcasted_iota(jnp.int32, (per, tm), 0)
    group_rows = []
    for g in range(N_GROUPS):
        xg = choice[g * per:(g + 1) * per, :]
        m1, i1 = _first_argmax(xg, ids_g, per)
        m2 = jnp.max(jnp.where(ids_g == i1, -jnp.inf, xg), axis=0, keepdims=True)
        group_rows.append(m1 + m2)
    gs = jnp.concatenate(group_rows, axis=0)
    ids_n = lax.broadcasted_iota(jnp.int32, (N_GROUPS, tm), 0)
    keep = jnp.zeros((N_GROUPS, tm), F32)
    for _ in range(TOPK_GROUPS):
        _, gi = _first_argmax(gs, ids_n, N_GROUPS)
        hit = ids_n == gi
        keep = jnp.where(hit, 1.0, keep)
        gs = jnp.where(hit, -jnp.inf, gs)
    masked = jnp.concatenate(
        [jnp.where(keep[g:g + 1, :] > 0.5, choice[g * per:(g + 1) * per, :], -jnp.inf)
         for g in range(N_GROUPS)], axis=0)
    ids_e = lax.broadcasted_iota(jnp.int32, (n_exp, tm), 0)
    top_idx = []
    top_w = []
    for _ in range(TOP_K):
        _, ei = _first_argmax(masked, ids_e, n_exp)
        hit = ids_e == ei
        top_idx.append(ei)
        top_w.append(jnp.sum(jnp.where(hit, scores, 0.0), axis=0, keepdims=True))
        masked = jnp.where(hit, -jnp.inf, masked)
    gw = jnp.concatenate(top_w, axis=0)
    gw = gw / (jnp.sum(gw, axis=0, keepdims=True) + 1e-20) * ROUTED_SCALE
    e_ref[...] = jnp.concatenate(top_idx, axis=0)
    g_ref[...] = gw


def _router(h, router_w, router_bias, *, tm=256):
    n = h.shape[0]
    n_exp = router_w.shape[1]
    tm = min(tm, n)
    return pl.pallas_call(
        functools.partial(_router_body, n_exp=n_exp, tm=tm),
        out_shape=(jax.ShapeDtypeStruct((TOP_K, n), jnp.int32), jax.ShapeDtypeStruct((TOP_K, n), F32)),
        grid=(n // tm,),
        in_specs=[pl.BlockSpec((tm, D_MODEL), lambda i: (i, 0)),
                  pl.BlockSpec((n_exp, D_MODEL), lambda i: (0, 0)),
                  pl.BlockSpec((n_exp, 1), lambda i: (0, 0))],
        out_specs=(pl.BlockSpec((TOP_K, tm), lambda i: (0, i)), pl.BlockSpec((TOP_K, tm), lambda i: (0, i))),
        compiler_params=_params(1),
        name="moe_router",
    )(h, router_w.T.astype(F32), router_bias.reshape(n_exp, 1).astype(F32))


def _expert_body(be_ref, nu_ref, tok_ref, tok_next_ref, dst_ref, h_ref, wg_ref, wu_ref, wd_ref, y_ref,
                 x_buf, y_buf, wgb_ref, wub_ref, wdb_ref, gather_sem, scatter_sem, *, tb, n_blocks):
    i = pl.program_id(0)
    n_used = nu_ref[0]
    slot = i % 2

    def start_gather(rows_ref, buf):
        def issue(pair, carry):
            for queue in range(2):
                r = 2 * pair + queue
                src = pl.multiple_of(rows_ref[0, 0, r] * SLAB, SLAB)
                pltpu.make_async_copy(h_ref.at[pl.ds(src, SLAB)], x_buf.at[buf, pl.ds(r * SLAB, SLAB)],
                                      gather_sem.at[buf]).start(priority=queue)
            return carry
        lax.fori_loop(0, tb // 2, issue, 0, unroll=4)

    def wait_buffer(buf_ref, sem_ref, buf):
        pltpu.make_async_copy(buf_ref.at[buf], buf_ref.at[buf], sem_ref.at[buf]).wait()

    @pl.when(jnp.logical_and(i == 0, n_used > 0))
    def _():
        start_gather(tok_ref, 0)

    @pl.when(i + 1 < n_used)
    def _():
        start_gather(tok_next_ref, 1 - slot)

    @pl.when(i < n_used)
    def _():
        fresh = jnp.logical_or(i == 0, be_ref[i] != be_ref[jnp.maximum(i - 1, 0)])

        @pl.when(fresh)
        def _():
            wgb_ref[...] = wg_ref[...].astype(BF16)
            wub_ref[...] = wu_ref[...].astype(BF16)
            wdb_ref[...] = wd_ref[...].astype(BF16)

        wait_buffer(x_buf, gather_sem, slot)

        @pl.when(i >= 2)
        def _():
            wait_buffer(y_buf, scatter_sem, slot)

        xb = _load_slabs(x_buf.at[slot], tb).astype(BF16)
        gate = jnp.dot(xb, wgb_ref[...], preferred_element_type=F32)
        up = jnp.dot(xb, wub_ref[...], preferred_element_type=F32)
        act = (_silu(gate) * up).astype(BF16)
        _store_slabs(y_buf.at[slot], jnp.dot(act, wdb_ref[...], preferred_element_type=F32), tb)

        def issue(pair, carry):
            for queue in range(2):
                r = 2 * pair + queue
                dst = pl.multiple_of(dst_ref[0, 0, r] * SLAB, SLAB)
                pltpu.make_async_copy(y_buf.at[slot, pl.ds(r * SLAB, SLAB)], y_ref.at[pl.ds(dst, SLAB)],
                                      scatter_sem.at[slot]).start(priority=queue)
            return carry
        lax.fori_loop(0, tb // 2, issue, 0, unroll=4)

    @pl.when(i == n_blocks - 1)
    def _():
        @pl.when(n_used >= 1)
        def _():
            wait_buffer(y_buf, scatter_sem, (n_used - 1) % 2)

        @pl.when(n_used >= 2)
        def _():
            wait_buffer(y_buf, scatter_sem, n_used % 2)


def _experts(h_slab, slot_tok, slot_dst, block_e, n_used, w_gate, w_up, w_down, *, tb):
    n_slots = slot_tok.shape[0]
    n_blocks = n_slots // tb
    ff = w_gate.shape[2]
    idx_shape = (n_blocks, 1, tb)
    grid_spec = pltpu.PrefetchScalarGridSpec(
        num_scalar_prefetch=2,
        grid=(n_blocks,),
        in_specs=[pl.BlockSpec((1, 1, tb), lambda i, be, nu: (i, 0, 0), memory_space=pltpu.SMEM),
                  pl.BlockSpec((1, 1, tb), lambda i, be, nu: (jnp.minimum(i + 1, n_blocks - 1), 0, 0),
                               memory_space=pltpu.SMEM),
                  pl.BlockSpec((1, 1, tb), lambda i, be, nu: (i, 0, 0), memory_space=pltpu.SMEM),
                  pl.BlockSpec(memory_space=pl.ANY),
                  pl.BlockSpec((None, D_MODEL, ff), lambda i, be, nu: (be[i], 0, 0)),
                  pl.BlockSpec((None, D_MODEL, ff), lambda i, be, nu: (be[i], 0, 0)),
                  pl.BlockSpec((None, ff, D_MODEL), lambda i, be, nu: (be[i], 0, 0))],
        out_specs=pl.BlockSpec(memory_space=pl.ANY),
        scratch_shapes=[pltpu.VMEM((2, tb * SLAB, 128), F32), pltpu.VMEM((2, tb * SLAB, 128), F32),
                        pltpu.VMEM((D_MODEL, ff), BF16), pltpu.VMEM((D_MODEL, ff), BF16),
                        pltpu.VMEM((ff, D_MODEL), BF16),
                        pltpu.SemaphoreType.DMA((2,)), pltpu.SemaphoreType.DMA((2,))],
    )
    tok3 = slot_tok.reshape(idx_shape)
    return pl.pallas_call(
        functools.partial(_expert_body, tb=tb, n_blocks=n_blocks),
        out_shape=jax.ShapeDtypeStruct((n_slots * SLAB, 128), F32),
        grid_spec=grid_spec,
        compiler_params=_params(1),
        name="moe_experts",
    )(block_e, n_used, tok3, tok3, slot_dst.reshape(idx_shape), h_slab, w_gate, w_up, w_down)


def _combine_body(h_ref, y_ref, gw_ref, sgu_ref, sd_ref, g_ref, b_ref, o_ref, *, ff, tm):
    h = h_ref[...]
    hb = h.astype(BF16)
    gu = jnp.dot(hb, sgu_ref[...], preferred_element_type=F32)
    act = (_silu(gu[:, :ff]) * gu[:, ff:]).astype(BF16)
    f = jnp.dot(act, sd_ref[...], preferred_element_type=F32)
    gw = gw_ref[...]
    for k in range(TOP_K):
        f = f + _load_slabs(y_ref, tm, first=k * SLAB, stride=TOP_K * SLAB) * gw[:, k:k + 1]
    o_ref[...] = _layer_norm(DN_ALPHA * h + f, g_ref[...], b_ref[...])


def _combine(h, y_tok, gw_t, sh_gu, sh_down, g, b, tok_off, n_rows, *, tm=128):
    ff = sh_down.shape[0]
    tm = min(tm, n_rows)
    off = tok_off // tm
    return pl.pallas_call(
        functools.partial(_combine_body, ff=ff, tm=tm),
        out_shape=jax.ShapeDtypeStruct((n_rows, D_MODEL), F32),
        grid=(n_rows // tm,),
        in_specs=[pl.BlockSpec((tm, D_MODEL), lambda i: (off + i, 0)),
                  pl.BlockSpec((tm * TOP_K * SLAB, 128), lambda i: (off + i, 0)),
                  pl.BlockSpec((tm, TOP_K), lambda i: (off + i, 0)),
                  pl.BlockSpec((D_MODEL, 2 * ff), lambda i: (0, 0)),
                  pl.BlockSpec((ff, D_MODEL), lambda i: (0, 0)),
                  pl.BlockSpec((1, D_MODEL), lambda i: (0, 0)),
                  pl.BlockSpec((1, D_MODEL), lambda i: (0, 0))],
        out_specs=pl.BlockSpec((tm, D_MODEL), lambda i: (i, 0)),
        compiler_params=_params(1),
        name="moe_combine_ln2",
    )(h, y_tok, gw_t, sh_gu, sh_down, g.reshape(1, D_MODEL).astype(F32), b.reshape(1, D_MODEL).astype(F32))


def _routing_tables(top_e, n_exp, tb):
    k, n = top_e.shape
    a = k * n
    n_blocks = a // tb + n_exp
    a_bits = max(a - 1, 1).bit_length()
    assert n_exp << a_bits < 2 ** 31
    ids = jnp.arange(a, dtype=jnp.int32)
    keys = lax.sort(top_e.reshape(-1) * (1 << a_bits) + ids)
    e_sorted = keys >> a_bits
    a_sorted = keys & ((1 << a_bits) - 1)
    tok_sorted = a_sorted % n
    k_sorted = a_sorted // n
    experts = jnp.arange(n_exp, dtype=jnp.int32)
    start = jnp.sum((e_sorted[:, None] < experts[None, :]).astype(jnp.int32), axis=0)
    counts = jnp.concatenate([start[1:], jnp.full((1,), a, jnp.int32)]) - start
    padded = (counts + tb - 1) // tb * tb
    pad_end = jnp.cumsum(padded)
    shift = pad_end - padded - start
    dest = ids + jnp.sum(jnp.where(e_sorted[:, None] == experts[None, :], shift[None, :], 0), axis=1)
    slot_row = jnp.full((n_blocks * tb,), -1, jnp.int32).at[dest].set(tok_sorted * k + k_sorted)
    is_pad = slot_row < 0
    pad_rank = jnp.cumsum(is_pad.astype(jnp.int32)) - 1
    slot_tok = jnp.where(is_pad, 0, slot_row // k)
    slot_dst = jnp.where(is_pad, a + pad_rank, slot_row)
    block_start = jnp.arange(n_blocks, dtype=jnp.int32) * tb
    block_e = jnp.minimum(jnp.sum(block_start[:, None] >= pad_end[None, :], axis=1), n_exp - 1).astype(jnp.int32)
    n_used = (pad_end[-1] // tb).astype(jnp.int32).reshape(1)
    block_e = jnp.where(block_start < pad_end[-1], block_e, block_e[jnp.maximum(n_used[0] - 1, 0)])
    return slot_tok, slot_dst, block_e, n_used


def _layer(xs, layer_idx, w_in, conv_w, a_log, dt_bias, norm_w, lam, subln_w, w_branch_gdn, w_branch_diff,
           w_out, ln1_g, ln1_b, router_w, router_bias, w_gate, w_up, w_down, sh_gate, sh_up, sh_down,
           ln2_g, ln2_b, *, moe_tb=256):
    seqs = [(x.shape[0], x.shape[1]) for x in xs]
    x = jnp.concatenate([x.reshape(-1, D_MODEL) for x in xs], axis=0)
    n = x.shape[0]
    seq_starts, seq_ends, group_off = [], [], []
    off = 0
    for b, t in seqs:
        group_off.append(off)
        for _ in range(b):
            seq_starts.append(off)
            off += t
            seq_ends.append(off)
    seq_starts, seq_ends = tuple(seq_starts), tuple(seq_ends)

    o_qkv, o_z, o_ab = GDN_CONV_CH, GDN_CONV_CH + GDN_VAL_W, GDN_CONV_CH + GDN_VAL_W + GDN_GATE_W
    w_main = jnp.concatenate([w_in[:, :o_z], w_in[:, o_ab:]], axis=1).astype(BF16)
    w_ab = w_in[:, o_z:o_ab].astype(BF16)

    p = _proj_in(x, w_main)
    gates = _gdn_gates(x, w_ab, a_log, dt_bias)
    g_t = gates.T.reshape(GDN_GATE_W, 1, n)
    act = _gdn_conv(p, conv_w.astype(F32), seq_starts, seq_ends)
    o_fwd = _gdn_scan(act, g_t, p, None, norm_w, seq_starts, seq_ends, reverse=False)
    o_gdn = _gdn_scan(act, g_t, p, o_fwd, norm_w, seq_starts, seq_ends, reverse=True)
    o_diff = jnp.concatenate(
        [_diff_attention(p, lam, subln_w, goff, b, t, layer_idx) for (b, t), goff in zip(seqs, group_off)], axis=0)
    merged = _merge(o_gdn, o_diff, w_branch_gdn.astype(BF16), w_branch_diff.astype(BF16), p)
    h, h_slab = _out_ln(merged, w_out.astype(BF16), x, ln1_g, ln1_b)

    n_exp = router_w.shape[1]
    top_e, gw = _router(h, router_w, router_bias)
    slot_tok, slot_dst, block_e, n_used = _routing_tables(top_e, n_exp, moe_tb)
    y_tok = _experts(h_slab, slot_tok, slot_dst, block_e, n_used, w_gate, w_up, w_down, tb=moe_tb)
    sh_gu = jnp.concatenate([sh_gate, sh_up], axis=1).astype(BF16)
    outs = []
    for (b, t), goff in zip(seqs, group_off):
        y = _combine(h, y_tok, gw.T, sh_gu, sh_down.astype(BF16), ln2_g, ln2_b, goff, b * t)
        outs.append(y.reshape(b, t, D_MODEL))
    return outs


def kernel(x_prompt, x_sample, w_in, gdn_conv_w, gdn_a_log, gdn_dt_bias, gdn_norm_w, diff_lambda, diff_subln_w, w_branch_gdn, w_branch_diff, w_out, ln1_g, ln1_b, router_w, router_bias, exp_w_gate, exp_w_up, exp_w_down, sh_w_gate, sh_w_up, sh_w_down, ln2_g, ln2_b):
    xs = [x_prompt, x_sample]
    for l in range(DEPTH):
        xs = _layer(xs, l, w_in[l], gdn_conv_w[l], gdn_a_log[l], gdn_dt_bias[l], gdn_norm_w[l], diff_lambda[l],
                    diff_subln_w[l], w_branch_gdn[l], w_branch_diff[l], w_out[l], ln1_g[l], ln1_b[l], router_w[l],
                    router_bias[l], exp_w_gate[l], exp_w_up[l], exp_w_down[l], sh_w_gate[l], sh_w_up[l],
                    sh_w_down[l], ln2_g[l], ln2_b[l])
    return (xs[0], xs[1])
```
